```python
import math
import jax, jax.numpy as jnp
from jax import lax
import numpy as np

D_MODEL = 2048
BATCH = 2
SEQ = 8192
DEPTH = 1
DEC_BATCH = 8
DEC_SEQ = 32
PAST_LEN = 4096

CHUNK = 64
QBLOCK = 128
D_PLE = 256
EPS = 1e-6
NH_A = 4
DQK_A = 128
DV_A = 256
QK_A = NH_A * DQK_A
D_A = NH_A * DV_A
CONV_W = 4
NH_B = 8
DH_B = 128
D_B = NH_B * DH_B
N_GROUPS = 4
E_PER_GROUP = 8
TOP_K = 2
D_EXPERT = 512
COL_SIZES = (QK_A, QK_A, D_A, NH_A, NH_A, D_A, D_B, D_B, D_B, D_MODEL, D_MODEL)
D_IN = sum(COL_SIZES)

kernel_name = 'hybrid_mlstm_stickbreak_hmoe_step'


def rmsnorm(x, g):
    xf = x.astype(jnp.float32)
    y = xf * lax.rsqrt(jnp.mean(xf * xf, axis=-1, keepdims=True) + EPS)
    return (y * g.astype(jnp.float32)).astype(x.dtype)


def split_cols(proj):
    idx = np.cumsum(COL_SIZES)[:-1].tolist()
    return jnp.split(proj, idx, axis=-1)


def causal_conv(x, buf, w, b):
    L = x.shape[1]
    xp = jnp.concatenate([buf.astype(x.dtype), x], axis=1)
    y = b
    for j in range(CONV_W):
        y = y + xp[:, j:j + L] * w[j]
    return y, xp[:, L:]


def mlstm(q, k, v, ig, lf, C0, n0, m0):
    N, L = q.shape[:2]
    clen = min(L, CHUNK)
    nc = L // clen
    f32 = jnp.float32

    def chunks(a):
        return a.reshape((N, nc, clen) + a.shape[2:]).swapaxes(0, 1)

    causal = jnp.tril(jnp.ones((clen, clen), dtype=bool))

    def step(carry, xs):
        C, n, m = carry
        qc, kc, vc, ic, fc = xs
        b = jnp.cumsum(fc, axis=1).transpose(0, 2, 1)
        i_ = ic.transpose(0, 2, 1)
        dmat = jnp.where(causal, b[..., :, None] - b[..., None, :] + i_[..., None, :], -jnp.inf)
        inter = b + m[..., None]
        m_t = jnp.maximum(inter, jnp.max(dmat, axis=-1))
        s = jnp.einsum('nlhd,nshd->nhls', qc, kc) * jnp.exp(dmat - m_t[..., None])
        w_inter = jnp.exp(inter - m_t)
        num = jnp.einsum('nhls,nshv->nhlv', s, vc) + w_inter[..., None] * jnp.einsum('nlhd,nhdv->nhlv', qc, C)
        den = jnp.sum(s, axis=-1) + w_inter * jnp.einsum('nlhd,nhd->nhl', qc, n)
        h = num / jnp.maximum(jnp.abs(den), jnp.exp(-m_t))[..., None]
        b_end = b[..., -1]
        dec = b_end[..., None] - b + i_
        m_new = jnp.maximum(b_end + m, jnp.max(dec, axis=-1))
        wk = jnp.exp(dec - m_new[..., None])
        w_old = jnp.exp(b_end + m - m_new)
        C_new = w_old[..., None, None] * C + jnp.einsum('nhs,nshd,nshv->nhdv', wk, kc, vc)
        n_new = w_old[..., None] * n + jnp.einsum('nhs,nshd->nhd', wk, kc)
        return (C_new, n_new, m_new), h.transpose(0, 2, 1, 3)

    carry0 = (C0.astype(f32), n0.astype(f32), m0.astype(f32))
    xs = tuple(chunks(a.astype(f32)) for a in (q, k, v, ig, lf))
    (C1, n1, m1), hs = lax.scan(step, carry0, xs)
    h = hs.swapaxes(0, 1).reshape(N, L, NH_A, DV_A)
    return h, C1, n1, m1


def sb_block(qb, qpos, k, v, kpos):
    z = jnp.einsum('nqhd,nkhd->nhqk', qb, k).astype(jnp.float32) * (DH_B ** -0.5)
    mask = kpos[None, :] < qpos[:, None]
    l1mb = jnp.where(mask, jax.nn.log_sigmoid(-z), 0.0)
    rest = lax.cumsum(l1mb, axis=3, reverse=True) - l1mb
    a = jnp.where(mask, jnp.exp(jax.nn.log_sigmoid(z) + rest), 0.0)
    return jnp.einsum('nhqk,nkhd->nqhd', a.astype(v.dtype), v)


def sb_attend(q, k, v, q_off):
    N, Lq = q.shape[:2]
    Lk = k.shape[1]
    kpos = jnp.arange(Lk, dtype=jnp.int32)
    qpos = q_off + jnp.arange(Lq, dtype=jnp.int32)
    if Lq <= QBLOCK:
        return sb_block(q, qpos, k, v, kpos)
    nb = Lq // QBLOCK
    qs = q.reshape(N, nb, QBLOCK, NH_B, DH_B).swapaxes(0, 1)
    qp = qpos.reshape(nb, QBLOCK)
    out = lax.map(lambda a: sb_block(a[0], a[1], k, v, kpos), (qs, qp))
    return out.swapaxes(0, 1).reshape(N, Lq, NH_B, DH_B)


def token_mixer(h, conv_buf, C0, n0, m0, k_past, v_past,
                w_in, b_if, conv_w, conv_b, g_head_a, w_proj_a, w_proj_b, w_out):
    N, L, _ = h.shape
    proj = h @ w_in
    qa, ka, va, ig, fg, oa, qb, kb, vb, ga, gb = split_cols(proj)
    qk_c, new_buf = causal_conv(jnp.concatenate([qa, ka], axis=-1), conv_buf, conv_w, conv_b)
    qk_c = jax.nn.silu(qk_c)
    q_a = qk_c[..., :QK_A].reshape(N, L, NH_A, DQK_A)
    k_a = qk_c[..., QK_A:].reshape(N, L, NH_A, DQK_A) * (DQK_A ** -0.5)
    v_a = va.reshape(N, L, NH_A, DV_A)
    gates = jnp.concatenate([ig, fg], axis=-1).astype(jnp.float32) + b_if.astype(jnp.float32)
    log_i = gates[..., :NH_A]
    log_f = jax.nn.log_sigmoid(gates[..., NH_A:])
    h_a, C1, n1, m1 = mlstm(q_a, k_a, v_a, log_i, log_f, C0, n0, m0)
    h_a = h_a * lax.rsqrt(jnp.mean(h_a * h_a, axis=-1, keepdims=True) + EPS)
    h_a = (h_a.reshape(N, L, D_A) * g_head_a.astype(jnp.float32)).astype(h.dtype)
    h_a = jax.nn.sigmoid(oa) * h_a
    q_b = qb.reshape(N, L, NH_B, DH_B)
    k_b = kb.reshape(N, L, NH_B, DH_B)
    v_b = vb.reshape(N, L, NH_B, DH_B)
    k_all = jnp.concatenate([k_past.astype(k_b.dtype), k_b], axis=1)
    v_all = jnp.concatenate([v_past.astype(v_b.dtype), v_b], axis=1)
    h_b = sb_attend(q_b, k_all, v_all, k_past.shape[1]).reshape(N, L, D_B)
    merged = jax.nn.sigmoid(ga) * (h_a @ w_proj_a) + jax.nn.sigmoid(gb) * (h_b @ w_proj_b)
    return merged @ w_out, k_b, v_b, new_buf, C1, n1, m1


def hier_moe(h, w_rg, b_rg, w_re, b_re, w_eg, w_eu, w_ed):
    N, L, D = h.shape
    t = h.reshape(N * L, D)
    f32 = jnp.float32
    gl = (t @ w_rg).astype(f32) + b_rg.astype(f32)
    pg = jax.nn.softmax(gl, axis=-1)
    _, gsel = lax.top_k(gl, 1)
    g_onehot = jax.nn.one_hot(gsel[:, 0], N_GROUPS, dtype=f32)
    p_sel = jnp.sum(pg * g_onehot, axis=-1)
    el = jnp.einsum('td,gde->tge', t, w_re).astype(f32) + b_re.astype(f32)
    el_sel = jnp.einsum('tge,tg->te', el, g_onehot)
    pe = jax.nn.softmax(el_sel, axis=-1)
    wtop, etop = lax.top_k(pe, TOP_K)
    wtop = wtop / jnp.sum(wtop, axis=-1, keepdims=True)
    e_w = jnp.einsum('tke,tk->te', jax.nn.one_hot(etop, E_PER_GROUP, dtype=f32), wtop)
    comb = (g_onehot[:, :, None] * e_w[:, None, :] * p_sel[:, None, None]).astype(t.dtype)
    y = jnp.zeros_like(t)
    for g in range(N_GROUPS):
        a = jnp.einsum('td,edf->tef', t, w_eg[g])
        u = jnp.einsum('td,edf->tef', t, w_eu[g])
        hid = jax.nn.silu(a) * u * comb[:, g, :, None]
        y = y + jnp.einsum('tef,efd->td', hid, w_ed[g])
    return y.reshape(N, L, D)


def layer(x, p, conv_buf, C0, n0, m0, k_past, v_past,
          g_mix, w_in, b_if, conv_w, conv_b, g_head_a, w_proj_a, w_proj_b, w_out,
          g_ffn, w_rg, b_rg, w_re, b_re, w_eg, w_eu, w_ed, g_ple, w_ple, w_ple_gate):
    mix, k_new, v_new, new_buf, C1, n1, m1 = token_mixer(
        rmsnorm(x, g_mix), conv_buf, C0, n0, m0, k_past, v_past,
        w_in, b_if, conv_w, conv_b, g_head_a, w_proj_a, w_proj_b, w_out)
    x = x + mix
    x = x + hier_moe(rmsnorm(x, g_ffn), w_rg, b_rg, w_re, b_re, w_eg, w_eu, w_ed)
    x = x + (p.astype(x.dtype) @ w_ple) * jax.nn.sigmoid(rmsnorm(x, g_ple) @ w_ple_gate)
    return x, k_new, v_new, new_buf, C1, n1, m1


def setup_inputs(seed: int = 0) -> dict:
    key = jax.random.key(seed)
    ks = jax.random.split(key, 40)
    f32 = jnp.float32

    def nrm(k, shape, scale):
        return jax.random.normal(k, shape, f32) * scale

    def gain(k, shape):
        return 1.0 + nrm(k, shape, 0.02)

    G, E, F = N_GROUPS, E_PER_GROUP, D_EXPERT
    b_if = jnp.concatenate([nrm(ks[0], (DEPTH, NH_A), 0.1),
                            jnp.linspace(3.0, 6.0, NH_A, dtype=f32)[None, :] + nrm(ks[1], (DEPTH, NH_A), 0.1)], axis=-1)
    return {
        'x_prompt': nrm(ks[2], (BATCH, SEQ, D_MODEL), 1.0),
        'x_sample': nrm(ks[3], (DEC_BATCH, DEC_SEQ, D_MODEL), 1.0),
        'cache_k': nrm(ks[4], (DEPTH, DEC_BATCH, PAST_LEN, NH_B, DH_B), 1.0),
        'cache_v': nrm(ks[5], (DEPTH, DEC_BATCH, PAST_LEN, NH_B, DH_B), 1.0),
        'state_conv': nrm(ks[6], (DEPTH, DEC_BATCH, CONV_W - 1, 2 * QK_A), 1.0),
        'state_C': nrm(ks[7], (DEPTH, DEC_BATCH, NH_A, DQK_A, DV_A), 0.1),
        'state_n': nrm(ks[8], (DEPTH, DEC_BATCH, NH_A, DQK_A), 0.3),
        'state_m': nrm(ks[9], (DEPTH, DEC_BATCH, NH_A), 0.5),
        'p_prompt': nrm(ks[10], (DEPTH, BATCH, SEQ, D_PLE), 1.0),
        'p_sample': nrm(ks[11], (DEPTH, DEC_BATCH, DEC_SEQ, D_PLE), 1.0),
        'g_mix': gain(ks[12], (DEPTH, D_MODEL)),
        'w_in': nrm(ks[13], (DEPTH, D_MODEL, D_IN), D_MODEL ** -0.5),
        'b_if': b_if,
        'conv_w': nrm(ks[14], (DEPTH, CONV_W, 2 * QK_A), CONV_W ** -0.5),
        'conv_b': nrm(ks[15], (DEPTH, 2 * QK_A), 0.02),
        'g_head_a': gain(ks[16], (DEPTH, D_A)),
        'w_proj_a': nrm(ks[17], (DEPTH, D_A, D_MODEL), D_A ** -0.5),
        'w_proj_b': nrm(ks[18], (DEPTH, D_B, D_MODEL), D_B ** -0.5),
        'w_out': nrm(ks[19], (DEPTH, D_MODEL, D_MODEL), D_MODEL ** -0.5),
        'g_ffn': gain(ks[20], (DEPTH, D_MODEL)),
        'w_route_g': nrm(ks[21], (DEPTH, D_MODEL, G), D_MODEL ** -0.5),
        'b_route_g': nrm(ks[22], (DEPTH, G), 0.01),
        'w_route_e': nrm(ks[23], (DEPTH, G, D_MODEL, E), D_MODEL ** -0.5),
        'b_route_e': nrm(ks[24], (DEPTH, G, E), 0.01),
        'w_exp_gate': nrm(ks[25], (DEPTH, G, E, D_MODEL, F), D_MODEL ** -0.5),
        'w_exp_up': nrm(ks[26], (DEPTH, G, E, D_MODEL, F), D_MODEL ** -0.5),
        'w_exp_down': nrm(ks[27], (DEPTH, G, E, F, D_MODEL), F ** -0.5),
        'g_ple': gain(ks[28], (DEPTH, D_MODEL)),
        'w_ple': nrm(ks[29], (DEPTH, D_PLE, D_MODEL), D_PLE ** -0.5),
        'w_ple_gate': nrm(ks[30], (DEPTH, D_MODEL, D_MODEL), D_MODEL ** -0.5),
        'g_final': gain(ks[31], (D_MODEL,)),
    }


def reference(x_prompt, x_sample, cache_k, cache_v, state_conv, state_C, state_n, state_m,
              p_prompt, p_sample, g_mix, w_in, b_if, conv_w, conv_b, g_head_a, w_proj_a,
              w_proj_b, w_out, g_ffn, w_route_g, b_route_g, w_route_e, b_route_e,
              w_exp_gate, w_exp_up, w_exp_down, g_ple, w_ple, w_ple_gate, g_final):
    f32 = jnp.float32
    xp, xs = x_prompt, x_sample
    outs_p = [[] for _ in range(6)]
    outs_s = [[] for _ in range(6)]
    for i in range(DEPTH):
        lw = (g_mix[i], w_in[i], b_if[i], conv_w[i], conv_b[i], g_head_a[i], w_proj_a[i],
              w_proj_b[i], w_out[i], g_ffn[i], w_route_g[i], b_route_g[i], w_route_e[i],
              b_route_e[i], w_exp_gate[i], w_exp_up[i], w_exp_down[i], g_ple[i], w_ple[i],
              w_ple_gate[i])
        buf0 = jnp.zeros((BATCH, CONV_W - 1, 2 * QK_A), xp.dtype)
        C0 = jnp.zeros((BATCH, NH_A, DQK_A, DV_A), f32)
        n0 = jnp.zeros((BATCH, NH_A, DQK_A), f32)
        m0 = jnp.zeros((BATCH, NH_A), f32)
        kp0 = jnp.zeros((BATCH, 0, NH_B, DH_B), xp.dtype)
        xp, *st_p = layer(xp, p_prompt[i], buf0, C0, n0, m0, kp0, kp0, *lw)
        xs, *st_s = layer(xs, p_sample[i], state_conv[i], state_C[i], state_n[i], state_m[i],
                          cache_k[i], cache_v[i], *lw)
        for lst, a in zip(outs_p, st_p):
            lst.append(a)
        for lst, a in zip(outs_s, st_s):
            lst.append(a)
    y_prompt = rmsnorm(xp, g_final)
    y_sample = rmsnorm(xs, g_final)
    k_p, v_p, conv_p, C_p, n_p, m_p = [jnp.stack(a, axis=0) for a in outs_p]
    k_s, v_s, conv_s, C_s, n_s, m_s = [jnp.stack(a, axis=0) for a in outs_s]
    return (y_prompt, y_sample, k_p, v_p, conv_p, C_p, n_p, m_p, k_s, v_s, conv_s, C_s, n_s, m_s)
```

```python
import functools
import math

import jax
import jax.numpy as jnp
from jax import lax
from jax.experimental import pallas as pl
from jax.experimental.pallas import tpu as pltpu

F32 = jnp.float32
BF16 = jnp.bfloat16
EPS = 1e-6

CHUNK = 64
NH_A, DQK_A, DV_A = 4, 128, 256
QK_A, D_A = NH_A * DQK_A, NH_A * DV_A
CONV_W = 4
NH_B, DH_B = 8, 128
D_B = NH_B * DH_B
N_GROUPS, E_PER_GROUP = 4, 8
N_EXPERTS = N_GROUPS * E_PER_GROUP
LANES = 128
EXPERT_TILE = 256
VMEM_LIMIT = 56 * 1024 * 1024


def _pick(n, cands):
    for c in cands:
        if n % c == 0:
            return c
    raise ValueError(f"no tile for {n} in {cands}")


def _params(sem):
    return pltpu.CompilerParams(dimension_semantics=sem, vmem_limit_bytes=VMEM_LIMIT)


def _rms(x, g):
    return x * lax.rsqrt(jnp.mean(x * x, axis=-1, keepdims=True) + EPS) * g


def _log_sigmoid(x):
    return jnp.minimum(x, 0.0) - jnp.log(1.0 + jnp.exp(-jnp.abs(x)))


def _rms_kernel(x_ref, g_ref, o_ref):
    o_ref[...] = _rms(x_ref[...], g_ref[...]).astype(o_ref.dtype)


def rms_cast(x, g):
    T, D = x.shape
    tm = _pick(T, (512, 256, 128, 64, 32, 16, 8))
    return pl.pallas_call(
        _rms_kernel,
        out_shape=jax.ShapeDtypeStruct((T, D), BF16),
        grid=(T // tm,),
        in_specs=[pl.BlockSpec((tm, D), lambda i: (i, 0)),
                  pl.BlockSpec((1, D), lambda i: (0, 0))],
        out_specs=pl.BlockSpec((tm, D), lambda i: (i, 0)),
        compiler_params=_params(("parallel",)),
        name="rms_cast",
    )(x, g.reshape(1, D))


def _mm_kernel(a_ref, w_ref, o_ref):
    o_ref[...] = jnp.dot(a_ref[...], w_ref[...], preferred_element_type=F32)


def matmul(a, w, name):
    T, K = a.shape
    N = w.shape[1]
    tm = _pick(T, (1280, 640, 512, 256, 128, 64, 32, 16))
    tn = _pick(N, (512, 256, 128))
    return pl.pallas_call(
        _mm_kernel,
        out_shape=jax.ShapeDtypeStruct((T, N), F32),
        grid=(T // tm, N // tn),
        in_specs=[pl.BlockSpec((tm, K), lambda i, j: (i, 0)),
                  pl.BlockSpec((K, tn), lambda i, j: (0, j))],
        out_specs=pl.BlockSpec((tm, tn), lambda i, j: (i, j)),
        compiler_params=_params(("parallel", "arbitrary")),
        name=name,
    )(a, w)


def _mlstm_kernel(qk_ref, v_ref, o_ref, g_ref, bif_ref, cw_ref, cb_ref, gh_ref,
                  cbuf_ref, c0_ref, n0_ref, m0_ref,
                  h_ref, cout_ref, nout_ref, mout_ref, convout_ref,
                  xbuf, c_s, n_s, m_s, *, c, nc):
    j = pl.program_id(1)

    @pl.when(j == 0)
    def _():
        c_s[...] = c0_ref[...]
        n_s[...] = n0_ref[...]
        m_s[...] = m0_ref[...]
        xbuf[5:8, :] = cbuf_ref[...]

    xbuf[8:8 + c, :] = qk_ref[...]
    y = cb_ref[...] + cw_ref[0:1, :] * xbuf[5:5 + c, :]
    for t in range(1, CONV_W):
        y = y + cw_ref[t:t + 1, :] * xbuf[5 + t:5 + t + c, :]
    last_rows = xbuf[c + 5:c + 8, :]
    xbuf[5:8, :] = last_rows
    qk = y * jax.nn.sigmoid(y)

    lane = lax.broadcasted_iota(jnp.int32, (c, LANES), 1)
    graw = g_ref[...] + bif_ref[...]
    xg = jnp.where(lane < NH_A, graw, jnp.where(lane < 2 * NH_A, _log_sigmoid(graw), 0.0))
    row = lax.broadcasted_iota(jnp.int32, (c, c), 0)
    col = lax.broadcasted_iota(jnp.int32, (c, c), 1)
    tril = (col <= row).astype(F32)
    eye = (col == row).astype(F32)
    hp = lax.Precision.HIGHEST
    bc = jnp.dot(tril, xg, precision=hp, preferred_element_type=F32)
    dn0 = (((0,), (0,)), ((), ()))
    xt = lax.dot_general(xg, eye, dn0, precision=hp, preferred_element_type=F32)
    triu = (row <= col).astype(F32)
    bt = lax.dot_general(xg, triu, dn0, precision=hp, preferred_element_type=F32)
    causal = col <= row

    for h in range(NH_A):
        q = qk[:, h * DQK_A:(h + 1) * DQK_A]
        k = qk[:, QK_A + h * DQK_A:QK_A + (h + 1) * DQK_A] * (DQK_A ** -0.5)
        v = v_ref[:, h * DV_A:(h + 1) * DV_A]
        qb, kb, vb = q.astype(BF16), k.astype(BF16), v.astype(BF16)
        i_col = xg[:, h:h + 1]
        b_col = bc[:, NH_A + h:NH_A + h + 1]
        i_row = xt[h:h + 1, :]
        b_row = bt[NH_A + h:NH_A + h + 1, :]
        m_prev = m_s[0:1, h:h + 1]
        C = c_s[h]
        nvec = n_s[h:h + 1, :]

        dmat = jnp.where(causal, b_col - b_row + i_row, -jnp.inf)
        inter = b_col + m_prev
        m_t = jnp.maximum(inter, jnp.max(dmat, axis=1, keepdims=True))
        e = jnp.exp(dmat - m_t)
        s = lax.dot_general(qb, kb, (((1,), (1,)), ((), ())), preferred_element_type=F32) * e
        w_inter = jnp.exp(inter - m_t)
        num = (jnp.dot(s.astype(BF16), vb, preferred_element_type=F32)
               + w_inter * jnp.dot(qb, C.astype(BF16), preferred_element_type=F32))
        den = jnp.sum(s, axis=1, keepdims=True) + w_inter * jnp.sum(q * nvec, axis=1, keepdims=True)
        hh = num / jnp.maximum(jnp.abs(den), jnp.exp(-m_t))
        hh = hh * lax.rsqrt(jnp.mean(hh * hh, axis=1, keepdims=True) + EPS)
        hh = hh * gh_ref[:, h * DV_A:(h + 1) * DV_A]
        hh = jax.nn.sigmoid(o_ref[:, h * DV_A:(h + 1) * DV_A]) * hh
        h_ref[:, h * DV_A:(h + 1) * DV_A] = hh.astype(h_ref.dtype)

        b_end = b_col[c - 1:c, :]
        dec = b_end - b_col + i_col
        m_new = jnp.maximum(b_end + m_prev, jnp.max(dec, axis=0, keepdims=True))
        wk = jnp.exp(dec - m_new)
        w_old = jnp.exp(b_end + m_prev - m_new)
        kw = k * wk
        c_s[h] = w_old * C + lax.dot_general(kw.astype(BF16), vb, dn0, preferred_element_type=F32)
        n_s[h:h + 1, :] = w_old * nvec + jnp.sum(kw, axis=0, keepdims=True)
        m_s[0:1, h:h + 1] = m_new

    @pl.when(j == nc - 1)
    def _():
        cout_ref[...] = c_s[...]
        nout_ref[...] = n_s[...]
        mout_ref[...] = m_s[...]
        convout_ref[...] = last_rows


def mlstm(P, G, row0, N, L, b_if, conv_w, conv_b, g_head, conv_buf, C0, n0, m0):
    c = min(L, CHUNK)
    nc = L // c
    base = row0 // c
    m0p = jnp.zeros((N, 1, LANES), F32).at[:, 0, :NH_A].set(m0)
    bif = jnp.zeros((1, LANES), F32).at[0, :2 * NH_A].set(b_if)
    rowmap = lambda col: (lambda n, j: (base + n * nc + j, col))
    st4 = lambda n, j: (n, 0, 0, 0)
    st3 = lambda n, j: (n, 0, 0)
    outs = pl.pallas_call(
        functools.partial(_mlstm_kernel, c=c, nc=nc),
        out_shape=(jax.ShapeDtypeStruct((N * L, D_A), BF16),
                   jax.ShapeDtypeStruct((N, NH_A, DQK_A, DV_A), F32),
                   jax.ShapeDtypeStruct((N, NH_A, DQK_A), F32),
                   jax.ShapeDtypeStruct((N, 1, LANES), F32),
                   jax.ShapeDtypeStruct((N, CONV_W - 1, 2 * QK_A), F32)),
        grid=(N, nc),
        in_specs=[pl.BlockSpec((c, 2 * QK_A), rowmap(0)),
                  pl.BlockSpec((c, D_A), rowmap(1)),
                  pl.BlockSpec((c, D_A), rowmap(2)),
                  pl.BlockSpec((c, LANES), rowmap(0)),
                  pl.BlockSpec((1, LANES), lambda n, j: (0, 0)),
                  pl.BlockSpec((CONV_W, 2 * QK_A), lambda n, j: (0, 0)),
                  pl.BlockSpec((1, 2 * QK_A), lambda n, j: (0, 0)),
                  pl.BlockSpec((1, D_A), lambda n, j: (0, 0)),
                  pl.BlockSpec((None, CONV_W - 1, 2 * QK_A), st3),
                  pl.BlockSpec((None, NH_A, DQK_A, DV_A), st4),
                  pl.BlockSpec((None, NH_A, DQK_A), st3),
                  pl.BlockSpec((None, 1, LANES), st3)],
        out_specs=(pl.BlockSpec((c, D_A), lambda n, j: (n * nc + j, 0)),
                   pl.BlockSpec((None, NH_A, DQK_A, DV_A), st4),
                   pl.BlockSpec((None, NH_A, DQK_A), st3),
                   pl.BlockSpec((None, 1, LANES), st3),
                   pl.BlockSpec((None, CONV_W - 1, 2 * QK_A), st3)),
        scratch_shapes=[pltpu.VMEM((c + 8, 2 * QK_A), F32),
                        pltpu.VMEM((NH_A, DQK_A, DV_A), F32),
                        pltpu.VMEM((NH_A, DQK_A), F32),
                        pltpu.VMEM((1, LANES), F32)],
        compiler_params=_params(("parallel", "arbitrary")),
        name="mlstm",
    )(P, P, P, G, bif, conv_w, conv_b.reshape(1, -1), g_head.reshape(1, -1), conv_buf, C0, n0, m0p)
    h, C1, n1, m1, cv = outs
    return h, C1, n1, m1[:, 0, :NH_A], cv


def _sb_block(q_bf, k_blk, v_blk, R, mask):
    tk = k_blk.shape[0]
    z = lax.dot_general(q_bf, k_blk.astype(BF16), (((1,), (1,)), ((), ())),
                        preferred_element_type=F32) * (DH_B ** -0.5)
    l1mb = -(jnp.maximum(z, 0.0) + jnp.log(1.0 + jnp.exp(-jnp.abs(z))))
    if mask is not None:
        l1mb = jnp.where(mask, l1mb, 0.0)
    jr = lax.broadcasted_iota(jnp.int32, (tk, tk), 0)
    sc = lax.broadcasted_iota(jnp.int32, (tk, tk), 1)
    later = (jr > sc).astype(BF16)
    hi = l1mb.astype(BF16)
    r1 = l1mb - hi.astype(F32)
    mid = r1.astype(BF16)
    lo = (r1 - mid.astype(F32)).astype(BF16)
    rest = (jnp.dot(hi, later, preferred_element_type=F32)
            + jnp.dot(mid, later, preferred_element_type=F32)
            + jnp.dot(lo, later, preferred_element_type=F32)) + R
    a = jnp.exp(z + l1mb + rest)
    if mask is not None:
        a = jnp.where(mask, a, 0.0)
    pv = jnp.dot(a.astype(BF16), v_blk.astype(BF16), preferred_element_type=F32)
    return pv, R + jnp.sum(l1mb, axis=1, keepdims=True)


def _sb_kernel(*refs, tq, tk, n_past_blk, self_blocks):
    if n_past_blk:
        q_ref, k_ref, v_ref, pk_ref, pv_ref, o_ref = refs
    else:
        q_ref, k_ref, v_ref, o_ref = refs
    qi = pl.program_id(2)
    q_bf = q_ref[...].astype(BF16)
    rr = lax.broadcasted_iota(jnp.int32, (tq, tq), 0)
    cc = lax.broadcasted_iota(jnp.int32, (tq, tq), 1)
    if self_blocks:
        start = pl.multiple_of(qi * tq, tq)
        kd = k_ref[pl.ds(start, tq), :]
        vd = v_ref[pl.ds(start, tq), :]
    else:
        kd = k_ref[...]
        vd = v_ref[...]
    acc, R = _sb_block(q_bf, kd, vd, jnp.zeros((tq, 1), F32), cc < rr)

    if self_blocks:
        def body(t, carry):
            acc, R = carry
            kb = qi - 1 - t
            st = pl.multiple_of(kb * tk, tk)
            pv, R = _sb_block(q_bf, k_ref[pl.ds(st, tk), :], v_ref[pl.ds(st, tk), :], R, None)
            return acc + pv, R
        acc, R = lax.fori_loop(0, qi, body, (acc, R))

    if n_past_blk:
        def pbody(t, carry):
            acc, R = carry
            kb = n_past_blk - 1 - t
            st = pl.multiple_of(kb * tk, tk)
            pv, R = _sb_block(q_bf, pk_ref[pl.ds(st, tk), :], pv_ref[pl.ds(st, tk), :], R, None)
            return acc + pv, R
        acc, R = lax.fori_loop(0, n_past_blk, pbody, (acc, R))
    o_ref[...] = acc.astype(o_ref.dtype)


def sb_attention(P, row0, N, L, past_k=None, past_v=None):
    tq = min(L, 256)
    nq = L // tq
    QB, KB, VB = 3 * D_B // DH_B, 4 * D_B // DH_B, 5 * D_B // DH_B
    in_specs = [pl.BlockSpec((tq, DH_B), lambda n, h, i: (row0 // tq + n * nq + i, QB + h)),
                pl.BlockSpec((L, DH_B), lambda n, h, i: (row0 // L + n, KB + h)),
                pl.BlockSpec((L, DH_B), lambda n, h, i: (row0 // L + n, VB + h))]
    args = [P, P, P]
    n_past_blk = 0
    tk = tq
    if past_k is not None:
        Lp = past_k.shape[1]
        tk = min(Lp, 256)
        n_past_blk = Lp // tk
        in_specs += [pl.BlockSpec((None, Lp, DH_B), lambda n, h, i: (n, 0, h))] * 2
        args += [past_k.reshape(N, Lp, D_B), past_v.reshape(N, Lp, D_B)]
    return pl.pallas_call(
        functools.partial(_sb_kernel, tq=tq, tk=tk, n_past_blk=n_past_blk, self_blocks=nq > 1),
        out_shape=jax.ShapeDtypeStruct((N * L, D_B), BF16),
        grid=(N, NH_B, nq),
        in_specs=in_specs,
        out_specs=pl.BlockSpec((tq, DH_B), lambda n, h, i: (n * nq + i, h)),
        compiler_params=_params(("parallel", "parallel", "arbitrary")),
        name="sb_attention",
    )(*args)


def _merge_kernel(h_ref, ha_ref, hb_ref, wga_ref, wgb_ref, wpa_ref, wpb_ref, o_ref):
    h = h_ref[...]
    ga = jnp.dot(h, wga_ref[...], preferred_element_type=F32)
    gb = jnp.dot(h, wgb_ref[...], preferred_element_type=F32)
    pa = jnp.dot(ha_ref[...], wpa_ref[...], preferred_element_type=F32)
    pb = jnp.dot(hb_ref[...], wpb_ref[...], preferred_element_type=F32)
    o_ref[...] = (jax.nn.sigmoid(ga) * pa + jax.nn.sigmoid(gb) * pb).astype(o_ref.dtype)


def merge(h, ha, hb, wga, wgb, wpa, wpb):
    T, D = h.shape
    tm = _pick(T, (640, 512, 256, 128, 64, 32, 16))
    tn = _pick(D, (512, 256, 128))
    row = lambda i, j: (i, 0)
    colw = lambda i, j: (0, j)
    return pl.pallas_call(
        _merge_kernel,
        out_shape=jax.ShapeDtypeStruct((T, D), BF16),
        grid=(T // tm, D // tn),
        in_specs=[pl.BlockSpec((tm, D), row), pl.BlockSpec((tm, D_A), row), pl.BlockSpec((tm, D_B), row),
                  pl.BlockSpec((D, tn), colw), pl.BlockSpec((D, tn), colw),
                  pl.BlockSpec((D_A, tn), colw), pl.BlockSpec((D_B, tn), colw)],
        out_specs=pl.BlockSpec((tm, tn), lambda i, j: (i, j)),
        compiler_params=_params(("parallel", "arbitrary")),
        name="merge",
    )(h, ha, hb, wga, wgb, wpa, wpb)


def _wout_route_kernel(mg_ref, x_ref, wo_ref, g_ref, wr_ref, br_ref, x1_ref, t_ref, r_ref):
    x1 = x_ref[...] + jnp.dot(mg_ref[...], wo_ref[...], preferred_element_type=F32)
    x1_ref[...] = x1
    t = _rms(x1, g_ref[...])
    t_ref[...] = t
    lg = jnp.dot(t.astype(BF16), wr_ref[...], preferred_element_type=F32) + br_ref[...]
    tm = lg.shape[0]
    lane = lax.broadcasted_iota(jnp.int32, (tm, LANES), 1)
    ninf = -jnp.inf
    glm = jnp.where(lane < N_GROUPS, lg, ninf)
    gmax = jnp.max(glm, axis=1, keepdims=True)
    gsel = jnp.min(jnp.where(glm == gmax, lane, LANES), axis=1, keepdims=True)
    p_sel = 1.0 / jnp.sum(jnp.exp(glm - gmax), axis=1, keepdims=True)
    lo = N_GROUPS + E_PER_GROUP * gsel
    elm = jnp.where((lane >= lo) & (lane < lo + E_PER_GROUP), lg, ninf)
    m1 = jnp.max(elm, axis=1, keepdims=True)
    i1 = jnp.min(jnp.where(elm == m1, lane, LANES), axis=1, keepdims=True)
    elm2 = jnp.where(lane == i1, ninf, elm)
    m2 = jnp.max(elm2, axis=1, keepdims=True)
    i2 = jnp.min(jnp.where(elm2 == m2, lane, LANES), axis=1, keepdims=True)
    e2 = jnp.exp(m2 - m1)
    w1 = p_sel / (1.0 + e2)
    w2 = p_sel * e2 / (1.0 + e2)
    r = jnp.where(lane == 0, (i1 - N_GROUPS).astype(F32),
        jnp.where(lane == 1, (i2 - N_GROUPS).astype(F32),
        jnp.where(lane == 2, w1, jnp.where(lane == 3, w2, 0.0))))
    r_ref[...] = r


def wout_route(mg, x, wo, g_ffn, wr, br):
    T, D = x.shape
    tm = _pick(T, (256, 128, 64, 32, 16, 8))
    row = lambda i: (i, 0)
    fix = lambda i: (0, 0)
    return pl.pallas_call(
        _wout_route_kernel,
        out_shape=(jax.ShapeDtypeStruct((T, D), F32), jax.ShapeDtypeStruct((T, D), F32),
                   jax.ShapeDtypeStruct((T, LANES), F32)),
        grid=(T // tm,),
        in_specs=[pl.BlockSpec((tm, D), row), pl.BlockSpec((tm, D), row), pl.BlockSpec((D, D), fix),
                  pl.BlockSpec((1, D), fix), pl.BlockSpec((D, LANES), fix), pl.BlockSpec((1, LANES), fix)],
        out_specs=(pl.BlockSpec((tm, D), row), pl.BlockSpec((tm, D), row), pl.BlockSpec((tm, LANES), row)),
        compiler_params=_params(("parallel",)),
        name="wout_route",
    )(mg, x, wo, g_ffn.reshape(1, D), wr, br)


def _dispatch_kernel(pos_hbm, t_ref, xs_in, xs_out, idx, sem, isem, *, tm):
    del xs_in
    i = pl.program_id(0)
    cp = pltpu.make_async_copy(pos_hbm.at[pl.ds(i * 2 * tm, 2 * tm)], idx, isem)
    cp.start()
    cp.wait()

    def row_copy(r, k):
        return pltpu.make_async_copy(t_ref.at[pl.ds(r, 1), :], xs_out.at[pl.ds(idx[2 * r + k], 1), :], sem)

    def start(r, _):
        row_copy(r, 0).start()
        row_copy(r, 1).start()
        return 0
    lax.fori_loop(0, tm, start, 0)

    def wait(r, _):
        row_copy(r, 0).wait()
        row_copy(r, 1).wait()
        return 0
    lax.fori_loop(0, tm, wait, 0)


def dispatch(t, pos, a_pad):
    T, D = t.shape
    tm = _pick(T, (256, 128, 64, 32, 16, 8))
    xs0 = jnp.zeros((a_pad, D), t.dtype)
    return pl.pallas_call(
        functools.partial(_dispatch_kernel, tm=tm),
        out_shape=jax.ShapeDtypeStruct((a_pad, D), t.dtype),
        grid=(T // tm,),
        in_specs=[pl.BlockSpec(memory_space=pl.ANY),
                  pl.BlockSpec((tm, D), lambda i: (i, 0)),
                  pl.BlockSpec(memory_space=pl.ANY)],
        out_specs=pl.BlockSpec(memory_space=pl.ANY),
        scratch_shapes=[pltpu.SMEM((2 * tm,), jnp.int32), pltpu.SemaphoreType.DMA(()),
                        pltpu.SemaphoreType.DMA(())],
        input_output_aliases={2: 0},
        compiler_params=_params(("arbitrary",)),
        name="moe_dispatch",
    )(pos, t, xs0)


def _expert_kernel(te_ref, nu_ref, x_ref, wg_ref, wu_ref, wd_ref, y_ref, wg_s, wu_s, wd_s):
    i = pl.program_id(0)
    prev = te_ref[jnp.maximum(i - 1, 0)]

    @pl.when((i == 0) | (te_ref[i] != prev))
    def _():
        wg_s[...] = wg_ref[...].astype(BF16)
        wu_s[...] = wu_ref[...].astype(BF16)
        wd_s[...] = wd_ref[...].astype(BF16)

    @pl.when(i < nu_ref[0])
    def _():
        x = x_ref[...].astype(BF16)
        a = jnp.dot(x, wg_s[...], preferred_element_type=F32)
        u = jnp.dot(x, wu_s[...], preferred_element_type=F32)
        hid = (a * jax.nn.sigmoid(a) * u).astype(BF16)
        y_ref[...] = jnp.dot(hid, wd_s[...], preferred_element_type=F32)

    @pl.when(i >= nu_ref[0])
    def _():
        y_ref[...] = jnp.zeros_like(y_ref)


def experts(xs, tile_expert, n_used, w_gate, w_up, w_down):
    A, D = xs.shape
    F = w_gate.shape[-1]
    nt = A // EXPERT_TILE
    grid_spec = pltpu.PrefetchScalarGridSpec(
        num_scalar_prefetch=2,
        grid=(nt,),
        in_specs=[pl.BlockSpec((EXPERT_TILE, D), lambda i, te, nu: (i, 0)),
                  pl.BlockSpec((None, D, F), lambda i, te, nu: (te[i], 0, 0)),
                  pl.BlockSpec((None, D, F), lambda i, te, nu: (te[i], 0, 0)),
                  pl.BlockSpec((None, F, D), lambda i, te, nu: (te[i], 0, 0))],
        out_specs=pl.BlockSpec((EXPERT_TILE, D), lambda i, te, nu: (i, 0)),
        scratch_shapes=[pltpu.VMEM((D, F), BF16), pltpu.VMEM((D, F), BF16), pltpu.VMEM((F, D), BF16)],
    )
    return pl.pallas_call(
        _expert_kernel,
        out_shape=jax.ShapeDtypeStruct((A, D), F32),
        grid_spec=grid_spec,
        compiler_params=_params(("arbitrary",)),
        name="moe_experts",
    )(tile_expert, n_used, xs, w_gate.reshape(N_EXPERTS, D, F), w_up.reshape(N_EXPERTS, D, F),
      w_down.reshape(N_EXPERTS, F, D))


def _combine_kernel(pos_hbm, ys_hbm, x1_ref, r_ref, p_ref, wple_ref, wpg_ref, gple_ref, gfin_ref,
                    y_ref, idx, ybuf, sem, isem, *, tm):
    i = pl.program_id(0)
    cp = pltpu.make_async_copy(pos_hbm.at[pl.ds(i * 2 * tm, 2 * tm)], idx, isem)
    cp.start()
    cp.wait()

    def row_copy(r, k):
        return pltpu.make_async_copy(ys_hbm.at[pl.ds(idx[2 * r + k], 1), :], ybuf.at[k, pl.ds(r, 1), :], sem)

    def start(r, _):
        row_copy(r, 0).start()
        row_copy(r, 1).start()
        return 0
    lax.fori_loop(0, tm, start, 0)

    def wait(r, _):
        row_copy(r, 0).wait()
        row_copy(r, 1).wait()
        return 0
    lax.fori_loop(0, tm, wait, 0)

    r = r_ref[...]
    x2 = x1_ref[...] + r[:, 2:3] * ybuf[0] + r[:, 3:4] * ybuf[1]
    hp = _rms(x2, gple_ref[...]).astype(BF16)
    gate = jax.nn.sigmoid(jnp.dot(hp, wpg_ref[...], preferred_element_type=F32))
    emb = jnp.dot(p_ref[...].astype(BF16), wple_ref[...], preferred_element_type=F32)
    x3 = x2 + emb * gate
    y_ref[...] = _rms(x3, gfin_ref[...])


def combine(pos, ys, x1, route, p, w_ple, w_ple_gate, g_ple, g_final):
    T, D = x1.shape
    tm = _pick(T, (256, 128, 64, 32, 16, 8))
    row = lambda i: (i, 0)
    fix = lambda i: (0, 0)
    return pl.pallas_call(
        functools.partial(_combine_kernel, tm=tm),
        out_shape=jax.ShapeDtypeStruct((T, D), F32),
        grid=(T // tm,),
        in_specs=[pl.BlockSpec(memory_space=pl.ANY), pl.BlockSpec(memory_space=pl.ANY),
                  pl.BlockSpec((tm, D), row), pl.BlockSpec((tm, LANES), row),
                  pl.BlockSpec((tm, p.shape[1]), row),
                  pl.BlockSpec(w_ple.shape, fix), pl.BlockSpec((D, D), fix),
                  pl.BlockSpec((1, D), fix), pl.BlockSpec((1, D), fix)],
        out_specs=pl.BlockSpec((tm, D), row),
        scratch_shapes=[pltpu.SMEM((2 * tm,), jnp.int32), pltpu.VMEM((2, tm, D), F32),
                        pltpu.SemaphoreType.DMA(()), pltpu.SemaphoreType.DMA(())],
        compiler_params=_params(("arbitrary",)),
        name="moe_combine_ple",
    )(pos, ys, x1, route, p, w_ple, w_ple_gate, g_ple.reshape(1, D), g_final.reshape(1, D))


def _routing_tables(ids, n_tiles):
    flat = ids.reshape(-1)
    onehot = (flat[:, None] == jnp.arange(N_EXPERTS, dtype=jnp.int32)[None, :]).astype(jnp.int32)
    csum = jnp.cumsum(onehot, axis=0)
    counts = csum[-1]
    rank = jnp.sum((csum - onehot) * onehot, axis=1)
    tiles = (counts + EXPERT_TILE - 1) // EXPERT_TILE
    tile_end = jnp.cumsum(tiles)
    tile_start = tile_end - tiles
    pos = (tile_start[flat] * EXPERT_TILE + rank).astype(jnp.int32)
    n_used = tile_end[-1]
    tidx = jnp.minimum(jnp.arange(n_tiles, dtype=jnp.int32), n_used - 1)
    tile_expert = jnp.sum((tidx[:, None] >= tile_end[None, :]).astype(jnp.int32), axis=1)
    return pos, tile_expert.astype(jnp.int32), n_used.reshape(1).astype(jnp.int32)


def kernel(x_prompt, x_sample, cache_k, cache_v, state_conv, state_C, state_n, state_m, p_prompt, p_sample,
           g_mix, w_in, b_if, conv_w, conv_b, g_head_a, w_proj_a, w_proj_b, w_out, g_ffn, w_route_g,
           b_route_g, w_route_e, b_route_e, w_exp_gate, w_exp_up, w_exp_down, g_ple, w_ple, w_ple_gate,
           g_final):
    assert w_in.shape[0] == 1, "single layer"
    B, S, D = x_prompt.shape
    DB, DS, _ = x_sample.shape
    Tp, Ts = B * S, DB * DS
    T = Tp + Ts
    x = jnp.concatenate([x_prompt.reshape(Tp, D), x_sample.reshape(Ts, D)], axis=0)
    p = jnp.concatenate([p_prompt[0].reshape(Tp, -1), p_sample[0].reshape(Ts, -1)], axis=0)

    wi = w_in[0]
    g0 = 2 * QK_A + D_A
    m0 = g0 + 2 * NH_A
    e0 = m0 + D_A + 3 * D_B
    w_main = jnp.concatenate([wi[:, :g0], wi[:, m0:e0]], axis=1).astype(BF16)
    w_gates = jnp.zeros((D, LANES), F32).at[:, :2 * NH_A].set(wi[:, g0:m0]).astype(BF16)
    wga = wi[:, e0:e0 + D].astype(BF16)
    wgb = wi[:, e0 + D:].astype(BF16)

    h = rms_cast(x, g_mix[0])
    P = matmul(h, w_main, "proj_main")
    G = matmul(h, w_gates, "proj_gates")

    zeros = functools.partial(jnp.zeros, dtype=F32)
    ha_p, C_p, n_p, m_p, cv_p = mlstm(P, G, 0, B, S, b_if[0], conv_w[0], conv_b[0], g_head_a[0],
                                      zeros((B, CONV_W - 1, 2 * QK_A)), zeros((B, NH_A, DQK_A, DV_A)),
                                      zeros((B, NH_A, DQK_A)), zeros((B, NH_A)))
    ha_s, C_s, n_s, m_s, cv_s = mlstm(P, G, Tp, DB, DS, b_if[0], conv_w[0], conv_b[0], g_head_a[0],
                                      state_conv[0], state_C[0], state_n[0], state_m[0])
    hb_p = sb_attention(P, 0, B, S)
    hb_s = sb_attention(P, Tp, DB, DS, cache_k[0], cache_v[0])
    ha = jnp.concatenate([ha_p, ha_s], axis=0)
    hb = jnp.concatenate([hb_p, hb_s], axis=0)

    mg = merge(h, ha, hb, wga, wgb, w_proj_a[0].astype(BF16), w_proj_b[0].astype(BF16))

    wr = jnp.zeros((D, LANES), F32)
    wr = wr.at[:, :N_GROUPS].set(w_route_g[0])
    wr = wr.at[:, N_GROUPS:N_GROUPS + N_EXPERTS].set(
        jnp.transpose(w_route_e[0], (1, 0, 2)).reshape(D, N_EXPERTS))
    br = jnp.zeros((1, LANES), F32)
    br = br.at[0, :N_GROUPS].set(b_route_g[0])
    br = br.at[0, N_GROUPS:N_GROUPS + N_EXPERTS].set(b_route_e[0].reshape(-1))
    x1, t, route = wout_route(mg, x, w_out[0].astype(BF16), g_ffn[0], wr.astype(BF16), br)

    ids = route[:, :2].astype(jnp.int32)
    n_tiles = (2 * T + N_EXPERTS * (EXPERT_TILE - 1)) // EXPERT_TILE + 1
    pos, tile_expert, n_used = _routing_tables(ids, n_tiles)
    xs = dispatch(t, pos, n_tiles * EXPERT_TILE)
    ys = experts(xs, tile_expert, n_used, w_exp_gate[0], w_exp_up[0], w_exp_down[0])
    y = combine(pos, ys, x1, route, p, w_ple[0].astype(BF16), w_ple_gate[0].astype(BF16), g_ple[0], g_final)

    kq = P[:, 4 * D_B:5 * D_B]
    vq = P[:, 5 * D_B:6 * D_B]
    return (y[:Tp].reshape(B, S, D), y[Tp:].reshape(DB, DS, D),
            kq[:Tp].reshape(1, B, S, NH_B, DH_B), vq[:Tp].reshape(1, B, S, NH_B, DH_B),
            cv_p[None], C_p[None], n_p[None], m_p[None],
            kq[Tp:].reshape(1, DB, DS, NH_B, DH_B), vq[Tp:].reshape(1, DB, DS, NH_B, DH_B),
            cv_s[None], C_s[None], n_s[None], m_s[None])
```

```python
import functools
import math

import jax
import jax.numpy as jnp
from jax import lax
from jax.experimental import pallas as pl
from jax.experimental.pallas import tpu as pltpu

F32 = jnp.float32
BF16 = jnp.bfloat16
EPS = 1e-6

CHUNK = 64
NH_A, DQK_A, DV_A = 4, 128, 256
QK_A, D_A = NH_A * DQK_A, NH_A * DV_A
CONV_W = 4
NH_B, DH_B = 8, 128
D_B = NH_B * DH_B
N_GROUPS, E_PER_GROUP = 4, 8
N_EXPERTS = N_GROUPS * E_PER_GROUP
LANES = 128
EXPERT_TILE = 256
VMEM_LIMIT = 56 * 1024 * 1024


def _pick(n, cands):
    for c in cands:
        if n % c == 0:
            return c
    raise ValueError(f"no tile for {n} in {cands}")


def _params(sem):
    return pltpu.CompilerParams(dimension_semantics=sem, vmem_limit_bytes=VMEM_LIMIT)


def _rms(x, g):
    return x * lax.rsqrt(jnp.mean(x * x, axis=-1, keepdims=True) + EPS) * g


def _log_sigmoid(x):
    return jnp.minimum(x, 0.0) - jnp.log(1.0 + jnp.exp(-jnp.abs(x)))


def _rms_kernel(x_ref, g_ref, o_ref):
    o_ref[...] = _rms(x_ref[...], g_ref[...]).astype(o_ref.dtype)


def rms_cast(x, g):
    T, D = x.shape
    tm = _pick(T, (512, 256, 128, 64, 32, 16, 8))
    return pl.pallas_call(
        _rms_kernel,
        out_shape=jax.ShapeDtypeStruct((T, D), BF16),
        grid=(T // tm,),
        in_specs=[pl.BlockSpec((tm, D), lambda i: (i, 0)),
                  pl.BlockSpec((1, D), lambda i: (0, 0))],
        out_specs=pl.BlockSpec((tm, D), lambda i: (i, 0)),
        compiler_params=_params(("parallel",)),
        name="rms_cast",
    )(x, g.reshape(1, D))


def _mm_kernel(a_ref, w_ref, o_ref):
    o_ref[...] = jnp.dot(a_ref[...], w_ref[...], preferred_element_type=F32)


def matmul(a, w, name):
    T, K = a.shape
    N = w.shape[1]
    tm = _pick(T, (1280, 640, 512, 256, 128, 64, 32, 16))
    tn = _pick(N, (512, 256, 128))
    return pl.pallas_call(
        _mm_kernel,
        out_shape=jax.ShapeDtypeStruct((T, N), F32),
        grid=(T // tm, N // tn),
        in_specs=[pl.BlockSpec((tm, K), lambda i, j: (i, 0)),
                  pl.BlockSpec((K, tn), lambda i, j: (0, j))],
        out_specs=pl.BlockSpec((tm, tn), lambda i, j: (i, j)),
        compiler_params=_params(("parallel", "arbitrary")),
        name=name,
    )(a, w)


def _mlstm_kernel(qk_ref, v_ref, o_ref, g_ref, bif_ref, cw_ref, cb_ref, gh_ref,
                  cbuf_ref, c0_ref, n0_ref, m0_ref,
                  h_ref, cout_ref, nout_ref, mout_ref, convout_ref,
                  xbuf, c_s, n_s, m_s, *, c, nc):
    j = pl.program_id(1)

    @pl.when(j == 0)
    def _():
        c_s[...] = c0_ref[...]
        n_s[...] = n0_ref[...]
        m_s[...] = m0_ref[...]
        xbuf[5:8, :] = cbuf_ref[...]

    xbuf[8:8 + c, :] = qk_ref[...]
    y = cb_ref[...] + cw_ref[0:1, :] * xbuf[5:5 + c, :]
    for t in range(1, CONV_W):
        y = y + cw_ref[t:t + 1, :] * xbuf[5 + t:5 + t + c, :]
    last_rows = xbuf[c + 5:c + 8, :]
    xbuf[5:8, :] = last_rows
    qk = y * jax.nn.sigmoid(y)

    lane = lax.broadcasted_iota(jnp.int32, (c, LANES), 1)
    graw = g_ref[...] + bif_ref[...]
    xg = jnp.where(lane < NH_A, graw, jnp.where(lane < 2 * NH_A, _log_sigmoid(graw), 0.0))
    row = lax.broadcasted_iota(jnp.int32, (c, c), 0)
    col = lax.broadcasted_iota(jnp.int32, (c, c), 1)
    tril = (col <= row).astype(F32)
    eye = (col == row).astype(F32)
    hp = lax.Precision.HIGHEST
    bc = jnp.dot(tril, xg, precision=hp, preferred_element_type=F32)
    dn0 = (((0,), (0,)), ((), ()))
    xt = lax.dot_general(xg, eye, dn0, precision=hp, preferred_element_type=F32)
    triu = (row <= col).astype(F32)
    bt = lax.dot_general(xg, triu, dn0, precision=hp, preferred_element_type=F32)
    causal = col <= row

    for h in range(NH_A):
        q = qk[:, h * DQK_A:(h + 1) * DQK_A]
        k = qk[:, QK_A + h * DQK_A:QK_A + (h + 1) * DQK_A] * (DQK_A ** -0.5)
        v = v_ref[:, h * DV_A:(h + 1) * DV_A]
        qb, kb, vb = q.astype(BF16), k.astype(BF16), v.astype(BF16)
        i_col = xg[:, h:h + 1]
        b_col = bc[:, NH_A + h:NH_A + h + 1]
        i_row = xt[h:h + 1, :]
        b_row = bt[NH_A + h:NH_A + h + 1, :]
        m_prev = m_s[0:1, h:h + 1]
        C = c_s[h]
        nvec = n_s[h:h + 1, :]

        dmat = jnp.where(causal, b_col - b_row + i_row, -jnp.inf)
        inter = b_col + m_prev
        m_t = jnp.maximum(inter, jnp.max(dmat, axis=1, keepdims=True))
        e = jnp.exp(dmat - m_t)
        s = lax.dot_general(qb, kb, (((1,), (1,)), ((), ())), preferred_element_type=F32) * e
        w_inter = jnp.exp(inter - m_t)
        num = (jnp.dot(s.astype(BF16), vb, preferred_element_type=F32)
               + w_inter * jnp.dot(qb, C.astype(BF16), preferred_element_type=F32))
        den = jnp.sum(s, axis=1, keepdims=True) + w_inter * jnp.sum(q * nvec, axis=1, keepdims=True)
        hh = num / jnp.maximum(jnp.abs(den), jnp.exp(-m_t))
        hh = hh * lax.rsqrt(jnp.mean(hh * hh, axis=1, keepdims=True) + EPS)
        hh = hh * gh_ref[:, h * DV_A:(h + 1) * DV_A]
        hh = jax.nn.sigmoid(o_ref[:, h * DV_A:(h + 1) * DV_A]) * hh
        h_ref[:, h * DV_A:(h + 1) * DV_A] = hh.astype(h_ref.dtype)

        b_end = b_col[c - 1:c, :]
        dec = b_end - b_col + i_col
        m_new = jnp.maximum(b_end + m_prev, jnp.max(dec, axis=0, keepdims=True))
        wk = jnp.exp(dec - m_new)
        w_old = jnp.exp(b_end + m_prev - m_new)
        kw = k * wk
        c_s[h] = w_old * C + lax.dot_general(kw.astype(BF16), vb, dn0, preferred_element_type=F32)
        n_s[h:h + 1, :] = w_old * nvec + jnp.sum(kw, axis=0, keepdims=True)
        m_s[0:1, h:h + 1] = m_new

    @pl.when(j == nc - 1)
    def _():
        cout_ref[...] = c_s[...]
        nout_ref[...] = n_s[...]
        mout_ref[...] = m_s[...]
        convout_ref[...] = last_rows


def mlstm(P, G, row0, N, L, b_if, conv_w, conv_b, g_head, conv_buf, C0, n0, m0):
    c = min(L, CHUNK)
    nc = L // c
    base = row0 // c
    m0p = jnp.zeros((N, 1, LANES), F32).at[:, 0, :NH_A].set(m0)
    bif = jnp.zeros((1, LANES), F32).at[0, :2 * NH_A].set(b_if)
    rowmap = lambda col: (lambda n, j: (base + n * nc + j, col))
    st4 = lambda n, j: (n, 0, 0, 0)
    st3 = lambda n, j: (n, 0, 0)
    outs = pl.pallas_call(
        functools.partial(_mlstm_kernel, c=c, nc=nc),
        out_shape=(jax.ShapeDtypeStruct((N * L, D_A), BF16),
                   jax.ShapeDtypeStruct((N, NH_A, DQK_A, DV_A), F32),
                   jax.ShapeDtypeStruct((N, NH_A, DQK_A), F32),
                   jax.ShapeDtypeStruct((N, 1, LANES), F32),
                   jax.ShapeDtypeStruct((N, CONV_W - 1, 2 * QK_A), F32)),
        grid=(N, nc),
        in_specs=[pl.BlockSpec((c, 2 * QK_A), rowmap(0)),
                  pl.BlockSpec((c, D_A), rowmap(1)),
                  pl.BlockSpec((c, D_A), rowmap(2)),
                  pl.BlockSpec((c, LANES), rowmap(0)),
                  pl.BlockSpec((1, LANES), lambda n, j: (0, 0)),
                  pl.BlockSpec((CONV_W, 2 * QK_A), lambda n, j: (0, 0)),
                  pl.BlockSpec((1, 2 * QK_A), lambda n, j: (0, 0)),
                  pl.BlockSpec((1, D_A), lambda n, j: (0, 0)),
                  pl.BlockSpec((None, CONV_W - 1, 2 * QK_A), st3),
                  pl.BlockSpec((None, NH_A, DQK_A, DV_A), st4),
                  pl.BlockSpec((None, NH_A, DQK_A), st3),
                  pl.BlockSpec((None, 1, LANES), st3)],
        out_specs=(pl.BlockSpec((c, D_A), lambda n, j: (n * nc + j, 0)),
                   pl.BlockSpec((None, NH_A, DQK_A, DV_A), st4),
                   pl.BlockSpec((None, NH_A, DQK_A), st3),
                   pl.BlockSpec((None, 1, LANES), st3),
                   pl.BlockSpec((None, CONV_W - 1, 2 * QK_A), st3)),
        scratch_shapes=[pltpu.VMEM((c + 8, 2 * QK_A), F32),
                        pltpu.VMEM((NH_A, DQK_A, DV_A), F32),
                        pltpu.VMEM((NH_A, DQK_A), F32),
                        pltpu.VMEM((1, LANES), F32)],
        compiler_params=_params(("parallel", "arbitrary")),
        name="mlstm",
    )(P, P, P, G, bif, conv_w, conv_b.reshape(1, -1), g_head.reshape(1, -1), conv_buf, C0, n0, m0p)
    h, C1, n1, m1, cv = outs
    return h, C1, n1, m1[:, 0, :NH_A], cv


STICK_UNDERFLOW = -110.0


def _later_matrix(tk):
    jr = lax.broadcasted_iota(jnp.int32, (tk, tk), 0)
    sc = lax.broadcasted_iota(jnp.int32, (tk, tk), 1)
    return (jr > sc).astype(BF16)


def _sb_block(q_bf, k_blk, v_blk, R, later, mask):
    z = lax.dot_general(q_bf, k_blk.astype(BF16), (((1,), (1,)), ((), ())),
                        preferred_element_type=F32) * (DH_B ** -0.5)
    l1mb = -(jnp.maximum(z, 0.0) + jnp.log(1.0 + jnp.exp(-jnp.abs(z))))
    if mask is not None:
        l1mb = jnp.where(mask, l1mb, 0.0)
    hi = l1mb.astype(BF16)
    lo = (l1mb - hi.astype(F32)).astype(BF16)
    rest = (jnp.dot(hi, later, preferred_element_type=F32)
            + jnp.dot(lo, later, preferred_element_type=F32)) + R
    a = jnp.exp(z + l1mb + rest)
    if mask is not None:
        a = jnp.where(mask, a, 0.0)
    pv = jnp.dot(a.astype(BF16), v_blk.astype(BF16), preferred_element_type=F32)
    return pv, R + jnp.sum(l1mb, axis=1, keepdims=True)


def _sb_kernel(q_ref, k_ref, v_ref, o_ref, *, tq):
    qi = pl.program_id(2)
    q_bf = q_ref[...].astype(BF16)
    rr = lax.broadcasted_iota(jnp.int32, (tq, tq), 0)
    cc = lax.broadcasted_iota(jnp.int32, (tq, tq), 1)
    later = _later_matrix(tq)
    start = pl.multiple_of(qi * tq, tq)
    acc, R = _sb_block(q_bf, k_ref[pl.ds(start, tq), :], v_ref[pl.ds(start, tq), :],
                       jnp.zeros((tq, 1), F32), later, cc < rr)

    def cond(c):
        return jnp.logical_and(c[0] < qi, c[3] > STICK_UNDERFLOW)

    def body(c):
        t, acc, R, _ = c
        st = pl.multiple_of((qi - 1 - t) * tq, tq)
        pv, R = _sb_block(q_bf, k_ref[pl.ds(st, tq), :], v_ref[pl.ds(st, tq), :], R, later, None)
        return t + 1, acc + pv, R, jnp.max(R)

    _, acc, _, _ = lax.while_loop(cond, body, (jnp.int32(0), acc, R, jnp.max(R)))
    o_ref[...] = acc.astype(o_ref.dtype)


QB_BLK, KB_BLK, VB_BLK = 3 * D_B // DH_B, 4 * D_B // DH_B, 5 * D_B // DH_B


def sb_attention(P, row0, N, L):
    tq = min(L, 256)
    nq = L // tq
    return pl.pallas_call(
        functools.partial(_sb_kernel, tq=tq),
        out_shape=jax.ShapeDtypeStruct((N * L, D_B), BF16),
        grid=(N, NH_B, nq),
        in_specs=[pl.BlockSpec((tq, DH_B), lambda n, h, i: (row0 // tq + n * nq + i, QB_BLK + h)),
                  pl.BlockSpec((L, DH_B), lambda n, h, i: (row0 // L + n, KB_BLK + h)),
                  pl.BlockSpec((L, DH_B), lambda n, h, i: (row0 // L + n, VB_BLK + h))],
        out_specs=pl.BlockSpec((tq, DH_B), lambda n, h, i: (n * nq + i, h)),
        compiler_params=_params(("parallel", "parallel", "arbitrary")),
        name="sb_attention",
    )(P, P, P)


def _sb_past_kernel(q_ref, k_ref, v_ref, pk_hbm, pv_hbm, o_ref, kbuf, vbuf, ksem, vsem, *, L, tk, n_blk):
    n = pl.program_id(0)

    def copies(blk):
        st = blk * tk
        return ([pltpu.make_async_copy(pk_hbm.at[0, n, pl.ds(st, tk), h, :], kbuf.at[h], ksem)
                 for h in range(NH_B)]
                + [pltpu.make_async_copy(pv_hbm.at[0, n, pl.ds(st, tk), h, :], vbuf.at[h], vsem)
                   for h in range(NH_B)])

    rr = lax.broadcasted_iota(jnp.int32, (L, L), 0)
    cc = lax.broadcasted_iota(jnp.int32, (L, L), 1)
    later_new = _later_matrix(L)
    later = _later_matrix(tk)
    q_bf = [q_ref[:, h * DH_B:(h + 1) * DH_B].astype(BF16) for h in range(NH_B)]
    accs, Rs = [], []
    for h in range(NH_B):
        sl = slice(h * DH_B, (h + 1) * DH_B)
        acc, R = _sb_block(q_bf[h], k_ref[:, sl], v_ref[:, sl], jnp.zeros((L, 1), F32), later_new, cc < rr)
        accs.append(acc)
        Rs.append(R)

    def rmax(Rs):
        return jnp.max(jnp.concatenate(Rs, axis=1))

    def cond(c):
        return jnp.logical_and(c[0] < n_blk, c[3] > STICK_UNDERFLOW)

    def body(c):
        t, accs, Rs, _ = c
        cps = copies(n_blk - 1 - t)
        for cp in cps:
            cp.start()
        for cp in cps:
            cp.wait()
        new_accs, new_Rs = [], []
        for h in range(NH_B):
            pv, R = _sb_block(q_bf[h], kbuf[h], vbuf[h], Rs[h], later, None)
            new_accs.append(accs[h] + pv)
            new_Rs.append(R)
        return t + 1, tuple(new_accs), tuple(new_Rs), rmax(new_Rs)

    _, accs, _, _ = lax.while_loop(cond, body, (jnp.int32(0), tuple(accs), tuple(Rs), rmax(Rs)))
    for h in range(NH_B):
        o_ref[:, h * DH_B:(h + 1) * DH_B] = accs[h].astype(o_ref.dtype)


def sb_attention_past(P, row0, N, L, past_k, past_v):
    Lp = past_k.shape[2]
    tk = min(Lp, 256)
    blk = lambda c: pl.BlockSpec((L, D_B), lambda n: (row0 // L + n, c))
    return pl.pallas_call(
        functools.partial(_sb_past_kernel, L=L, tk=tk, n_blk=Lp // tk),
        out_shape=jax.ShapeDtypeStruct((N * L, D_B), BF16),
        grid=(N,),
        in_specs=[blk(3), blk(4), blk(5),
                  pl.BlockSpec(memory_space=pl.ANY), pl.BlockSpec(memory_space=pl.ANY)],
        out_specs=pl.BlockSpec((L, D_B), lambda n: (n, 0)),
        scratch_shapes=[pltpu.VMEM((NH_B, tk, DH_B), F32), pltpu.VMEM((NH_B, tk, DH_B), F32),
                        pltpu.SemaphoreType.DMA(()), pltpu.SemaphoreType.DMA(())],
        compiler_params=_params(("arbitrary",)),
        name="sb_attention_past",
    )(P, P, P, past_k, past_v)


def _merge_kernel(h_ref, ha_ref, hb_ref, wga_ref, wgb_ref, wpa_ref, wpb_ref, o_ref):
    h = h_ref[...]
    ga = jnp.dot(h, wga_ref[...], preferred_element_type=F32)
    gb = jnp.dot(h, wgb_ref[...], preferred_element_type=F32)
    pa = jnp.dot(ha_ref[...], wpa_ref[...], preferred_element_type=F32)
    pb = jnp.dot(hb_ref[...], wpb_ref[...], preferred_element_type=F32)
    o_ref[...] = (jax.nn.sigmoid(ga) * pa + jax.nn.sigmoid(gb) * pb).astype(o_ref.dtype)


def merge(h, ha, hb, wga, wgb, wpa, wpb):
    T, D = h.shape
    tm = _pick(T, (640, 512, 256, 128, 64, 32, 16))
    tn = _pick(D, (512, 256, 128))
    row = lambda i, j: (i, 0)
    colw = lambda i, j: (0, j)
    return pl.pallas_call(
        _merge_kernel,
        out_shape=jax.ShapeDtypeStruct((T, D), BF16),
        grid=(T // tm, D // tn),
        in_specs=[pl.BlockSpec((tm, D), row), pl.BlockSpec((tm, D_A), row), pl.BlockSpec((tm, D_B), row),
                  pl.BlockSpec((D, tn), colw), pl.BlockSpec((D, tn), colw),
                  pl.BlockSpec((D_A, tn), colw), pl.BlockSpec((D_B, tn), colw)],
        out_specs=pl.BlockSpec((tm, tn), lambda i, j: (i, j)),
        compiler_params=_params(("parallel", "arbitrary")),
        name="merge",
    )(h, ha, hb, wga, wgb, wpa, wpb)


def _wout_route_kernel(mg_ref, x_ref, wo_ref, g_ref, wr_ref, br_ref, x1_ref, t_ref, r_ref):
    x1 = x_ref[...] + jnp.dot(mg_ref[...], wo_ref[...], preferred_element_type=F32)
    x1_ref[...] = x1
    t = _rms(x1, g_ref[...])
    t_ref[...] = t
    lg = jnp.dot(t.astype(BF16), wr_ref[...], preferred_element_type=F32) + br_ref[...]
    tm = lg.shape[0]
    lane = lax.broadcasted_iota(jnp.int32, (tm, LANES), 1)
    ninf = -jnp.inf
    glm = jnp.where(lane < N_GROUPS, lg, ninf)
    gmax = jnp.max(glm, axis=1, keepdims=True)
    gsel = jnp.min(jnp.where(glm == gmax, lane, LANES), axis=1, keepdims=True)
    p_sel = 1.0 / jnp.sum(jnp.exp(glm - gmax), axis=1, keepdims=True)
    lo = N_GROUPS + E_PER_GROUP * gsel
    elm = jnp.where((lane >= lo) & (lane < lo + E_PER_GROUP), lg, ninf)
    m1 = jnp.max(elm, axis=1, keepdims=True)
    i1 = jnp.min(jnp.where(elm == m1, lane, LANES), axis=1, keepdims=True)
    elm2 = jnp.where(lane == i1, ninf, elm)
    m2 = jnp.max(elm2, axis=1, keepdims=True)
    i2 = jnp.min(jnp.where(elm2 == m2, lane, LANES), axis=1, keepdims=True)
    e2 = jnp.exp(m2 - m1)
    w1 = p_sel / (1.0 + e2)
    w2 = p_sel * e2 / (1.0 + e2)
    r = jnp.where(lane == 0, (i1 - N_GROUPS).astype(F32),
        jnp.where(lane == 1, (i2 - N_GROUPS).astype(F32),
        jnp.where(lane == 2, w1, jnp.where(lane == 3, w2, 0.0))))
    r_ref[...] = r


def wout_route(mg, x, wo, g_ffn, wr, br):
    T, D = x.shape
    tm = _pick(T, (256, 128, 64, 32, 16, 8))
    row = lambda i: (i, 0)
    fix = lambda i: (0, 0)
    return pl.pallas_call(
        _wout_route_kernel,
        out_shape=(jax.ShapeDtypeStruct((T, D), F32), jax.ShapeDtypeStruct((T, D), F32),
                   jax.ShapeDtypeStruct((T, LANES), F32)),
        grid=(T // tm,),
        in_specs=[pl.BlockSpec((tm, D), row), pl.BlockSpec((tm, D), row), pl.BlockSpec((D, D), fix),
                  pl.BlockSpec((1, D), fix), pl.BlockSpec((D, LANES), fix), pl.BlockSpec((1, LANES), fix)],
        out_specs=(pl.BlockSpec((tm, D), row), pl.BlockSpec((tm, D), row), pl.BlockSpec((tm, LANES), row)),
        compiler_params=_params(("parallel",)),
        name="wout_route",
    )(mg, x, wo, g_ffn.reshape(1, D), wr, br)


def _dispatch_kernel(pos_hbm, t_ref, xs_in, xs_out, idx, sem, isem, *, tm):
    del xs_in
    i = pl.program_id(0)
    cp = pltpu.make_async_copy(pos_hbm.at[pl.ds(i * 2 * tm, 2 * tm)], idx, isem)
    cp.start()
    cp.wait()

    def row_copy(r, k):
        return pltpu.make_async_copy(t_ref.at[pl.ds(r, 1), :], xs_out.at[pl.ds(idx[2 * r + k], 1), :], sem)

    def start(r, _):
        row_copy(r, 0).start()
        row_copy(r, 1).start()
        return 0
    lax.fori_loop(0, tm, start, 0)

    def wait(r, _):
        row_copy(r, 0).wait()
        row_copy(r, 1).wait()
        return 0
    lax.fori_loop(0, tm, wait, 0)


def dispatch(t, pos, a_pad):
    T, D = t.shape
    tm = _pick(T, (256, 128, 64, 32, 16, 8))
    xs0 = jnp.zeros((a_pad, D), t.dtype)
    return pl.pallas_call(
        functools.partial(_dispatch_kernel, tm=tm),
        out_shape=jax.ShapeDtypeStruct((a_pad, D), t.dtype),
        grid=(T // tm,),
        in_specs=[pl.BlockSpec(memory_space=pl.ANY),
                  pl.BlockSpec((tm, D), lambda i: (i, 0)),
                  pl.BlockSpec(memory_space=pl.ANY)],
        out_specs=pl.BlockSpec(memory_space=pl.ANY),
        scratch_shapes=[pltpu.SMEM((2 * tm,), jnp.int32), pltpu.SemaphoreType.DMA(()),
                        pltpu.SemaphoreType.DMA(())],
        input_output_aliases={2: 0},
        compiler_params=_params(("arbitrary",)),
        name="moe_dispatch",
    )(pos, t, xs0)


def _expert_kernel(te_ref, nu_ref, x_ref, wg_ref, wu_ref, wd_ref, y_ref, wg_s, wu_s, wd_s):
    i = pl.program_id(0)
    prev = te_ref[jnp.maximum(i - 1, 0)]

    @pl.when((i == 0) | (te_ref[i] != prev))
    def _():
        wg_s[...] = wg_ref[...].astype(BF16)
        wu_s[...] = wu_ref[...].astype(BF16)
        wd_s[...] = wd_ref[...].astype(BF16)

    @pl.when(i < nu_ref[0])
    def _():
        x = x_ref[...].astype(BF16)
        a = jnp.dot(x, wg_s[...], preferred_element_type=F32)
        u = jnp.dot(x, wu_s[...], preferred_element_type=F32)
        hid = (a * jax.nn.sigmoid(a) * u).astype(BF16)
        y_ref[...] = jnp.dot(hid, wd_s[...], preferred_element_type=F32)

    @pl.when(i >= nu_ref[0])
    def _():
        y_ref[...] = jnp.zeros_like(y_ref)


def experts(xs, tile_expert, n_used, w_gate, w_up, w_down):
    A, D = xs.shape
    F = w_gate.shape[-1]
    nt = A // EXPERT_TILE
    grid_spec = pltpu.PrefetchScalarGridSpec(
        num_scalar_prefetch=2,
        grid=(nt,),
        in_specs=[pl.BlockSpec((EXPERT_TILE, D), lambda i, te, nu: (i, 0)),
                  pl.BlockSpec((None, D, F), lambda i, te, nu: (te[i], 0, 0)),
                  pl.BlockSpec((None, D, F), lambda i, te, nu: (te[i], 0, 0)),
                  pl.BlockSpec((None, F, D), lambda i, te, nu: (te[i], 0, 0))],
        out_specs=pl.BlockSpec((EXPERT_TILE, D), lambda i, te, nu: (i, 0)),
        scratch_shapes=[pltpu.VMEM((D, F), BF16), pltpu.VMEM((D, F), BF16), pltpu.VMEM((F, D), BF16)],
    )
    return pl.pallas_call(
        _expert_kernel,
        out_shape=jax.ShapeDtypeStruct((A, D), F32),
        grid_spec=grid_spec,
        compiler_params=_params(("arbitrary",)),
        name="moe_experts",
    )(tile_expert, n_used, xs, w_gate.reshape(N_EXPERTS, D, F), w_up.reshape(N_EXPERTS, D, F),
      w_down.reshape(N_EXPERTS, F, D))


def _combine_kernel(pos_hbm, ys_hbm, x1_ref, r_ref, p_ref, wple_ref, wpg_ref, gple_ref, gfin_ref,
                    y_ref, idx, ybuf, sem, isem, *, tm):
    i = pl.program_id(0)
    cp = pltpu.make_async_copy(pos_hbm.at[pl.ds(i * 2 * tm, 2 * tm)], idx, isem)
    cp.start()
    cp.wait()

    def row_copy(r, k):
        return pltpu.make_async_copy(ys_hbm.at[pl.ds(idx[2 * r + k], 1), :], ybuf.at[k, pl.ds(r, 1), :], sem)

    def start(r, _):
        row_copy(r, 0).start()
        row_copy(r, 1).start()
        return 0
    lax.fori_loop(0, tm, start, 0)

    def wait(r, _):
        row_copy(r, 0).wait()
        row_copy(r, 1).wait()
        return 0
    lax.fori_loop(0, tm, wait, 0)

    r = r_ref[...]
    x2 = x1_ref[...] + r[:, 2:3] * ybuf[0] + r[:, 3:4] * ybuf[1]
    hp = _rms(x2, gple_ref[...]).astype(BF16)
    gate = jax.nn.sigmoid(jnp.dot(hp, wpg_ref[...], preferred_element_type=F32))
    emb = jnp.dot(p_ref[...].astype(BF16), wple_ref[...], preferred_element_type=F32)
    x3 = x2 + emb * gate
    y_ref[...] = _rms(x3, gfin_ref[...])


def combine(pos, ys, x1, route, p, w_ple, w_ple_gate, g_ple, g_final):
    T, D = x1.shape
    tm = _pick(T, (256, 128, 64, 32, 16, 8))
    row = lambda i: (i, 0)
    fix = lambda i: (0, 0)
    return pl.pallas_call(
        functools.partial(_combine_kernel, tm=tm),
        out_shape=jax.ShapeDtypeStruct((T, D), F32),
        grid=(T // tm,),
        in_specs=[pl.BlockSpec(memory_space=pl.ANY), pl.BlockSpec(memory_space=pl.ANY),
                  pl.BlockSpec((tm, D), row), pl.BlockSpec((tm, LANES), row),
                  pl.BlockSpec((tm, p.shape[1]), row),
                  pl.BlockSpec(w_ple.shape, fix), pl.BlockSpec((D, D), fix),
                  pl.BlockSpec((1, D), fix), pl.BlockSpec((1, D), fix)],
        out_specs=pl.BlockSpec((tm, D), row),
        scratch_shapes=[pltpu.SMEM((2 * tm,), jnp.int32), pltpu.VMEM((2, tm, D), F32),
                        pltpu.SemaphoreType.DMA(()), pltpu.SemaphoreType.DMA(())],
        compiler_params=_params(("arbitrary",)),
        name="moe_combine_ple",
    )(pos, ys, x1, route, p, w_ple, w_ple_gate, g_ple.reshape(1, D), g_final.reshape(1, D))


def _routing_tables(ids, n_tiles):
    flat = ids.reshape(-1)
    onehot = (flat[:, None] == jnp.arange(N_EXPERTS, dtype=jnp.int32)[None, :]).astype(jnp.int32)
    csum = jnp.cumsum(onehot, axis=0)
    counts = csum[-1]
    rank = jnp.sum((csum - onehot) * onehot, axis=1)
    tiles = (counts + EXPERT_TILE - 1) // EXPERT_TILE
    tile_end = jnp.cumsum(tiles)
    tile_start = tile_end - tiles
    pos = (tile_start[flat] * EXPERT_TILE + rank).astype(jnp.int32)
    n_used = tile_end[-1]
    tidx = jnp.minimum(jnp.arange(n_tiles, dtype=jnp.int32), n_used - 1)
    tile_expert = jnp.sum((tidx[:, None] >= tile_end[None, :]).astype(jnp.int32), axis=1)
    return pos, tile_expert.astype(jnp.int32), n_used.reshape(1).astype(jnp.int32)


def kernel(x_prompt, x_sample, cache_k, cache_v, state_conv, state_C, state_n, state_m, p_prompt, p_sample,
           g_mix, w_in, b_if, conv_w, conv_b, g_head_a, w_proj_a, w_proj_b, w_out, g_ffn, w_route_g,
           b_route_g, w_route_e, b_route_e, w_exp_gate, w_exp_up, w_exp_down, g_ple, w_ple, w_ple_gate,
           g_final):
    assert w_in.shape[0] == 1, "single layer"
    B, S, D = x_prompt.shape
    DB, DS, _ = x_sample.shape
    Tp, Ts = B * S, DB * DS
    T = Tp + Ts
    x = jnp.concatenate([x_prompt.reshape(Tp, D), x_sample.reshape(Ts, D)], axis=0)
    p = jnp.concatenate([p_prompt[0].reshape(Tp, -1), p_sample[0].reshape(Ts, -1)], axis=0)

    wi = w_in[0]
    g0 = 2 * QK_A + D_A
    m0 = g0 + 2 * NH_A
    e0 = m0 + D_A + 3 * D_B
    w_main = jnp.concatenate([wi[:, :g0], wi[:, m0:e0]], axis=1).astype(BF16)
    w_gates = jnp.zeros((D, LANES), F32).at[:, :2 * NH_A].set(wi[:, g0:m0]).astype(BF16)
    wga = wi[:, e0:e0 + D].astype(BF16)
    wgb = wi[:, e0 + D:].astype(BF16)

    h = rms_cast(x, g_mix[0])
    P = matmul(h, w_main, "proj_main")
    G = matmul(h, w_gates, "proj_gates")

    zeros = functools.partial(jnp.zeros, dtype=F32)
    ha_p, C_p, n_p, m_p, cv_p = mlstm(P, G, 0, B, S, b_if[0], conv_w[0], conv_b[0], g_head_a[0],
                                      zeros((B, CONV_W - 1, 2 * QK_A)), zeros((B, NH_A, DQK_A, DV_A)),
                                      zeros((B, NH_A, DQK_A)), zeros((B, NH_A)))
    ha_s, C_s, n_s, m_s, cv_s = mlstm(P, G, Tp, DB, DS, b_if[0], conv_w[0], conv_b[0], g_head_a[0],
                                      state_conv[0], state_C[0], state_n[0], state_m[0])
    hb_p = sb_attention(P, 0, B, S)
    hb_s = sb_attention_past(P, Tp, DB, DS, cache_k, cache_v)
    ha = jnp.concatenate([ha_p, ha_s], axis=0)
    hb = jnp.concatenate([hb_p, hb_s], axis=0)

    mg = merge(h, ha, hb, wga, wgb, w_proj_a[0].astype(BF16), w_proj_b[0].astype(BF16))

    wr = jnp.zeros((D, LANES), F32)
    wr = wr.at[:, :N_GROUPS].set(w_route_g[0])
    wr = wr.at[:, N_GROUPS:N_GROUPS + N_EXPERTS].set(
        jnp.transpose(w_route_e[0], (1, 0, 2)).reshape(D, N_EXPERTS))
    br = jnp.zeros((1, LANES), F32)
    br = br.at[0, :N_GROUPS].set(b_route_g[0])
    br = br.at[0, N_GROUPS:N_GROUPS + N_EXPERTS].set(b_route_e[0].reshape(-1))
    x1, t, route = wout_route(mg, x, w_out[0].astype(BF16), g_ffn[0], wr.astype(BF16), br)

    ids = route[:, :2].astype(jnp.int32)
    n_tiles = (2 * T + N_EXPERTS * (EXPERT_TILE - 1)) // EXPERT_TILE + 1
    pos, tile_expert, n_used = _routing_tables(ids, n_tiles)
    xs = dispatch(t, pos, n_tiles * EXPERT_TILE)
    ys = experts(xs, tile_expert, n_used, w_exp_gate[0], w_exp_up[0], w_exp_down[0])
    y = combine(pos, ys, x1, route, p, w_ple[0].astype(BF16), w_ple_gate[0].astype(BF16), g_ple[0], g_final)

    kq = P[:, 4 * D_B:5 * D_B]
    vq = P[:, 5 * D_B:6 * D_B]
    return (y[:Tp].reshape(B, S, D), y[Tp:].reshape(DB, DS, D),
            kq[:Tp].reshape(1, B, S, NH_B, DH_B), vq[:Tp].reshape(1, B, S, NH_B, DH_B),
            cv_p[None], C_p[None], n_p[None], m_p[None],
            kq[Tp:].reshape(1, DB, DS, NH_B, DH_B), vq[Tp:].reshape(1, DB, DS, NH_B, DH_B),
            cv_s[None], C_s[None], n_s[None], m_s[None])
```

```python
import functools
import math

import jax
import jax.numpy as jnp
from jax import lax
from jax.experimental import pallas as pl
from jax.experimental.pallas import tpu as pltpu

F32 = jnp.float32
BF16 = jnp.bfloat16
EPS = 1e-6

CHUNK = 64
NH_A, DQK_A, DV_A = 4, 128, 256
QK_A, D_A = NH_A * DQK_A, NH_A * DV_A
CONV_W = 4
NH_B, DH_B = 8, 128
D_B = NH_B * DH_B
N_GROUPS, E_PER_GROUP = 4, 8
N_EXPERTS = N_GROUPS * E_PER_GROUP
LANES = 128
EXPERT_TILE = 256
VMEM_LIMIT = 56 * 1024 * 1024


def _pick(n, cands):
    for c in cands:
        if n % c == 0:
            return c
    raise ValueError(f"no tile for {n} in {cands}")


def _params(sem):
    return pltpu.CompilerParams(dimension_semantics=sem, vmem_limit_bytes=VMEM_LIMIT)


def _rms(x, g):
    return x * lax.rsqrt(jnp.mean(x * x, axis=-1, keepdims=True) + EPS) * g


def _log_sigmoid(x):
    return jnp.minimum(x, 0.0) - jnp.log(1.0 + jnp.exp(-jnp.abs(x)))


def _two_stream_specs(tm, D, n_first):
    return [pl.BlockSpec((tm, D), lambda i: (jnp.minimum(i, n_first - 1), 0)),
            pl.BlockSpec((tm, D), lambda i: (jnp.maximum(i - n_first, 0), 0))]


def _rms_kernel(xp_ref, xs_ref, g_ref, o_ref, *, n_first):
    x = jnp.where(pl.program_id(0) < n_first, xp_ref[...], xs_ref[...])
    o_ref[...] = _rms(x, g_ref[...]).astype(o_ref.dtype)


def rms_cast(xp, xs, g, tm):
    (Tp, D), Ts = xp.shape, xs.shape[0]
    T = Tp + Ts
    return pl.pallas_call(
        functools.partial(_rms_kernel, n_first=Tp // tm),
        out_shape=jax.ShapeDtypeStruct((T, D), BF16),
        grid=(T // tm,),
        in_specs=_two_stream_specs(tm, D, Tp // tm) + [pl.BlockSpec((1, D), lambda i: (0, 0))],
        out_specs=pl.BlockSpec((tm, D), lambda i: (i, 0)),
        compiler_params=_params(("arbitrary",)),
        name="rms_cast",
    )(xp, xs, g.reshape(1, D))


def _mm_kernel(a_ref, w_ref, o_ref):
    o_ref[...] = jnp.dot(a_ref[...], w_ref[...], preferred_element_type=F32)


def matmul(a, w, name):
    T, K = a.shape
    N = w.shape[1]
    tm = _pick(T, (1280, 640, 512, 256, 128, 64, 32, 16))
    tn = _pick(N, (512, 256, 128))
    return pl.pallas_call(
        _mm_kernel,
        out_shape=jax.ShapeDtypeStruct((T, N), F32),
        grid=(T // tm, N // tn),
        in_specs=[pl.BlockSpec((tm, K), lambda i, j: (i, 0)),
                  pl.BlockSpec((K, tn), lambda i, j: (0, j))],
        out_specs=pl.BlockSpec((tm, tn), lambda i, j: (i, j)),
        compiler_params=_params(("parallel", "arbitrary")),
        name=name,
    )(a, w)


def _mlstm_kernel(qk_ref, v_ref, o_ref, g_ref, bif_ref, cw_ref, cb_ref, gh_ref,
                  cbuf_ref, c0_ref, n0_ref, m0_ref,
                  h_ref, cout_ref, nout_ref, mout_ref, convout_ref,
                  xbuf, c_s, n_s, m_s, *, c, nc):
    j = pl.program_id(1)

    @pl.when(j == 0)
    def _():
        c_s[...] = c0_ref[...]
        n_s[...] = n0_ref[...]
        m_s[...] = m0_ref[...]
        xbuf[5:8, :] = cbuf_ref[...]

    xbuf[8:8 + c, :] = qk_ref[...]
    y = cb_ref[...] + cw_ref[0:1, :] * xbuf[5:5 + c, :]
    for t in range(1, CONV_W):
        y = y + cw_ref[t:t + 1, :] * xbuf[5 + t:5 + t + c, :]
    last_rows = xbuf[c + 5:c + 8, :]
    xbuf[5:8, :] = last_rows
    qk = y * jax.nn.sigmoid(y)

    lane = lax.broadcasted_iota(jnp.int32, (c, LANES), 1)
    graw = g_ref[...] + bif_ref[...]
    xg = jnp.where(lane < NH_A, graw, jnp.where(lane < 2 * NH_A, _log_sigmoid(graw), 0.0))
    row = lax.broadcasted_iota(jnp.int32, (c, c), 0)
    col = lax.broadcasted_iota(jnp.int32, (c, c), 1)
    tril = (col <= row).astype(F32)
    eye = (col == row).astype(F32)
    hp = lax.Precision.HIGHEST
    bc = jnp.dot(tril, xg, precision=hp, preferred_element_type=F32)
    dn0 = (((0,), (0,)), ((), ()))
    xt = lax.dot_general(xg, eye, dn0, precision=hp, preferred_element_type=F32)
    triu = (row <= col).astype(F32)
    bt = lax.dot_general(xg, triu, dn0, precision=hp, preferred_element_type=F32)
    causal = col <= row

    for h in range(NH_A):
        q = qk[:, h * DQK_A:(h + 1) * DQK_A]
        k = qk[:, QK_A + h * DQK_A:QK_A + (h + 1) * DQK_A] * (DQK_A ** -0.5)
        v = v_ref[:, h * DV_A:(h + 1) * DV_A]
        qb, kb, vb = q.astype(BF16), k.astype(BF16), v.astype(BF16)
        i_col = xg[:, h:h + 1]
        b_col = bc[:, NH_A + h:NH_A + h + 1]
        i_row = xt[h:h + 1, :]
        b_row = bt[NH_A + h:NH_A + h + 1, :]
        m_prev = m_s[0:1, h:h + 1]
        C = c_s[h]
        nvec = n_s[h:h + 1, :]

        dmat = jnp.where(causal, b_col - b_row + i_row, -jnp.inf)
        inter = b_col + m_prev
        m_t = jnp.maximum(inter, jnp.max(dmat, axis=1, keepdims=True))
        e = jnp.exp(dmat - m_t)
        s = lax.dot_general(qb, kb, (((1,), (1,)), ((), ())), preferred_element_type=F32) * e
        w_inter = jnp.exp(inter - m_t)
        num = (jnp.dot(s.astype(BF16), vb, preferred_element_type=F32)
               + w_inter * jnp.dot(qb, C.astype(BF16), preferred_element_type=F32))
        den = jnp.sum(s, axis=1, keepdims=True) + w_inter * jnp.sum(q * nvec, axis=1, keepdims=True)
        hh = num / jnp.maximum(jnp.abs(den), jnp.exp(-m_t))
        hh = hh * lax.rsqrt(jnp.mean(hh * hh, axis=1, keepdims=True) + EPS)
        hh = hh * gh_ref[:, h * DV_A:(h + 1) * DV_A]
        hh = jax.nn.sigmoid(o_ref[:, h * DV_A:(h + 1) * DV_A]) * hh
        h_ref[:, h * DV_A:(h + 1) * DV_A] = hh.astype(h_ref.dtype)

        b_end = b_col[c - 1:c, :]
        dec = b_end - b_col + i_col
        m_new = jnp.maximum(b_end + m_prev, jnp.max(dec, axis=0, keepdims=True))
        wk = jnp.exp(dec - m_new)
        w_old = jnp.exp(b_end + m_prev - m_new)
        kw = k * wk
        c_s[h] = w_old * C + lax.dot_general(kw.astype(BF16), vb, dn0, preferred_element_type=F32)
        n_s[h:h + 1, :] = w_old * nvec + jnp.sum(kw, axis=0, keepdims=True)
        m_s[0:1, h:h + 1] = m_new

    @pl.when(j == nc - 1)
    def _():
        cout_ref[...] = c_s[...]
        nout_ref[...] = n_s[...]
        mout_ref[...] = m_s[...]
        convout_ref[...] = last_rows


def mlstm(P, G, row0, N, L, b_if, conv_w, conv_b, g_head, conv_buf, C0, n0, m0):
    c = min(L, CHUNK)
    nc = L // c
    base = row0 // c
    m0p = jnp.zeros((N, 1, LANES), F32).at[:, 0, :NH_A].set(m0)
    bif = jnp.zeros((1, LANES), F32).at[0, :2 * NH_A].set(b_if)
    rowmap = lambda col: (lambda n, j: (base + n * nc + j, col))
    st4 = lambda n, j: (n, 0, 0, 0)
    st3 = lambda n, j: (n, 0, 0)
    outs = pl.pallas_call(
        functools.partial(_mlstm_kernel, c=c, nc=nc),
        out_shape=(jax.ShapeDtypeStruct((N * L, D_A), BF16),
                   jax.ShapeDtypeStruct((N, NH_A, DQK_A, DV_A), F32),
                   jax.ShapeDtypeStruct((N, NH_A, DQK_A), F32),
                   jax.ShapeDtypeStruct((N, 1, LANES), F32),
                   jax.ShapeDtypeStruct((N, CONV_W - 1, 2 * QK_A), F32)),
        grid=(N, nc),
        in_specs=[pl.BlockSpec((c, 2 * QK_A), rowmap(0)),
                  pl.BlockSpec((c, D_A), rowmap(1)),
                  pl.BlockSpec((c, D_A), rowmap(2)),
                  pl.BlockSpec((c, LANES), rowmap(0)),
                  pl.BlockSpec((1, LANES), lambda n, j: (0, 0)),
                  pl.BlockSpec((CONV_W, 2 * QK_A), lambda n, j: (0, 0)),
                  pl.BlockSpec((1, 2 * QK_A), lambda n, j: (0, 0)),
                  pl.BlockSpec((1, D_A), lambda n, j: (0, 0)),
                  pl.BlockSpec((None, CONV_W - 1, 2 * QK_A), st3),
                  pl.BlockSpec((None, NH_A, DQK_A, DV_A), st4),
                  pl.BlockSpec((None, NH_A, DQK_A), st3),
                  pl.BlockSpec((None, 1, LANES), st3)],
        out_specs=(pl.BlockSpec((c, D_A), lambda n, j: (n * nc + j, 0)),
                   pl.BlockSpec((None, NH_A, DQK_A, DV_A), st4),
                   pl.BlockSpec((None, NH_A, DQK_A), st3),
                   pl.BlockSpec((None, 1, LANES), st3),
                   pl.BlockSpec((None, CONV_W - 1, 2 * QK_A), st3)),
        scratch_shapes=[pltpu.VMEM((c + 8, 2 * QK_A), F32),
                        pltpu.VMEM((NH_A, DQK_A, DV_A), F32),
                        pltpu.VMEM((NH_A, DQK_A), F32),
                        pltpu.VMEM((1, LANES), F32)],
        compiler_params=_params(("parallel", "arbitrary")),
        name="mlstm",
    )(P, P, P, G, bif, conv_w, conv_b.reshape(1, -1), g_head.reshape(1, -1), conv_buf, C0, n0, m0p)
    h, C1, n1, m1, cv = outs
    return h, C1, n1, m1[:, 0, :NH_A], cv


STICK_UNDERFLOW = -110.0


def _later_matrix(tk):
    jr = lax.broadcasted_iota(jnp.int32, (tk, tk), 0)
    sc = lax.broadcasted_iota(jnp.int32, (tk, tk), 1)
    return (jr > sc).astype(BF16)


def _sb_block(q_bf, k_blk, v_blk, R, later, mask):
    z = lax.dot_general(q_bf, k_blk.astype(BF16), (((1,), (1,)), ((), ())),
                        preferred_element_type=F32) * (DH_B ** -0.5)
    l1mb = -(jnp.maximum(z, 0.0) + jnp.log(1.0 + jnp.exp(-jnp.abs(z))))
    if mask is not None:
        l1mb = jnp.where(mask, l1mb, 0.0)
    hi = l1mb.astype(BF16)
    lo = (l1mb - hi.astype(F32)).astype(BF16)
    rest = (jnp.dot(hi, later, preferred_element_type=F32)
            + jnp.dot(lo, later, preferred_element_type=F32)) + R
    a = jnp.exp(z + l1mb + rest)
    if mask is not None:
        a = jnp.where(mask, a, 0.0)
    pv = jnp.dot(a.astype(BF16), v_blk.astype(BF16), preferred_element_type=F32)
    return pv, R + jnp.sum(l1mb, axis=1, keepdims=True)


def _sb_kernel(q_ref, k_ref, v_ref, o_ref, *, tq):
    qi = pl.program_id(2)
    q_bf = q_ref[...].astype(BF16)
    rr = lax.broadcasted_iota(jnp.int32, (tq, tq), 0)
    cc = lax.broadcasted_iota(jnp.int32, (tq, tq), 1)
    later = _later_matrix(tq)
    start = pl.multiple_of(qi * tq, tq)
    acc, R = _sb_block(q_bf, k_ref[pl.ds(start, tq), :], v_ref[pl.ds(start, tq), :],
                       jnp.zeros((tq, 1), F32), later, cc < rr)

    def cond(c):
        return jnp.logical_and(c[0] < qi, c[3] > STICK_UNDERFLOW)

    def body(c):
        t, acc, R, _ = c
        st = pl.multiple_of((qi - 1 - t) * tq, tq)
        pv, R = _sb_block(q_bf, k_ref[pl.ds(st, tq), :], v_ref[pl.ds(st, tq), :], R, later, None)
        return t + 1, acc + pv, R, jnp.max(R)

    _, acc, _, _ = lax.while_loop(cond, body, (jnp.int32(0), acc, R, jnp.max(R)))
    o_ref[...] = acc.astype(o_ref.dtype)


QB_BLK, KB_BLK, VB_BLK = 3 * D_B // DH_B, 4 * D_B // DH_B, 5 * D_B // DH_B


def sb_attention(P, row0, N, L):
    tq = min(L, 256)
    nq = L // tq
    return pl.pallas_call(
        functools.partial(_sb_kernel, tq=tq),
        out_shape=jax.ShapeDtypeStruct((N * L, D_B), BF16),
        grid=(N, NH_B, nq),
        in_specs=[pl.BlockSpec((tq, DH_B), lambda n, h, i: (row0 // tq + n * nq + i, QB_BLK + h)),
                  pl.BlockSpec((L, DH_B), lambda n, h, i: (row0 // L + n, KB_BLK + h)),
                  pl.BlockSpec((L, DH_B), lambda n, h, i: (row0 // L + n, VB_BLK + h))],
        out_specs=pl.BlockSpec((tq, DH_B), lambda n, h, i: (n * nq + i, h)),
        compiler_params=_params(("parallel", "parallel", "arbitrary")),
        name="sb_attention",
    )(P, P, P)


def _sb_past_kernel(q_ref, k_ref, v_ref, pk_hbm, pv_hbm, o_ref, kbuf, vbuf, ksem, vsem, *, L, tk, n_blk):
    n = pl.program_id(0)

    def copies(blk):
        st = blk * tk
        return ([pltpu.make_async_copy(pk_hbm.at[0, n, pl.ds(st, tk), h, :], kbuf.at[h], ksem)
                 for h in range(NH_B)]
                + [pltpu.make_async_copy(pv_hbm.at[0, n, pl.ds(st, tk), h, :], vbuf.at[h], vsem)
                   for h in range(NH_B)])

    rr = lax.broadcasted_iota(jnp.int32, (L, L), 0)
    cc = lax.broadcasted_iota(jnp.int32, (L, L), 1)
    later_new = _later_matrix(L)
    later = _later_matrix(tk)
    q_bf = [q_ref[:, h * DH_B:(h + 1) * DH_B].astype(BF16) for h in range(NH_B)]
    accs, Rs = [], []
    for h in range(NH_B):
        sl = slice(h * DH_B, (h + 1) * DH_B)
        acc, R = _sb_block(q_bf[h], k_ref[:, sl], v_ref[:, sl], jnp.zeros((L, 1), F32), later_new, cc < rr)
        accs.append(acc)
        Rs.append(R)

    def rmax(Rs):
        return jnp.max(jnp.concatenate(Rs, axis=1))

    def cond(c):
        return jnp.logical_and(c[0] < n_blk, c[3] > STICK_UNDERFLOW)

    def body(c):
        t, accs, Rs, _ = c
        cps = copies(n_blk - 1 - t)
        for cp in cps:
            cp.start()
        for cp in cps:
            cp.wait()
        new_accs, new_Rs = [], []
        for h in range(NH_B):
            pv, R = _sb_block(q_bf[h], kbuf[h], vbuf[h], Rs[h], later, None)
            new_accs.append(accs[h] + pv)
            new_Rs.append(R)
        return t + 1, tuple(new_accs), tuple(new_Rs), rmax(new_Rs)

    _, accs, _, _ = lax.while_loop(cond, body, (jnp.int32(0), tuple(accs), tuple(Rs), rmax(Rs)))
    for h in range(NH_B):
        o_ref[:, h * DH_B:(h + 1) * DH_B] = accs[h].astype(o_ref.dtype)


def sb_attention_past(P, row0, N, L, past_k, past_v):
    Lp = past_k.shape[2]
    tk = min(Lp, 256)
    blk = lambda c: pl.BlockSpec((L, D_B), lambda n: (row0 // L + n, c))
    return pl.pallas_call(
        functools.partial(_sb_past_kernel, L=L, tk=tk, n_blk=Lp // tk),
        out_shape=jax.ShapeDtypeStruct((N * L, D_B), BF16),
        grid=(N,),
        in_specs=[blk(3), blk(4), blk(5),
                  pl.BlockSpec(memory_space=pl.ANY), pl.BlockSpec(memory_space=pl.ANY)],
        out_specs=pl.BlockSpec((L, D_B), lambda n: (n, 0)),
        scratch_shapes=[pltpu.VMEM((NH_B, tk, DH_B), F32), pltpu.VMEM((NH_B, tk, DH_B), F32),
                        pltpu.SemaphoreType.DMA(()), pltpu.SemaphoreType.DMA(())],
        compiler_params=_params(("arbitrary",)),
        name="sb_attention_past",
    )(P, P, P, past_k, past_v)


def _merge_kernel(h_ref, ha_ref, hb_ref, wga_ref, wgb_ref, wpa_ref, wpb_ref, o_ref):
    h = h_ref[...]
    ga = jnp.dot(h, wga_ref[...], preferred_element_type=F32)
    gb = jnp.dot(h, wgb_ref[...], preferred_element_type=F32)
    pa = jnp.dot(ha_ref[...], wpa_ref[...], preferred_element_type=F32)
    pb = jnp.dot(hb_ref[...], wpb_ref[...], preferred_element_type=F32)
    o_ref[...] = (jax.nn.sigmoid(ga) * pa + jax.nn.sigmoid(gb) * pb).astype(o_ref.dtype)


def merge(h, ha, hb, wga, wgb, wpa, wpb):
    T, D = h.shape
    tm = _pick(T, (640, 512, 256, 128, 64, 32, 16))
    tn = _pick(D, (512, 256, 128))
    row = lambda i, j: (i, 0)
    colw = lambda i, j: (0, j)
    return pl.pallas_call(
        _merge_kernel,
        out_shape=jax.ShapeDtypeStruct((T, D), BF16),
        grid=(T // tm, D // tn),
        in_specs=[pl.BlockSpec((tm, D), row), pl.BlockSpec((tm, D_A), row), pl.BlockSpec((tm, D_B), row),
                  pl.BlockSpec((D, tn), colw), pl.BlockSpec((D, tn), colw),
                  pl.BlockSpec((D_A, tn), colw), pl.BlockSpec((D_B, tn), colw)],
        out_specs=pl.BlockSpec((tm, tn), lambda i, j: (i, j)),
        compiler_params=_params(("parallel", "arbitrary")),
        name="merge",
    )(h, ha, hb, wga, wgb, wpa, wpb)


def _wout_route_kernel(mg_ref, xp_ref, xs_ref, wo_ref, g_ref, wr_ref, br_ref, x1_ref, t_ref, r_ref, cnt_ref,
                       run_s, *, n_first):
    i = pl.program_id(0)

    @pl.when(i == 0)
    def _():
        run_s[...] = jnp.zeros_like(run_s)

    x = jnp.where(i < n_first, xp_ref[...], xs_ref[...])
    x1 = x + jnp.dot(mg_ref[...], wo_ref[...], preferred_element_type=F32)
    x1_ref[...] = x1
    t = _rms(x1, g_ref[...])
    t_ref[...] = t
    lg = jnp.dot(t.astype(BF16), wr_ref[...], preferred_element_type=F32) + br_ref[...]
    tm = lg.shape[0]
    lane = lax.broadcasted_iota(jnp.int32, (tm, LANES), 1)
    ninf = -jnp.inf
    glm = jnp.where(lane < N_GROUPS, lg, ninf)
    gmax = jnp.max(glm, axis=1, keepdims=True)
    gsel = jnp.min(jnp.where(glm == gmax, lane, LANES), axis=1, keepdims=True)
    p_sel = 1.0 / jnp.sum(jnp.exp(glm - gmax), axis=1, keepdims=True)
    lo = N_GROUPS + E_PER_GROUP * gsel
    elm = jnp.where((lane >= lo) & (lane < lo + E_PER_GROUP), lg, ninf)
    m1 = jnp.max(elm, axis=1, keepdims=True)
    i1 = jnp.min(jnp.where(elm == m1, lane, LANES), axis=1, keepdims=True)
    elm2 = jnp.where(lane == i1, ninf, elm)
    m2 = jnp.max(elm2, axis=1, keepdims=True)
    i2 = jnp.min(jnp.where(elm2 == m2, lane, LANES), axis=1, keepdims=True)
    e2 = jnp.exp(m2 - m1)
    w1 = p_sel / (1.0 + e2)
    w2 = p_sel * e2 / (1.0 + e2)
    e1 = (lane == i1 - N_GROUPS).astype(F32)
    e2h = (lane == i2 - N_GROUPS).astype(F32)
    both = e1 + e2h
    rr = lax.broadcasted_iota(jnp.int32, (tm, tm), 0)
    cc = lax.broadcasted_iota(jnp.int32, (tm, tm), 1)
    before = jnp.dot((cc < rr).astype(BF16), both.astype(BF16), preferred_element_type=F32) + run_s[...]
    rank1 = jnp.sum(e1 * before, axis=1, keepdims=True)
    rank2 = jnp.sum(e2h * before, axis=1, keepdims=True)
    run_s[...] = run_s[...] + jnp.sum(both, axis=0, keepdims=True)
    cnt_ref[...] = run_s[...]
    r = jnp.where(lane == 0, (i1 - N_GROUPS).astype(F32),
        jnp.where(lane == 1, (i2 - N_GROUPS).astype(F32),
        jnp.where(lane == 2, w1, jnp.where(lane == 3, w2,
        jnp.where(lane == 4, rank1, jnp.where(lane == 5, rank2, 0.0))))))
    r_ref[...] = r


def wout_route(mg, xp, xs, wo, g_ffn, wr, br, tm):
    (Tp, D), Ts = xp.shape, xs.shape[0]
    T = Tp + Ts
    row = lambda i: (i, 0)
    fix = lambda i: (0, 0)
    return pl.pallas_call(
        functools.partial(_wout_route_kernel, n_first=Tp // tm),
        out_shape=(jax.ShapeDtypeStruct((T, D), F32), jax.ShapeDtypeStruct((T, D), F32),
                   jax.ShapeDtypeStruct((T, LANES), F32), jax.ShapeDtypeStruct((1, LANES), F32)),
        grid=(T // tm,),
        in_specs=[pl.BlockSpec((tm, D), row)] + _two_stream_specs(tm, D, Tp // tm)
                 + [pl.BlockSpec((D, D), fix), pl.BlockSpec((1, D), fix), pl.BlockSpec((D, LANES), fix),
                    pl.BlockSpec((1, LANES), fix)],
        out_specs=(pl.BlockSpec((tm, D), row), pl.BlockSpec((tm, D), row), pl.BlockSpec((tm, LANES), row),
                   pl.BlockSpec((1, LANES), fix)),
        scratch_shapes=[pltpu.VMEM((1, LANES), F32)],
        compiler_params=_params(("arbitrary",)),
        name="wout_route",
    )(mg, xp, xs, wo, g_ffn.reshape(1, D), wr, br)


IDX_RING = 8


def _expert_kernel(te_ref, nu_ref, src_hbm, dst_hbm, t_hbm, wg_ref, wu_ref, wd_ref, y_hbm,
                   sidx, didx, xb0, xb1, yb0, yb1, wg_s, wu_s, wd_s, gsem, ssem, isem, *, dump0):
    i = pl.program_id(0)
    nu = nu_ref[0]
    TE = EXPERT_TILE
    xb, yb = (xb0, xb1), (yb0, yb1)

    def table_copies(tile, seq):
        base = (seq & (IDX_RING - 1)) * TE
        return (pltpu.make_async_copy(src_hbm.at[pl.ds(tile * TE, TE)], sidx.at[pl.ds(base, TE)], isem.at[0]),
                pltpu.make_async_copy(dst_hbm.at[pl.ds(tile * TE, TE)], didx.at[pl.ds(base, TE)], isem.at[1]))

    def gather(seq, par):
        base = (seq & (IDX_RING - 1)) * TE
        return [pltpu.make_async_copy(t_hbm.at[pl.ds(sidx[base + r], 1), :], xb[par].at[pl.ds(r, 1), :],
                                      gsem.at[par]) for r in range(TE)]

    def scatter(seq, par):
        base = (seq & (IDX_RING - 1)) * TE
        return [pltpu.make_async_copy(yb[par].at[pl.ds(r, 1), :], y_hbm.at[pl.ds(didx[base + r], 1), :],
                                      ssem.at[par]) for r in range(TE)]

    def start(cps):
        for cp in cps:
            cp.start()

    def wait(cps):
        for cp in cps:
            cp.wait()

    last = nu - 1

    @pl.when(i == 0)
    def _():
        first = table_copies(0, 0)
        start(first)
        wait(first)
        start(table_copies(jnp.minimum(1, last), 1))
        start(gather(0, 0))
        yb1[...] = jnp.zeros_like(yb1)
        for r in range(TE):
            didx[(IDX_RING - 1) * TE + r] = dump0 + TE + r
        fill = pltpu.make_async_copy(yb1, y_hbm.at[pl.ds(dump0, TE), :], ssem.at[0])
        fill.start()
        fill.wait()

    @pl.when((i < nu) & ((i == 0) | (te_ref[i] != te_ref[jnp.maximum(i - 1, 0)])))
    def _():
        wg_s[...] = wg_ref[...].astype(BF16)
        wu_s[...] = wu_ref[...].astype(BF16)
        wd_s[...] = wd_ref[...].astype(BF16)

    def step(par):
        wait(table_copies(jnp.minimum(i + 1, last), i + 1))
        start(table_copies(jnp.minimum(i + 2, last), i + 2))

        @pl.when(i >= 1)
        def _():
            wait(scatter(i - 2, par))

        wait(gather(i, par))
        start(scatter(i - 1, 1 - par))
        start(gather(i + 1, 1 - par))
        x = xb[par][...].astype(BF16)
        a = jnp.dot(x, wg_s[...], preferred_element_type=F32)
        u = jnp.dot(x, wu_s[...], preferred_element_type=F32)
        hid = (a * jax.nn.sigmoid(a) * u).astype(BF16)
        yb[par][...] = jnp.dot(hid, wd_s[...], preferred_element_type=F32)

        @pl.when(i == last)
        def _():
            start(scatter(i, par))
            wait(scatter(i - 1, 1 - par))
            wait(scatter(i, par))
            wait(gather(i + 1, 1 - par))
            wait(table_copies(last, i + 2))

    for par in (0, 1):
        pl.when((i < nu) & (i % 2 == par))(functools.partial(step, par))


def experts(t, slot_src, slot_dst, tile_expert, n_used, w_gate, w_up, w_down):
    T, D = t.shape
    F = w_gate.shape[-1]
    nt = slot_src.shape[0] // EXPERT_TILE
    wmap = lambda i, te, nu: (te[i], 0, 0)
    any_spec = pl.BlockSpec(memory_space=pl.ANY)
    tile_buf = pltpu.VMEM((EXPERT_TILE, D), F32)
    grid_spec = pltpu.PrefetchScalarGridSpec(
        num_scalar_prefetch=2,
        grid=(nt,),
        in_specs=[any_spec, any_spec, any_spec,
                  pl.BlockSpec((None, D, F), wmap), pl.BlockSpec((None, D, F), wmap),
                  pl.BlockSpec((None, F, D), wmap)],
        out_specs=any_spec,
        scratch_shapes=[pltpu.SMEM((IDX_RING * EXPERT_TILE,), jnp.int32),
                        pltpu.SMEM((IDX_RING * EXPERT_TILE,), jnp.int32),
                        tile_buf, tile_buf, tile_buf, tile_buf,
                        pltpu.VMEM((D, F), BF16), pltpu.VMEM((D, F), BF16), pltpu.VMEM((F, D), BF16),
                        pltpu.SemaphoreType.DMA((2,)), pltpu.SemaphoreType.DMA((2,)),
                        pltpu.SemaphoreType.DMA((2,))],
    )
    return pl.pallas_call(
        functools.partial(_expert_kernel, dump0=2 * T),
        out_shape=jax.ShapeDtypeStruct((2 * T + 2 * EXPERT_TILE, D), F32),
        grid_spec=grid_spec,
        compiler_params=_params(("arbitrary",)),
        name="moe_experts",
    )(tile_expert, n_used, slot_src, slot_dst, t, w_gate.reshape(N_EXPERTS, D, F),
      w_up.reshape(N_EXPERTS, D, F), w_down.reshape(N_EXPERTS, F, D))


def _combine_kernel(x1_ref, y0_ref, y1_ref, r_ref, p_ref, wple_ref, wpg_ref, gple_ref, gfin_ref, yp_ref, ys_ref, *,
                    n_first):
    i = pl.program_id(0)
    r = r_ref[...]
    x2 = x1_ref[...] + r[:, 2:3] * y0_ref[...] + r[:, 3:4] * y1_ref[...]
    hp = _rms(x2, gple_ref[...]).astype(BF16)
    gate = jax.nn.sigmoid(jnp.dot(hp, wpg_ref[...], preferred_element_type=F32))
    emb = jnp.dot(p_ref[...].astype(BF16), wple_ref[...], preferred_element_type=F32)
    y = _rms(x2 + emb * gate, gfin_ref[...])

    @pl.when(i < n_first)
    def _():
        yp_ref[...] = y

    @pl.when(i >= n_first)
    def _():
        ys_ref[...] = y


def combine(y_flat, x1, route, p, w_ple, w_ple_gate, g_ple, g_final, Tp, tm):
    T, D = x1.shape
    row = lambda i: (i, 0)
    row1 = lambda i: (i + T // tm, 0)
    fix = lambda i: (0, 0)
    n_first = Tp // tm
    return pl.pallas_call(
        functools.partial(_combine_kernel, n_first=n_first),
        out_shape=(jax.ShapeDtypeStruct((Tp, D), F32), jax.ShapeDtypeStruct((T - Tp, D), F32)),
        grid=(T // tm,),
        in_specs=[pl.BlockSpec((tm, D), row), pl.BlockSpec((tm, D), row), pl.BlockSpec((tm, D), row1),
                  pl.BlockSpec((tm, LANES), row),
                  pl.BlockSpec((tm, p.shape[1]), row),
                  pl.BlockSpec(w_ple.shape, fix), pl.BlockSpec((D, D), fix),
                  pl.BlockSpec((1, D), fix), pl.BlockSpec((1, D), fix)],
        out_specs=tuple(_two_stream_specs(tm, D, n_first)),
        compiler_params=_params(("arbitrary",)),
        name="moe_combine_ple",
    )(x1, y_flat, y_flat, route, p, w_ple, w_ple_gate, g_ple.reshape(1, D), g_final.reshape(1, D))


def _routing_tables(route, counts, n_tiles):
    ids = route[:, 0:2].astype(jnp.int32).reshape(-1)
    rank = route[:, 4:6].astype(jnp.int32).reshape(-1)
    counts = counts.astype(jnp.int32)
    tiles = (counts + EXPERT_TILE - 1) // EXPERT_TILE
    tile_end = jnp.cumsum(tiles)
    tile_start = tile_end - tiles
    onehot = ids[:, None] == jnp.arange(N_EXPERTS, dtype=jnp.int32)[None, :]
    pos = jnp.sum(jnp.where(onehot, tile_start[None, :], 0), axis=1) * EXPERT_TILE + rank
    slot_assign = jnp.full((n_tiles * EXPERT_TILE,), -1, jnp.int32).at[pos].set(
        jnp.arange(ids.shape[0], dtype=jnp.int32), unique_indices=True)
    n_assign = ids.shape[0]
    slot = jnp.arange(n_tiles * EXPERT_TILE, dtype=jnp.int32)
    token = jnp.maximum(slot_assign, 0) // 2
    slot_src = token
    dump = n_assign + ((slot // EXPERT_TILE) % 2) * EXPERT_TILE + slot % EXPERT_TILE
    slot_dst = jnp.where(slot_assign >= 0, (slot_assign % 2) * (n_assign // 2) + token, dump)
    n_used = tile_end[-1]
    tidx = jnp.minimum(jnp.arange(n_tiles, dtype=jnp.int32), n_used - 1)
    tile_expert = jnp.sum((tidx[:, None] >= tile_end[None, :]).astype(jnp.int32), axis=1)
    return slot_src, slot_dst, tile_expert.astype(jnp.int32), n_used.reshape(1).astype(jnp.int32)


def kernel(x_prompt, x_sample, cache_k, cache_v, state_conv, state_C, state_n, state_m, p_prompt, p_sample,
           g_mix, w_in, b_if, conv_w, conv_b, g_head_a, w_proj_a, w_proj_b, w_out, g_ffn, w_route_g,
           b_route_g, w_route_e, b_route_e, w_exp_gate, w_exp_up, w_exp_down, g_ple, w_ple, w_ple_gate,
           g_final):
    assert w_in.shape[0] == 1, "single layer"
    B, S, D = x_prompt.shape
    DB, DS, _ = x_sample.shape
    Tp, Ts = B * S, DB * DS
    T = Tp + Ts
    xp, xs = x_prompt.reshape(Tp, D), x_sample.reshape(Ts, D)
    p = jnp.concatenate([p_prompt[0].reshape(Tp, -1), p_sample[0].reshape(Ts, -1)], axis=0)
    tm = _pick(math.gcd(Tp, Ts), (256, 128, 64, 32, 16, 8))

    wi = w_in[0]
    g0 = 2 * QK_A + D_A
    m0 = g0 + 2 * NH_A
    e0 = m0 + D_A + 3 * D_B
    w_main = jnp.concatenate([wi[:, :g0], wi[:, m0:e0]], axis=1).astype(BF16)
    w_gates = jnp.zeros((D, LANES), F32).at[:, :2 * NH_A].set(wi[:, g0:m0]).astype(BF16)
    wga = wi[:, e0:e0 + D].astype(BF16)
    wgb = wi[:, e0 + D:].astype(BF16)

    h = rms_cast(xp, xs, g_mix[0], tm)
    P = matmul(h, w_main, "proj_main")
    G = matmul(h, w_gates, "proj_gates")

    zeros = functools.partial(jnp.zeros, dtype=F32)
    ha_p, C_p, n_p, m_p, cv_p = mlstm(P, G, 0, B, S, b_if[0], conv_w[0], conv_b[0], g_head_a[0],
                                      zeros((B, CONV_W - 1, 2 * QK_A)), zeros((B, NH_A, DQK_A, DV_A)),
                                      zeros((B, NH_A, DQK_A)), zeros((B, NH_A)))
    ha_s, C_s, n_s, m_s, cv_s = mlstm(P, G, Tp, DB, DS, b_if[0], conv_w[0], conv_b[0], g_head_a[0],
                                      state_conv[0], state_C[0], state_n[0], state_m[0])
    hb_p = sb_attention(P, 0, B, S)
    hb_s = sb_attention_past(P, Tp, DB, DS, cache_k, cache_v)
    ha = jnp.concatenate([ha_p, ha_s], axis=0)
    hb = jnp.concatenate([hb_p, hb_s], axis=0)

    mg = merge(h, ha, hb, wga, wgb, w_proj_a[0].astype(BF16), w_proj_b[0].astype(BF16))

    wr = jnp.zeros((D, LANES), F32)
    wr = wr.at[:, :N_GROUPS].set(w_route_g[0])
    wr = wr.at[:, N_GROUPS:N_GROUPS + N_EXPERTS].set(
        jnp.transpose(w_route_e[0], (1, 0, 2)).reshape(D, N_EXPERTS))
    br = jnp.zeros((1, LANES), F32)
    br = br.at[0, :N_GROUPS].set(b_route_g[0])
    br = br.at[0, N_GROUPS:N_GROUPS + N_EXPERTS].set(b_route_e[0].reshape(-1))
    x1, t, route, counts = wout_route(mg, xp, xs, w_out[0].astype(BF16), g_ffn[0], wr.astype(BF16), br, tm)

    n_tiles = (2 * T + N_EXPERTS * (EXPERT_TILE - 1)) // EXPERT_TILE + 1
    slot_src, slot_dst, tile_expert, n_used = _routing_tables(route, counts[0, :N_EXPERTS], n_tiles)
    y_flat = experts(t, slot_src, slot_dst, tile_expert, n_used, w_exp_gate[0], w_exp_up[0], w_exp_down[0])
    y_p, y_s = combine(y_flat, x1, route, p, w_ple[0].astype(BF16),
                       w_ple_gate[0].astype(BF16), g_ple[0], g_final, Tp, tm)

    kq = P[:, 4 * D_B:5 * D_B]
    vq = P[:, 5 * D_B:6 * D_B]
    return (y_p.reshape(B, S, D), y_s.reshape(DB, DS, D),
            kq[:Tp].reshape(1, B, S, NH_B, DH_B), vq[:Tp].reshape(1, B, S, NH_B, DH_B),
            cv_p[None], C_p[None], n_p[None], m_p[None],
            kq[Tp:].reshape(1, DB, DS, NH_B, DH_B), vq[Tp:].reshape(1, DB, DS, NH_B, DH_B),
            cv_s[None], C_s[None], n_s[None], m_s[None])
```

```python
import functools
import math

import jax
import jax.numpy as jnp
from jax import lax
from jax.experimental import pallas as pl
from jax.experimental.pallas import tpu as pltpu

F32 = jnp.float32
BF16 = jnp.bfloat16
EPS = 1e-6

CHUNK = 64
NH_A, DQK_A, DV_A = 4, 128, 256
QK_A, D_A = NH_A * DQK_A, NH_A * DV_A
CONV_W = 4
NH_B, DH_B = 8, 128
D_B = NH_B * DH_B
N_GROUPS, E_PER_GROUP = 4, 8
N_EXPERTS = N_GROUPS * E_PER_GROUP
LANES = 128
EXPERT_TILE = 256
VMEM_LIMIT = 56 * 1024 * 1024


def _pick(n, cands):
    for c in cands:
        if n % c == 0:
            return c
    raise ValueError(f"no tile for {n} in {cands}")


def _params(sem):
    return pltpu.CompilerParams(dimension_semantics=sem, vmem_limit_bytes=VMEM_LIMIT)


def _rms(x, g):
    return x * lax.rsqrt(jnp.mean(x * x, axis=-1, keepdims=True) + EPS) * g


def _log_sigmoid(x):
    return jnp.minimum(x, 0.0) - jnp.log(1.0 + jnp.exp(-jnp.abs(x)))


def _two_stream_specs(tm, D, n_first):
    return [pl.BlockSpec((tm, D), lambda i: (jnp.minimum(i, n_first - 1), 0)),
            pl.BlockSpec((tm, D), lambda i: (jnp.maximum(i - n_first, 0), 0))]


def _rms_kernel(xp_ref, xs_ref, g_ref, o_ref, *, n_first):
    x = jnp.where(pl.program_id(0) < n_first, xp_ref[...], xs_ref[...])
    o_ref[...] = _rms(x, g_ref[...]).astype(o_ref.dtype)


def rms_cast(xp, xs, g, tm):
    (Tp, D), Ts = xp.shape, xs.shape[0]
    T = Tp + Ts
    return pl.pallas_call(
        functools.partial(_rms_kernel, n_first=Tp // tm),
        out_shape=jax.ShapeDtypeStruct((T, D), BF16),
        grid=(T // tm,),
        in_specs=_two_stream_specs(tm, D, Tp // tm) + [pl.BlockSpec((1, D), lambda i: (0, 0))],
        out_specs=pl.BlockSpec((tm, D), lambda i: (i, 0)),
        compiler_params=_params(("arbitrary",)),
        name="rms_cast",
    )(xp, xs, g.reshape(1, D))


def _mm_kernel(a_ref, w_ref, o_ref):
    o_ref[...] = jnp.dot(a_ref[...], w_ref[...], preferred_element_type=F32)


def matmul(a, w, name):
    T, K = a.shape
    N = w.shape[1]
    tm = _pick(T, (1280, 640, 512, 256, 128, 64, 32, 16))
    tn = _pick(N, (512, 256, 128))
    return pl.pallas_call(
        _mm_kernel,
        out_shape=jax.ShapeDtypeStruct((T, N), F32),
        grid=(T // tm, N // tn),
        in_specs=[pl.BlockSpec((tm, K), lambda i, j: (i, 0)),
                  pl.BlockSpec((K, tn), lambda i, j: (0, j))],
        out_specs=pl.BlockSpec((tm, tn), lambda i, j: (i, j)),
        compiler_params=_params(("parallel", "arbitrary")),
        name=name,
    )(a, w)


def _mlstm_kernel(*refs, c, nc, nb):
    qk_refs, v_refs, o_refs, g_refs = (refs[k * nb:(k + 1) * nb] for k in range(4))
    (bif_ref, cw_ref, cb_ref, gh_ref, cbuf_ref, c0_ref, n0_ref, m0_ref,
     h_ref, cout_ref, nout_ref, mout_ref, convout_ref, xbuf, c_s, n_s, m_s) = refs[4 * nb:]
    j = pl.program_id(1)

    @pl.when(j == 0)
    def _():
        c_s[...] = c0_ref[...]
        n_s[...] = n0_ref[...]
        m_s[...] = m0_ref[...]
        xbuf[:, 5:8, :] = cbuf_ref[...]

    row = lax.broadcasted_iota(jnp.int32, (c, c), 0)
    col = lax.broadcasted_iota(jnp.int32, (c, c), 1)
    consts = dict(lane=lax.broadcasted_iota(jnp.int32, (c, LANES), 1), tril=(col <= row).astype(F32),
                  eye=(col == row).astype(F32), triu=(row <= col).astype(F32), causal=col <= row)
    last_rows = [
        _mlstm_chunk(qk_refs[b], v_refs[b], o_refs[b], g_refs[b], bif_ref, cw_ref, cb_ref, gh_ref,
                     h_ref.at[b], xbuf.at[b], c_s.at[b], n_s.at[b], m_s.at[b], consts, c)
        for b in range(nb)]

    @pl.when(j == nc - 1)
    def _():
        cout_ref[...] = c_s[...]
        nout_ref[...] = n_s[...]
        mout_ref[...] = m_s[...]
        for b in range(nb):
            convout_ref[b] = last_rows[b]


def _mlstm_chunk(qk_ref, v_ref, o_ref, g_ref, bif_ref, cw_ref, cb_ref, gh_ref, h_ref, xbuf, c_s, n_s, m_s,
                 consts, c):
    lane, tril, eye, triu, causal = (consts[k] for k in ("lane", "tril", "eye", "triu", "causal"))
    xbuf[8:8 + c, :] = qk_ref[...]
    y = cb_ref[...] + cw_ref[0:1, :] * xbuf[5:5 + c, :]
    for t in range(1, CONV_W):
        y = y + cw_ref[t:t + 1, :] * xbuf[5 + t:5 + t + c, :]
    last_rows = xbuf[c + 5:c + 8, :]
    xbuf[5:8, :] = last_rows
    qk = y * jax.nn.sigmoid(y)

    graw = g_ref[...] + bif_ref[...]
    xg = jnp.where(lane < NH_A, graw, jnp.where(lane < 2 * NH_A, _log_sigmoid(graw), 0.0))
    hp = lax.Precision.HIGHEST
    bc = jnp.dot(tril, xg, precision=hp, preferred_element_type=F32)
    dn0 = (((0,), (0,)), ((), ()))
    xt = lax.dot_general(xg, eye, dn0, precision=hp, preferred_element_type=F32)
    bt = lax.dot_general(xg, triu, dn0, precision=hp, preferred_element_type=F32)

    for h in range(NH_A):
        q = qk[:, h * DQK_A:(h + 1) * DQK_A]
        k = qk[:, QK_A + h * DQK_A:QK_A + (h + 1) * DQK_A] * (DQK_A ** -0.5)
        v = v_ref[:, h * DV_A:(h + 1) * DV_A]
        qb, kb, vb = q.astype(BF16), k.astype(BF16), v.astype(BF16)
        i_col = xg[:, h:h + 1]
        b_col = bc[:, NH_A + h:NH_A + h + 1]
        i_row = xt[h:h + 1, :]
        b_row = bt[NH_A + h:NH_A + h + 1, :]
        m_prev = m_s[0:1, h:h + 1]
        C = c_s[h]
        nvec = n_s[h:h + 1, :]

        dmat = jnp.where(causal, b_col - b_row + i_row, -jnp.inf)
        inter = b_col + m_prev
        m_t = jnp.maximum(inter, jnp.max(dmat, axis=1, keepdims=True))
        e = jnp.exp(dmat - m_t)
        s = lax.dot_general(qb, kb, (((1,), (1,)), ((), ())), preferred_element_type=F32) * e
        w_inter = jnp.exp(inter - m_t)
        num = (jnp.dot(s.astype(BF16), vb, preferred_element_type=F32)
               + w_inter * jnp.dot(qb, C.astype(BF16), preferred_element_type=F32))
        den = jnp.sum(s, axis=1, keepdims=True) + w_inter * jnp.sum(q * nvec, axis=1, keepdims=True)
        hh = num / jnp.maximum(jnp.abs(den), jnp.exp(-m_t))
        hh = hh * lax.rsqrt(jnp.mean(hh * hh, axis=1, keepdims=True) + EPS)
        hh = hh * gh_ref[:, h * DV_A:(h + 1) * DV_A]
        hh = jax.nn.sigmoid(o_ref[:, h * DV_A:(h + 1) * DV_A]) * hh
        h_ref[:, h * DV_A:(h + 1) * DV_A] = hh.astype(h_ref.dtype)

        b_end = b_col[c - 1:c, :]
        dec = b_end - b_col + i_col
        m_new = jnp.maximum(b_end + m_prev, jnp.max(dec, axis=0, keepdims=True))
        wk = jnp.exp(dec - m_new)
        w_old = jnp.exp(b_end + m_prev - m_new)
        kw = k * wk
        c_s[h] = w_old * C + lax.dot_general(kw.astype(BF16), vb, dn0, preferred_element_type=F32)
        n_s[h:h + 1, :] = w_old * nvec + jnp.sum(kw, axis=0, keepdims=True)
        m_s[0:1, h:h + 1] = m_new
    return last_rows


MLSTM_SEQS = 2


def mlstm(P, G, row0, N, L, b_if, conv_w, conv_b, g_head, conv_buf, C0, n0, m0):
    c = min(L, CHUNK)
    nc = L // c
    nb = MLSTM_SEQS if N % MLSTM_SEQS == 0 else 1
    base = row0 // c
    m0p = jnp.zeros((N, 1, LANES), F32).at[:, 0, :NH_A].set(m0)
    bif = jnp.zeros((1, LANES), F32).at[0, :2 * NH_A].set(b_if)

    def rows(width, col):
        return [pl.BlockSpec((c, width), lambda n, j, b=b: (base + (n * nb + b) * nc + j, col))
                for b in range(nb)]
    fix = lambda n, j: (0, 0)
    st4 = lambda n, j: (n, 0, 0, 0)
    st3 = lambda n, j: (n, 0, 0)
    outs = pl.pallas_call(
        functools.partial(_mlstm_kernel, c=c, nc=nc, nb=nb),
        out_shape=(jax.ShapeDtypeStruct((N, L, D_A), BF16),
                   jax.ShapeDtypeStruct((N, NH_A, DQK_A, DV_A), F32),
                   jax.ShapeDtypeStruct((N, NH_A, DQK_A), F32),
                   jax.ShapeDtypeStruct((N, 1, LANES), F32),
                   jax.ShapeDtypeStruct((N, CONV_W - 1, 2 * QK_A), F32)),
        grid=(N // nb, nc),
        in_specs=rows(2 * QK_A, 0) + rows(D_A, 1) + rows(D_A, 2) + rows(LANES, 0)
                 + [pl.BlockSpec((1, LANES), fix),
                    pl.BlockSpec((CONV_W, 2 * QK_A), fix),
                    pl.BlockSpec((1, 2 * QK_A), fix),
                    pl.BlockSpec((1, D_A), fix),
                    pl.BlockSpec((nb, CONV_W - 1, 2 * QK_A), st3),
                    pl.BlockSpec((nb, NH_A, DQK_A, DV_A), st4),
                    pl.BlockSpec((nb, NH_A, DQK_A), st3),
                    pl.BlockSpec((nb, 1, LANES), st3)],
        out_specs=(pl.BlockSpec((nb, c, D_A), lambda n, j: (n, j, 0)),
                   pl.BlockSpec((nb, NH_A, DQK_A, DV_A), st4),
                   pl.BlockSpec((nb, NH_A, DQK_A), st3),
                   pl.BlockSpec((nb, 1, LANES), st3),
                   pl.BlockSpec((nb, CONV_W - 1, 2 * QK_A), st3)),
        scratch_shapes=[pltpu.VMEM((nb, c + 8, 2 * QK_A), F32),
                        pltpu.VMEM((nb, NH_A, DQK_A, DV_A), F32),
                        pltpu.VMEM((nb, NH_A, DQK_A), F32),
                        pltpu.VMEM((nb, 1, LANES), F32)],
        compiler_params=_params(("parallel", "arbitrary")),
        name="mlstm",
    )(*([P] * (3 * nb) + [G] * nb), bif, conv_w, conv_b.reshape(1, -1), g_head.reshape(1, -1),
      conv_buf, C0, n0, m0p)
    h, C1, n1, m1, cv = outs
    return h.reshape(N * L, D_A), C1, n1, m1[:, 0, :NH_A], cv


STICK_UNDERFLOW = -110.0


def _later_matrix(tk):
    jr = lax.broadcasted_iota(jnp.int32, (tk, tk), 0)
    sc = lax.broadcasted_iota(jnp.int32, (tk, tk), 1)
    return (jr > sc).astype(BF16)


def _sb_block(q_bf, k_blk, v_blk, R, later, mask):
    z = lax.dot_general(q_bf, k_blk.astype(BF16), (((1,), (1,)), ((), ())),
                        preferred_element_type=F32) * (DH_B ** -0.5)
    l1mb = -(jnp.maximum(z, 0.0) + jnp.log(1.0 + jnp.exp(-jnp.abs(z))))
    if mask is not None:
        l1mb = jnp.where(mask, l1mb, 0.0)
    hi = l1mb.astype(BF16)
    lo = (l1mb - hi.astype(F32)).astype(BF16)
    rest = (jnp.dot(hi, later, preferred_element_type=F32)
            + jnp.dot(lo, later, preferred_element_type=F32)) + R
    a = jnp.exp(z + l1mb + rest)
    if mask is not None:
        a = jnp.where(mask, a, 0.0)
    pv = jnp.dot(a.astype(BF16), v_blk.astype(BF16), preferred_element_type=F32)
    return pv, R + jnp.sum(l1mb, axis=1, keepdims=True)


SB_HEADS = 2


def _sb_kernel(q_ref, k_ref, v_ref, o_ref, *, tq):
    qi = pl.program_id(2)
    rr = lax.broadcasted_iota(jnp.int32, (tq, tq), 0)
    cc = lax.broadcasted_iota(jnp.int32, (tq, tq), 1)
    later = _later_matrix(tq)
    cols = [slice(h * DH_B, (h + 1) * DH_B) for h in range(SB_HEADS)]
    q_bf = [q_ref[:, c].astype(BF16) for c in cols]
    start = pl.multiple_of(qi * tq, tq)
    accs, Rs = [], []
    for h, c in enumerate(cols):
        acc, R = _sb_block(q_bf[h], k_ref[pl.ds(start, tq), c], v_ref[pl.ds(start, tq), c],
                           jnp.zeros((tq, 1), F32), later, cc < rr)
        accs.append(acc)
        Rs.append(R)

    def rmax(Rs):
        return jnp.max(jnp.concatenate(Rs, axis=1))

    def cond(c):
        return jnp.logical_and(c[0] < qi, c[3] > STICK_UNDERFLOW)

    def body(carry):
        t, accs, Rs, _ = carry
        st = pl.multiple_of((qi - 1 - t) * tq, tq)
        new_accs, new_Rs = [], []
        for h, c in enumerate(cols):
            pv, R = _sb_block(q_bf[h], k_ref[pl.ds(st, tq), c], v_ref[pl.ds(st, tq), c], Rs[h], later, None)
            new_accs.append(accs[h] + pv)
            new_Rs.append(R)
        return t + 1, tuple(new_accs), tuple(new_Rs), rmax(new_Rs)

    _, accs, _, _ = lax.while_loop(cond, body, (jnp.int32(0), tuple(accs), tuple(Rs), rmax(Rs)))
    for h, c in enumerate(cols):
        o_ref[:, c] = accs[h].astype(o_ref.dtype)


QB_BLK, KB_BLK, VB_BLK = 3 * D_B // DH_B, 4 * D_B // DH_B, 5 * D_B // DH_B


def sb_attention(P, row0, N, L):
    tq = min(L, 256)
    nq = L // tq
    W = SB_HEADS * DH_B
    qb, kb, vb = QB_BLK // SB_HEADS, KB_BLK // SB_HEADS, VB_BLK // SB_HEADS
    return pl.pallas_call(
        functools.partial(_sb_kernel, tq=tq),
        out_shape=jax.ShapeDtypeStruct((N * L, D_B), BF16),
        grid=(N, NH_B // SB_HEADS, nq),
        in_specs=[pl.BlockSpec((tq, W), lambda n, h, i: (row0 // tq + n * nq + i, qb + h)),
                  pl.BlockSpec((L, W), lambda n, h, i: (row0 // L + n, kb + h)),
                  pl.BlockSpec((L, W), lambda n, h, i: (row0 // L + n, vb + h))],
        out_specs=pl.BlockSpec((tq, W), lambda n, h, i: (n * nq + i, h)),
        compiler_params=_params(("parallel", "parallel", "arbitrary")),
        name="sb_attention",
    )(P, P, P)


def _sb_past_kernel(q_ref, k_ref, v_ref, pk_hbm, pv_hbm, o_ref, kbuf, vbuf, ksem, vsem, *, L, tk, n_blk):
    n = pl.program_id(0)

    def copies(blk):
        st = blk * tk
        return ([pltpu.make_async_copy(pk_hbm.at[0, n, pl.ds(st, tk), h, :], kbuf.at[h], ksem)
                 for h in range(NH_B)]
                + [pltpu.make_async_copy(pv_hbm.at[0, n, pl.ds(st, tk), h, :], vbuf.at[h], vsem)
                   for h in range(NH_B)])

    rr = lax.broadcasted_iota(jnp.int32, (L, L), 0)
    cc = lax.broadcasted_iota(jnp.int32, (L, L), 1)
    later_new = _later_matrix(L)
    later = _later_matrix(tk)
    q_bf = [q_ref[:, h * DH_B:(h + 1) * DH_B].astype(BF16) for h in range(NH_B)]
    accs, Rs = [], []
    for h in range(NH_B):
        sl = slice(h * DH_B, (h + 1) * DH_B)
        acc, R = _sb_block(q_bf[h], k_ref[:, sl], v_ref[:, sl], jnp.zeros((L, 1), F32), later_new, cc < rr)
        accs.append(acc)
        Rs.append(R)

    def rmax(Rs):
        return jnp.max(jnp.concatenate(Rs, axis=1))

    def cond(c):
        return jnp.logical_and(c[0] < n_blk, c[3] > STICK_UNDERFLOW)

    def body(c):
        t, accs, Rs, _ = c
        cps = copies(n_blk - 1 - t)
        for cp in cps:
            cp.start()
        for cp in cps:
            cp.wait()
        new_accs, new_Rs = [], []
        for h in range(NH_B):
            pv, R = _sb_block(q_bf[h], kbuf[h], vbuf[h], Rs[h], later, None)
            new_accs.append(accs[h] + pv)
            new_Rs.append(R)
        return t + 1, tuple(new_accs), tuple(new_Rs), rmax(new_Rs)

    _, accs, _, _ = lax.while_loop(cond, body, (jnp.int32(0), tuple(accs), tuple(Rs), rmax(Rs)))
    for h in range(NH_B):
        o_ref[:, h * DH_B:(h + 1) * DH_B] = accs[h].astype(o_ref.dtype)


def sb_attention_past(P, row0, N, L, past_k, past_v):
    Lp = past_k.shape[2]
    tk = min(Lp, 256)
    blk = lambda c: pl.BlockSpec((L, D_B), lambda n: (row0 // L + n, c))
    return pl.pallas_call(
        functools.partial(_sb_past_kernel, L=L, tk=tk, n_blk=Lp // tk),
        out_shape=jax.ShapeDtypeStruct((N * L, D_B), BF16),
        grid=(N,),
        in_specs=[blk(3), blk(4), blk(5),
                  pl.BlockSpec(memory_space=pl.ANY), pl.BlockSpec(memory_space=pl.ANY)],
        out_specs=pl.BlockSpec((L, D_B), lambda n: (n, 0)),
        scratch_shapes=[pltpu.VMEM((NH_B, tk, DH_B), F32), pltpu.VMEM((NH_B, tk, DH_B), F32),
                        pltpu.SemaphoreType.DMA(()), pltpu.SemaphoreType.DMA(())],
        compiler_params=_params(("arbitrary",)),
        name="sb_attention_past",
    )(P, P, P, past_k, past_v)


def _merge_kernel(h_ref, hap_ref, has_ref, hbp_ref, hbs_ref, wga_ref, wgb_ref, wpa_ref, wpb_ref, o_ref, *, n_first):
    first = pl.program_id(1) < n_first
    h = h_ref[...]
    ha = jnp.where(first, hap_ref[...], has_ref[...])
    hb = jnp.where(first, hbp_ref[...], hbs_ref[...])
    ga = jnp.dot(h, wga_ref[...], preferred_element_type=F32)
    gb = jnp.dot(h, wgb_ref[...], preferred_element_type=F32)
    pa = jnp.dot(ha, wpa_ref[...], preferred_element_type=F32)
    pb = jnp.dot(hb, wpb_ref[...], preferred_element_type=F32)
    o_ref[...] = (jax.nn.sigmoid(ga) * pa + jax.nn.sigmoid(gb) * pb).astype(o_ref.dtype)


def merge(h, ha_p, ha_s, hb_p, hb_s, wga, wgb, wpa, wpb, tm):
    T, D = h.shape
    n_first = ha_p.shape[0] // tm
    tn = _pick(D, (512, 256, 128))
    colw = lambda j, i: (0, j)
    first = lambda j, i: (jnp.minimum(i, n_first - 1), 0)
    second = lambda j, i: (jnp.maximum(i - n_first, 0), 0)
    return pl.pallas_call(
        functools.partial(_merge_kernel, n_first=n_first),
        out_shape=jax.ShapeDtypeStruct((T, D), BF16),
        grid=(D // tn, T // tm),
        in_specs=[pl.BlockSpec((tm, D), lambda j, i: (i, 0)),
                  pl.BlockSpec((tm, D_A), first), pl.BlockSpec((tm, D_A), second),
                  pl.BlockSpec((tm, D_B), first), pl.BlockSpec((tm, D_B), second),
                  pl.BlockSpec((D, tn), colw), pl.BlockSpec((D, tn), colw),
                  pl.BlockSpec((D_A, tn), colw), pl.BlockSpec((D_B, tn), colw)],
        out_specs=pl.BlockSpec((tm, tn), lambda j, i: (i, j)),
        compiler_params=_params(("arbitrary", "arbitrary")),
        name="merge",
    )(h, ha_p, ha_s, hb_p, hb_s, wga, wgb, wpa, wpb)


def _wout_route_kernel(mg_ref, xp_ref, xs_ref, wo_ref, g_ref, wr_ref, br_ref, x1_ref, t_ref, r_ref, cnt_ref,
                       run_s, *, n_first):
    i = pl.program_id(0)

    @pl.when(i == 0)
    def _():
        run_s[...] = jnp.zeros_like(run_s)

    x = jnp.where(i < n_first, xp_ref[...], xs_ref[...])
    x1 = x + jnp.dot(mg_ref[...], wo_ref[...], preferred_element_type=F32)
    x1_ref[...] = x1
    t = _rms(x1, g_ref[...])
    t_ref[...] = t
    lg = jnp.dot(t.astype(BF16), wr_ref[...], preferred_element_type=F32) + br_ref[...]
    tm = lg.shape[0]
    lane = lax.broadcasted_iota(jnp.int32, (tm, LANES), 1)
    ninf = -jnp.inf
    glm = jnp.where(lane < N_GROUPS, lg, ninf)
    gmax = jnp.max(glm, axis=1, keepdims=True)
    gsel = jnp.min(jnp.where(glm == gmax, lane, LANES), axis=1, keepdims=True)
    p_sel = 1.0 / jnp.sum(jnp.exp(glm - gmax), axis=1, keepdims=True)
    lo = N_GROUPS + E_PER_GROUP * gsel
    elm = jnp.where((lane >= lo) & (lane < lo + E_PER_GROUP), lg, ninf)
    m1 = jnp.max(elm, axis=1, keepdims=True)
    i1 = jnp.min(jnp.where(elm == m1, lane, LANES), axis=1, keepdims=True)
    elm2 = jnp.where(lane == i1, ninf, elm)
    m2 = jnp.max(elm2, axis=1, keepdims=True)
    i2 = jnp.min(jnp.where(elm2 == m2, lane, LANES), axis=1, keepdims=True)
    e2 = jnp.exp(m2 - m1)
    w1 = p_sel / (1.0 + e2)
    w2 = p_sel * e2 / (1.0 + e2)
    e1 = (lane == i1 - N_GROUPS).astype(F32)
    e2h = (lane == i2 - N_GROUPS).astype(F32)
    both = e1 + e2h
    rr = lax.broadcasted_iota(jnp.int32, (tm, tm), 0)
    cc = lax.broadcasted_iota(jnp.int32, (tm, tm), 1)
    before = jnp.dot((cc < rr).astype(BF16), both.astype(BF16), preferred_element_type=F32) + run_s[...]
    rank1 = jnp.sum(e1 * before, axis=1, keepdims=True)
    rank2 = jnp.sum(e2h * before, axis=1, keepdims=True)
    run_s[...] = run_s[...] + jnp.sum(both, axis=0, keepdims=True)
    cnt_ref[...] = run_s[...]
    r = jnp.where(lane == 0, (i1 - N_GROUPS).astype(F32),
        jnp.where(lane == 1, (i2 - N_GROUPS).astype(F32),
        jnp.where(lane == 2, w1, jnp.where(lane == 3, w2,
        jnp.where(lane == 4, rank1, jnp.where(lane == 5, rank2, 0.0))))))
    r_ref[...] = r


def wout_route(mg, xp, xs, wo, g_ffn, wr, br, tm):
    (Tp, D), Ts = xp.shape, xs.shape[0]
    T = Tp + Ts
    row = lambda i: (i, 0)
    fix = lambda i: (0, 0)
    return pl.pallas_call(
        functools.partial(_wout_route_kernel, n_first=Tp // tm),
        out_shape=(jax.ShapeDtypeStruct((T, D), F32), jax.ShapeDtypeStruct((T, D), F32),
                   jax.ShapeDtypeStruct((T, LANES), F32), jax.ShapeDtypeStruct((1, LANES), F32)),
        grid=(T // tm,),
        in_specs=[pl.BlockSpec((tm, D), row)] + _two_stream_specs(tm, D, Tp // tm)
                 + [pl.BlockSpec((D, D), fix), pl.BlockSpec((1, D), fix), pl.BlockSpec((D, LANES), fix),
                    pl.BlockSpec((1, LANES), fix)],
        out_specs=(pl.BlockSpec((tm, D), row), pl.BlockSpec((tm, D), row), pl.BlockSpec((tm, LANES), row),
                   pl.BlockSpec((1, LANES), fix)),
        scratch_shapes=[pltpu.VMEM((1, LANES), F32)],
        compiler_params=_params(("arbitrary",)),
        name="wout_route",
    )(mg, xp, xs, wo, g_ffn.reshape(1, D), wr, br)


IDX_RING = 8


def _expert_kernel(te_ref, nu_ref, src_hbm, dst_hbm, t_hbm, wg_ref, wu_ref, wd_ref, y_hbm,
                   sidx, didx, xb0, xb1, yb0, yb1, wg_s, wu_s, wd_s, gsem, ssem, isem, *, dump0):
    i = pl.program_id(0)
    nu = nu_ref[0]
    TE = EXPERT_TILE
    xb, yb = (xb0, xb1), (yb0, yb1)

    def table_copies(tile, seq):
        base = (seq & (IDX_RING - 1)) * TE
        return (pltpu.make_async_copy(src_hbm.at[pl.ds(tile * TE, TE)], sidx.at[pl.ds(base, TE)], isem.at[0]),
                pltpu.make_async_copy(dst_hbm.at[pl.ds(tile * TE, TE)], didx.at[pl.ds(base, TE)], isem.at[1]))

    def gather(seq, par):
        base = (seq & (IDX_RING - 1)) * TE
        return [pltpu.make_async_copy(t_hbm.at[pl.ds(sidx[base + r], 1), :], xb[par].at[pl.ds(r, 1), :],
                                      gsem.at[par]) for r in range(TE)]

    def scatter(seq, par):
        base = (seq & (IDX_RING - 1)) * TE
        return [pltpu.make_async_copy(yb[par].at[pl.ds(r, 1), :], y_hbm.at[pl.ds(didx[base + r], 1), :],
                                      ssem.at[par]) for r in range(TE)]

    def start(cps):
        rows = len(cps) == TE
        for n, cp in enumerate(cps):
            cp.start(priority=n % 2 if rows else 0)

    def wait(cps):
        for cp in cps:
            cp.wait()

    last = nu - 1

    @pl.when(i == 0)
    def _():
        first = table_copies(0, 0)
        start(first)
        wait(first)
        start(table_copies(jnp.minimum(1, last), 1))
        start(gather(0, 0))
        yb1[...] = jnp.zeros_like(yb1)
        for r in range(TE):
            didx[(IDX_RING - 1) * TE + r] = dump0 + TE + r
        fill = pltpu.make_async_copy(yb1, y_hbm.at[pl.ds(dump0, TE), :], ssem.at[0])
        fill.start()
        fill.wait()

    @pl.when((i < nu) & ((i == 0) | (te_ref[i] != te_ref[jnp.maximum(i - 1, 0)])))
    def _():
        wg_s[...] = wg_ref[...].astype(BF16)
        wu_s[...] = wu_ref[...].astype(BF16)
        wd_s[...] = wd_ref[...].astype(BF16)

    def step(par):
        wait(table_copies(jnp.minimum(i + 1, last), i + 1))
        start(table_copies(jnp.minimum(i + 2, last), i + 2))

        @pl.when(i >= 1)
        def _():
            wait(scatter(i - 2, par))

        wait(gather(i, par))
        start(scatter(i - 1, 1 - par))
        start(gather(i + 1, 1 - par))
        x = xb[par][...].astype(BF16)
        a = jnp.dot(x, wg_s[...], preferred_element_type=F32)
        u = jnp.dot(x, wu_s[...], preferred_element_type=F32)
        hid = (a * jax.nn.sigmoid(a) * u).astype(BF16)
        yb[par][...] = jnp.dot(hid, wd_s[...], preferred_element_type=F32)

        @pl.when(i == last)
        def _():
            start(scatter(i, par))
            wait(scatter(i - 1, 1 - par))
            wait(scatter(i, par))
            wait(gather(i + 1, 1 - par))
            wait(table_copies(last, i + 2))

    for par in (0, 1):
        pl.when((i < nu) & (i % 2 == par))(functools.partial(step, par))


def experts(t, slot_src, slot_dst, tile_expert, n_used, w_gate, w_up, w_down):
    T, D = t.shape
    F = w_gate.shape[-1]
    nt = slot_src.shape[0] // EXPERT_TILE
    wmap = lambda i, te, nu: (te[i], 0, 0)
    any_spec = pl.BlockSpec(memory_space=pl.ANY)
    tile_buf = pltpu.VMEM((EXPERT_TILE, D), F32)
    grid_spec = pltpu.PrefetchScalarGridSpec(
        num_scalar_prefetch=2,
        grid=(nt,),
        in_specs=[any_spec, any_spec, any_spec,
                  pl.BlockSpec((None, D, F), wmap), pl.BlockSpec((None, D, F), wmap),
                  pl.BlockSpec((None, F, D), wmap)],
        out_specs=any_spec,
        scratch_shapes=[pltpu.SMEM((IDX_RING * EXPERT_TILE,), jnp.int32),
                        pltpu.SMEM((IDX_RING * EXPERT_TILE,), jnp.int32),
                        tile_buf, tile_buf, tile_buf, tile_buf,
                        pltpu.VMEM((D, F), BF16), pltpu.VMEM((D, F), BF16), pltpu.VMEM((F, D), BF16),
                        pltpu.SemaphoreType.DMA((2,)), pltpu.SemaphoreType.DMA((2,)),
                        pltpu.SemaphoreType.DMA((2,))],
    )
    return pl.pallas_call(
        functools.partial(_expert_kernel, dump0=2 * T),
        out_shape=jax.ShapeDtypeStruct((2 * T + 2 * EXPERT_TILE, D), F32),
        grid_spec=grid_spec,
        compiler_params=_params(("arbitrary",)),
        name="moe_experts",
    )(tile_expert, n_used, slot_src, slot_dst, t, w_gate.reshape(N_EXPERTS, D, F),
      w_up.reshape(N_EXPERTS, D, F), w_down.reshape(N_EXPERTS, F, D))


def _combine_kernel(x1_ref, y0_ref, y1_ref, r_ref, pp_ref, ps_ref, wple_ref, wpg_ref, gple_ref, gfin_ref,
                    yp_ref, ys_ref, *, n_first):
    i = pl.program_id(0)
    p = jnp.where(i < n_first, pp_ref[...], ps_ref[...])
    r = r_ref[...]
    x2 = x1_ref[...] + r[:, 2:3] * y0_ref[...] + r[:, 3:4] * y1_ref[...]
    hp = _rms(x2, gple_ref[...]).astype(BF16)
    gate = jax.nn.sigmoid(jnp.dot(hp, wpg_ref[...], preferred_element_type=F32))
    emb = jnp.dot(p.astype(BF16), wple_ref[...], preferred_element_type=F32)
    y = _rms(x2 + emb * gate, gfin_ref[...])

    @pl.when(i < n_first)
    def _():
        yp_ref[...] = y

    @pl.when(i >= n_first)
    def _():
        ys_ref[...] = y


def combine(y_flat, x1, route, p_p, p_s, w_ple, w_ple_gate, g_ple, g_final, tm):
    T, D = x1.shape
    row = lambda i: (i, 0)
    row1 = lambda i: (i + T // tm, 0)
    fix = lambda i: (0, 0)
    Tp = p_p.shape[0]
    n_first = Tp // tm
    return pl.pallas_call(
        functools.partial(_combine_kernel, n_first=n_first),
        out_shape=(jax.ShapeDtypeStruct((Tp, D), F32), jax.ShapeDtypeStruct((T - Tp, D), F32)),
        grid=(T // tm,),
        in_specs=[pl.BlockSpec((tm, D), row), pl.BlockSpec((tm, D), row), pl.BlockSpec((tm, D), row1),
                  pl.BlockSpec((tm, LANES), row)] + _two_stream_specs(tm, p_p.shape[1], n_first)
                 + [pl.BlockSpec(w_ple.shape, fix), pl.BlockSpec((D, D), fix),
                  pl.BlockSpec((1, D), fix), pl.BlockSpec((1, D), fix)],
        out_specs=tuple(_two_stream_specs(tm, D, n_first)),
        compiler_params=_params(("arbitrary",)),
        name="moe_combine_ple",
    )(x1, y_flat, y_flat, route, p_p, p_s, w_ple, w_ple_gate, g_ple.reshape(1, D), g_final.reshape(1, D))


def _routing_tables(route, counts, n_tiles):
    ids = route[:, 0:2].astype(jnp.int32).reshape(-1)
    rank = route[:, 4:6].astype(jnp.int32).reshape(-1)
    counts = counts.astype(jnp.int32)
    tiles = (counts + EXPERT_TILE - 1) // EXPERT_TILE
    tile_end = jnp.cumsum(tiles)
    tile_start = tile_end - tiles
    onehot = ids[:, None] == jnp.arange(N_EXPERTS, dtype=jnp.int32)[None, :]
    pos = jnp.sum(jnp.where(onehot, tile_start[None, :], 0), axis=1) * EXPERT_TILE + rank
    slot_assign = jnp.full((n_tiles * EXPERT_TILE,), -1, jnp.int32).at[pos].set(
        jnp.arange(ids.shape[0], dtype=jnp.int32), unique_indices=True)
    n_assign = ids.shape[0]
    slot = jnp.arange(n_tiles * EXPERT_TILE, dtype=jnp.int32)
    token = jnp.maximum(slot_assign, 0) // 2
    slot_src = token
    dump = n_assign + ((slot // EXPERT_TILE) % 2) * EXPERT_TILE + slot % EXPERT_TILE
    slot_dst = jnp.where(slot_assign >= 0, (slot_assign % 2) * (n_assign // 2) + token, dump)
    n_used = tile_end[-1]
    tidx = jnp.minimum(jnp.arange(n_tiles, dtype=jnp.int32), n_used - 1)
    tile_expert = jnp.sum((tidx[:, None] >= tile_end[None, :]).astype(jnp.int32), axis=1)
    return slot_src, slot_dst, tile_expert.astype(jnp.int32), n_used.reshape(1).astype(jnp.int32)


def kernel(x_prompt, x_sample, cache_k, cache_v, state_conv, state_C, state_n, state_m, p_prompt, p_sample,
           g_mix, w_in, b_if, conv_w, conv_b, g_head_a, w_proj_a, w_proj_b, w_out, g_ffn, w_route_g,
           b_route_g, w_route_e, b_route_e, w_exp_gate, w_exp_up, w_exp_down, g_ple, w_ple, w_ple_gate,
           g_final):
    assert w_in.shape[0] == 1, "single layer"
    B, S, D = x_prompt.shape
    DB, DS, _ = x_sample.shape
    Tp, Ts = B * S, DB * DS
    T = Tp + Ts
    xp, xs = x_prompt.reshape(Tp, D), x_sample.reshape(Ts, D)
    tm = _pick(math.gcd(Tp, Ts), (256, 128, 64, 32, 16, 8))

    wi = w_in[0]
    g0 = 2 * QK_A + D_A
    m0 = g0 + 2 * NH_A
    e0 = m0 + D_A + 3 * D_B
    w_main = jnp.concatenate([wi[:, :g0], wi[:, m0:e0]], axis=1).astype(BF16)
    w_gates = jnp.zeros((D, LANES), F32).at[:, :2 * NH_A].set(wi[:, g0:m0]).astype(BF16)
    wga = wi[:, e0:e0 + D].astype(BF16)
    wgb = wi[:, e0 + D:].astype(BF16)

    h = rms_cast(xp, xs, g_mix[0], tm)
    P = matmul(h, w_main, "proj_main")
    G = matmul(h, w_gates, "proj_gates")

    zeros = functools.partial(jnp.zeros, dtype=F32)
    ha_p, C_p, n_p, m_p, cv_p = mlstm(P, G, 0, B, S, b_if[0], conv_w[0], conv_b[0], g_head_a[0],
                                      zeros((B, CONV_W - 1, 2 * QK_A)), zeros((B, NH_A, DQK_A, DV_A)),
                                      zeros((B, NH_A, DQK_A)), zeros((B, NH_A)))
    ha_s, C_s, n_s, m_s, cv_s = mlstm(P, G, Tp, DB, DS, b_if[0], conv_w[0], conv_b[0], g_head_a[0],
                                      state_conv[0], state_C[0], state_n[0], state_m[0])
    hb_p = sb_attention(P, 0, B, S)
    hb_s = sb_attention_past(P, Tp, DB, DS, cache_k, cache_v)
    mg = merge(h, ha_p, ha_s, hb_p, hb_s, wga, wgb, w_proj_a[0].astype(BF16), w_proj_b[0].astype(BF16), tm)

    wr = jnp.zeros((D, LANES), F32)
    wr = wr.at[:, :N_GROUPS].set(w_route_g[0])
    wr = wr.at[:, N_GROUPS:N_GROUPS + N_EXPERTS].set(
        jnp.transpose(w_route_e[0], (1, 0, 2)).reshape(D, N_EXPERTS))
    br = jnp.zeros((1, LANES), F32)
    br = br.at[0, :N_GROUPS].set(b_route_g[0])
    br = br.at[0, N_GROUPS:N_GROUPS + N_EXPERTS].set(b_route_e[0].reshape(-1))
    x1, t, route, counts = wout_route(mg, xp, xs, w_out[0].astype(BF16), g_ffn[0], wr.astype(BF16), br, tm)

    n_tiles = (2 * T + N_EXPERTS * (EXPERT_TILE - 1)) // EXPERT_TILE + 1
    slot_src, slot_dst, tile_expert, n_used = _routing_tables(route, counts[0, :N_EXPERTS], n_tiles)
    y_flat = experts(t, slot_src, slot_dst, tile_expert, n_used, w_exp_gate[0], w_exp_up[0], w_exp_down[0])
    y_p, y_s = combine(y_flat, x1, route, p_prompt[0].reshape(Tp, -1), p_sample[0].reshape(Ts, -1),
                       w_ple[0].astype(BF16), w_ple_gate[0].astype(BF16), g_ple[0], g_final, tm)

    kq = P[:, 4 * D_B:5 * D_B]
    vq = P[:, 5 * D_B:6 * D_B]
    return (y_p.reshape(B, S, D), y_s.reshape(DB, DS, D),
            kq[:Tp].reshape(1, B, S, NH_B, DH_B), vq[:Tp].reshape(1, B, S, NH_B, DH_B),
            cv_p[None], C_p[None], n_p[None], m_p[None],
            kq[Tp:].reshape(1, DB, DS, NH_B, DH_B), vq[Tp:].reshape(1, DB, DS, NH_B, DH_B),
            cv_s[None], C_s[None], n_s[None], m_s[None])
```

```python
import functools
import math

import jax
import jax.numpy as jnp
from jax import lax
from jax.experimental import pallas as pl
from jax.experimental.pallas import tpu as pltpu

F32 = jnp.float32
BF16 = jnp.bfloat16
EPS = 1e-6

CHUNK = 64
NH_A, DQK_A, DV_A = 4, 128, 256
QK_A, D_A = NH_A * DQK_A, NH_A * DV_A
CONV_W = 4
NH_B, DH_B = 8, 128
D_B = NH_B * DH_B
N_GROUPS, E_PER_GROUP = 4, 8
N_EXPERTS = N_GROUPS * E_PER_GROUP
LANES = 128
EXPERT_TILE = 256
VMEM_LIMIT = 56 * 1024 * 1024


def _pick(n, cands):
    for c in cands:
        if n % c == 0:
            return c
    raise ValueError(f"no tile for {n} in {cands}")


def _params(sem):
    return pltpu.CompilerParams(dimension_semantics=sem, vmem_limit_bytes=VMEM_LIMIT)


def _rms(x, g):
    return x * lax.rsqrt(jnp.mean(x * x, axis=-1, keepdims=True) + EPS) * g


ROW_CHUNKS = 2048 // LANES
ROW_PITCH = 20


def _store_row_major(ref, x):
    rows = x.shape[0]
    for j in range(ROW_PITCH):
        piece = x[:, j * LANES:(j + 1) * LANES] if j < ROW_CHUNKS else jnp.zeros((rows, LANES), x.dtype)
        ref[pl.ds(j, rows, stride=ROW_PITCH), :] = piece


def _load_row_major(ref, rows):
    return jnp.concatenate([ref[pl.ds(j, rows, stride=ROW_PITCH), :] for j in range(ROW_CHUNKS)], axis=1)


def _log_sigmoid(x):
    return jnp.minimum(x, 0.0) - jnp.log(1.0 + jnp.exp(-jnp.abs(x)))


def _two_stream_specs(tm, D, n_first):
    return [pl.BlockSpec((tm, D), lambda i: (jnp.minimum(i, n_first - 1), 0)),
            pl.BlockSpec((tm, D), lambda i: (jnp.maximum(i - n_first, 0), 0))]


def _rms_kernel(xp_ref, xs_ref, g_ref, o_ref, *, n_first):
    x = jnp.where(pl.program_id(0) < n_first, xp_ref[...], xs_ref[...])
    o_ref[...] = _rms(x, g_ref[...]).astype(o_ref.dtype)


def rms_cast(xp, xs, g, tm):
    (Tp, D), Ts = xp.shape, xs.shape[0]
    T = Tp + Ts
    return pl.pallas_call(
        functools.partial(_rms_kernel, n_first=Tp // tm),
        out_shape=jax.ShapeDtypeStruct((T, D), BF16),
        grid=(T // tm,),
        in_specs=_two_stream_specs(tm, D, Tp // tm) + [pl.BlockSpec((1, D), lambda i: (0, 0))],
        out_specs=pl.BlockSpec((tm, D), lambda i: (i, 0)),
        compiler_params=_params(("arbitrary",)),
        name="rms_cast",
    )(xp, xs, g.reshape(1, D))


def _mm_kernel(a_ref, w_ref, o_ref):
    o_ref[...] = jnp.dot(a_ref[...], w_ref[...], preferred_element_type=F32)


def matmul(a, w, name):
    T, K = a.shape
    N = w.shape[1]
    tm = _pick(T, (1280, 640, 512, 256, 128, 64, 32, 16))
    tn = _pick(N, (512, 256, 128))
    return pl.pallas_call(
        _mm_kernel,
        out_shape=jax.ShapeDtypeStruct((T, N), F32),
        grid=(T // tm, N // tn),
        in_specs=[pl.BlockSpec((tm, K), lambda i, j: (i, 0)),
                  pl.BlockSpec((K, tn), lambda i, j: (0, j))],
        out_specs=pl.BlockSpec((tm, tn), lambda i, j: (i, j)),
        compiler_params=_params(("parallel", "arbitrary")),
        name=name,
    )(a, w)


def _mlstm_kernel(*refs, c, nc, nb):
    qk_refs, v_refs, o_refs, g_refs = (refs[k * nb:(k + 1) * nb] for k in range(4))
    (bif_ref, cw_ref, cb_ref, gh_ref, cbuf_ref, c0_ref, n0_ref, m0_ref,
     h_ref, cout_ref, nout_ref, mout_ref, convout_ref, xbuf, c_s, n_s, m_s) = refs[4 * nb:]
    j = pl.program_id(1)

    @pl.when(j == 0)
    def _():
        c_s[...] = c0_ref[...]
        n_s[...] = n0_ref[...]
        m_s[...] = m0_ref[...]
        xbuf[:, 5:8, :] = cbuf_ref[...]

    row = lax.broadcasted_iota(jnp.int32, (c, c), 0)
    col = lax.broadcasted_iota(jnp.int32, (c, c), 1)
    consts = dict(lane=lax.broadcasted_iota(jnp.int32, (c, LANES), 1), tril=(col <= row).astype(F32),
                  eye=(col == row).astype(F32), triu=(row <= col).astype(F32), causal=col <= row)
    last_rows = [
        _mlstm_chunk(qk_refs[b], v_refs[b], o_refs[b], g_refs[b], bif_ref, cw_ref, cb_ref, gh_ref,
                     h_ref.at[b], xbuf.at[b], c_s.at[b], n_s.at[b], m_s.at[b], consts, c)
        for b in range(nb)]

    @pl.when(j == nc - 1)
    def _():
        cout_ref[...] = c_s[...]
        nout_ref[...] = n_s[...]
        mout_ref[...] = m_s[...]
        for b in range(nb):
            convout_ref[b] = last_rows[b]


def _mlstm_chunk(qk_ref, v_ref, o_ref, g_ref, bif_ref, cw_ref, cb_ref, gh_ref, h_ref, xbuf, c_s, n_s, m_s,
                 consts, c):
    lane, tril, eye, triu, causal = (consts[k] for k in ("lane", "tril", "eye", "triu", "causal"))
    xbuf[8:8 + c, :] = qk_ref[...]
    y = cb_ref[...] + cw_ref[0:1, :] * xbuf[5:5 + c, :]
    for t in range(1, CONV_W):
        y = y + cw_ref[t:t + 1, :] * xbuf[5 + t:5 + t + c, :]
    last_rows = xbuf[c + 5:c + 8, :]
    xbuf[5:8, :] = last_rows
    qk = y * jax.nn.sigmoid(y)

    graw = g_ref[...] + bif_ref[...]
    xg = jnp.where(lane < NH_A, graw, jnp.where(lane < 2 * NH_A, _log_sigmoid(graw), 0.0))
    hp = lax.Precision.HIGHEST
    bc = jnp.dot(tril, xg, precision=hp, preferred_element_type=F32)
    dn0 = (((0,), (0,)), ((), ()))
    xt = lax.dot_general(xg, eye, dn0, precision=hp, preferred_element_type=F32)
    bt = lax.dot_general(xg, triu, dn0, precision=hp, preferred_element_type=F32)

    for h in range(NH_A):
        q = qk[:, h * DQK_A:(h + 1) * DQK_A]
        k = qk[:, QK_A + h * DQK_A:QK_A + (h + 1) * DQK_A] * (DQK_A ** -0.5)
        v = v_ref[:, h * DV_A:(h + 1) * DV_A]
        qb, kb, vb = q.astype(BF16), k.astype(BF16), v.astype(BF16)
        i_col = xg[:, h:h + 1]
        b_col = bc[:, NH_A + h:NH_A + h + 1]
        i_row = xt[h:h + 1, :]
        b_row = bt[NH_A + h:NH_A + h + 1, :]
        m_prev = m_s[0:1, h:h + 1]
        C = c_s[h]
        nvec = n_s[h:h + 1, :]

        dmat = jnp.where(causal, b_col - b_row + i_row, -jnp.inf)
        inter = b_col + m_prev
        m_t = jnp.maximum(inter, jnp.max(dmat, axis=1, keepdims=True))
        e = jnp.exp(dmat - m_t)
        s = lax.dot_general(qb, kb, (((1,), (1,)), ((), ())), preferred_element_type=F32) * e
        w_inter = jnp.exp(inter - m_t)
        num = (jnp.dot(s.astype(BF16), vb, preferred_element_type=F32)
               + w_inter * jnp.dot(qb, C.astype(BF16), preferred_element_type=F32))
        den = jnp.sum(s, axis=1, keepdims=True) + w_inter * jnp.sum(q * nvec, axis=1, keepdims=True)
        hh = num / jnp.maximum(jnp.abs(den), jnp.exp(-m_t))
        hh = hh * lax.rsqrt(jnp.mean(hh * hh, axis=1, keepdims=True) + EPS)
        hh = hh * gh_ref[:, h * DV_A:(h + 1) * DV_A]
        hh = jax.nn.sigmoid(o_ref[:, h * DV_A:(h + 1) * DV_A]) * hh
        h_ref[:, h * DV_A:(h + 1) * DV_A] = hh.astype(h_ref.dtype)

        b_end = b_col[c - 1:c, :]
        dec = b_end - b_col + i_col
        m_new = jnp.maximum(b_end + m_prev, jnp.max(dec, axis=0, keepdims=True))
        wk = jnp.exp(dec - m_new)
        w_old = jnp.exp(b_end + m_prev - m_new)
        kw = k * wk
        c_s[h] = w_old * C + lax.dot_general(kw.astype(BF16), vb, dn0, preferred_element_type=F32)
        n_s[h:h + 1, :] = w_old * nvec + jnp.sum(kw, axis=0, keepdims=True)
        m_s[0:1, h:h + 1] = m_new
    return last_rows


MLSTM_SEQS = 2


def mlstm(P, G, row0, N, L, b_if, conv_w, conv_b, g_head, conv_buf, C0, n0, m0):
    c = min(L, CHUNK)
    nc = L // c
    nb = MLSTM_SEQS if N % MLSTM_SEQS == 0 else 1
    base = row0 // c
    m0p = jnp.zeros((N, 1, LANES), F32).at[:, 0, :NH_A].set(m0)
    bif = jnp.zeros((1, LANES), F32).at[0, :2 * NH_A].set(b_if)

    def rows(width, col):
        return [pl.BlockSpec((c, width), lambda n, j, b=b: (base + (n * nb + b) * nc + j, col))
                for b in range(nb)]
    fix = lambda n, j: (0, 0)
    st4 = lambda n, j: (n, 0, 0, 0)
    st3 = lambda n, j: (n, 0, 0)
    outs = pl.pallas_call(
        functools.partial(_mlstm_kernel, c=c, nc=nc, nb=nb),
        out_shape=(jax.ShapeDtypeStruct((N, L, D_A), BF16),
                   jax.ShapeDtypeStruct((N, NH_A, DQK_A, DV_A), F32),
                   jax.ShapeDtypeStruct((N, NH_A, DQK_A), F32),
                   jax.ShapeDtypeStruct((N, 1, LANES), F32),
                   jax.ShapeDtypeStruct((N, CONV_W - 1, 2 * QK_A), F32)),
        grid=(N // nb, nc),
        in_specs=rows(2 * QK_A, 0) + rows(D_A, 1) + rows(D_A, 2) + rows(LANES, 0)
                 + [pl.BlockSpec((1, LANES), fix),
                    pl.BlockSpec((CONV_W, 2 * QK_A), fix),
                    pl.BlockSpec((1, 2 * QK_A), fix),
                    pl.BlockSpec((1, D_A), fix),
                    pl.BlockSpec((nb, CONV_W - 1, 2 * QK_A), st3),
                    pl.BlockSpec((nb, NH_A, DQK_A, DV_A), st4),
                    pl.BlockSpec((nb, NH_A, DQK_A), st3),
                    pl.BlockSpec((nb, 1, LANES), st3)],
        out_specs=(pl.BlockSpec((nb, c, D_A), lambda n, j: (n, j, 0)),
                   pl.BlockSpec((nb, NH_A, DQK_A, DV_A), st4),
                   pl.BlockSpec((nb, NH_A, DQK_A), st3),
                   pl.BlockSpec((nb, 1, LANES), st3),
                   pl.BlockSpec((nb, CONV_W - 1, 2 * QK_A), st3)),
        scratch_shapes=[pltpu.VMEM((nb, c + 8, 2 * QK_A), F32),
                        pltpu.VMEM((nb, NH_A, DQK_A, DV_A), F32),
                        pltpu.VMEM((nb, NH_A, DQK_A), F32),
                        pltpu.VMEM((nb, 1, LANES), F32)],
        compiler_params=_params(("parallel", "arbitrary")),
        name="mlstm",
    )(*([P] * (3 * nb) + [G] * nb), bif, conv_w, conv_b.reshape(1, -1), g_head.reshape(1, -1),
      conv_buf, C0, n0, m0p)
    h, C1, n1, m1, cv = outs
    return h.reshape(N * L, D_A), C1, n1, m1[:, 0, :NH_A], cv


STICK_UNDERFLOW = -110.0


def _later_matrix(tk):
    jr = lax.broadcasted_iota(jnp.int32, (tk, tk), 0)
    sc = lax.broadcasted_iota(jnp.int32, (tk, tk), 1)
    return (jr > sc).astype(BF16)


def _sb_block(q_bf, k_blk, v_blk, R, later, mask):
    z = lax.dot_general(q_bf, k_blk.astype(BF16), (((1,), (1,)), ((), ())),
                        preferred_element_type=F32) * (DH_B ** -0.5)
    l1mb = -(jnp.maximum(z, 0.0) + jnp.log(1.0 + jnp.exp(-jnp.abs(z))))
    if mask is not None:
        l1mb = jnp.where(mask, l1mb, 0.0)
    hi = l1mb.astype(BF16)
    lo = (l1mb - hi.astype(F32)).astype(BF16)
    rest = (jnp.dot(hi, later, preferred_element_type=F32)
            + jnp.dot(lo, later, preferred_element_type=F32)) + R
    a = jnp.exp(z + l1mb + rest)
    if mask is not None:
        a = jnp.where(mask, a, 0.0)
    pv = jnp.dot(a.astype(BF16), v_blk.astype(BF16), preferred_element_type=F32)
    return pv, R + jnp.sum(l1mb, axis=1, keepdims=True)


SB_HEADS = 2


def _sb_kernel(q_ref, k_ref, v_ref, o_ref, *, tq):
    qi = pl.program_id(2)
    rr = lax.broadcasted_iota(jnp.int32, (tq, tq), 0)
    cc = lax.broadcasted_iota(jnp.int32, (tq, tq), 1)
    later = _later_matrix(tq)
    cols = [slice(h * DH_B, (h + 1) * DH_B) for h in range(SB_HEADS)]
    q_bf = [q_ref[:, c].astype(BF16) for c in cols]
    start = pl.multiple_of(qi * tq, tq)
    accs, Rs = [], []
    for h, c in enumerate(cols):
        acc, R = _sb_block(q_bf[h], k_ref[pl.ds(start, tq), c], v_ref[pl.ds(start, tq), c],
                           jnp.zeros((tq, 1), F32), later, cc < rr)
        accs.append(acc)
        Rs.append(R)

    def rmax(Rs):
        return jnp.max(jnp.concatenate(Rs, axis=1))

    def cond(c):
        return jnp.logical_and(c[0] < qi, c[3] > STICK_UNDERFLOW)

    def body(carry):
        t, accs, Rs, _ = carry
        st = pl.multiple_of((qi - 1 - t) * tq, tq)
        new_accs, new_Rs = [], []
        for h, c in enumerate(cols):
            pv, R = _sb_block(q_bf[h], k_ref[pl.ds(st, tq), c], v_ref[pl.ds(st, tq), c], Rs[h], later, None)
            new_accs.append(accs[h] + pv)
            new_Rs.append(R)
        return t + 1, tuple(new_accs), tuple(new_Rs), rmax(new_Rs)

    _, accs, _, _ = lax.while_loop(cond, body, (jnp.int32(0), tuple(accs), tuple(Rs), rmax(Rs)))
    for h, c in enumerate(cols):
        o_ref[:, c] = accs[h].astype(o_ref.dtype)


QB_BLK, KB_BLK, VB_BLK = 3 * D_B // DH_B, 4 * D_B // DH_B, 5 * D_B // DH_B


def sb_attention(P, row0, N, L):
    tq = min(L, 256)
    nq = L // tq
    W = SB_HEADS * DH_B
    qb, kb, vb = QB_BLK // SB_HEADS, KB_BLK // SB_HEADS, VB_BLK // SB_HEADS
    return pl.pallas_call(
        functools.partial(_sb_kernel, tq=tq),
        out_shape=jax.ShapeDtypeStruct((N * L, D_B), BF16),
        grid=(N, NH_B // SB_HEADS, nq),
        in_specs=[pl.BlockSpec((tq, W), lambda n, h, i: (row0 // tq + n * nq + i, qb + h)),
                  pl.BlockSpec((L, W), lambda n, h, i: (row0 // L + n, kb + h)),
                  pl.BlockSpec((L, W), lambda n, h, i: (row0 // L + n, vb + h))],
        out_specs=pl.BlockSpec((tq, W), lambda n, h, i: (n * nq + i, h)),
        compiler_params=_params(("parallel", "parallel", "arbitrary")),
        name="sb_attention",
    )(P, P, P)


def _sb_past_kernel(q_ref, k_ref, v_ref, pk_hbm, pv_hbm, o_ref, kbuf, vbuf, ksem, vsem, *, L, tk, n_blk):
    n = pl.program_id(0)

    def copies(blk):
        st = blk * tk
        return ([pltpu.make_async_copy(pk_hbm.at[0, n, pl.ds(st, tk), h, :], kbuf.at[h], ksem)
                 for h in range(NH_B)]
                + [pltpu.make_async_copy(pv_hbm.at[0, n, pl.ds(st, tk), h, :], vbuf.at[h], vsem)
                   for h in range(NH_B)])

    rr = lax.broadcasted_iota(jnp.int32, (L, L), 0)
    cc = lax.broadcasted_iota(jnp.int32, (L, L), 1)
    later_new = _later_matrix(L)
    later = _later_matrix(tk)
    q_bf = [q_ref[:, h * DH_B:(h + 1) * DH_B].astype(BF16) for h in range(NH_B)]
    accs, Rs = [], []
    for h in range(NH_B):
        sl = slice(h * DH_B, (h + 1) * DH_B)
        acc, R = _sb_block(q_bf[h], k_ref[:, sl], v_ref[:, sl], jnp.zeros((L, 1), F32), later_new, cc < rr)
        accs.append(acc)
        Rs.append(R)

    def rmax(Rs):
        return jnp.max(jnp.concatenate(Rs, axis=1))

    def cond(c):
        return jnp.logical_and(c[0] < n_blk, c[3] > STICK_UNDERFLOW)

    def body(c):
        t, accs, Rs, _ = c
        cps = copies(n_blk - 1 - t)
        for cp in cps:
            cp.start()
        for cp in cps:
            cp.wait()
        new_accs, new_Rs = [], []
        for h in range(NH_B):
            pv, R = _sb_block(q_bf[h], kbuf[h], vbuf[h], Rs[h], later, None)
            new_accs.append(accs[h] + pv)
            new_Rs.append(R)
        return t + 1, tuple(new_accs), tuple(new_Rs), rmax(new_Rs)

    _, accs, _, _ = lax.while_loop(cond, body, (jnp.int32(0), tuple(accs), tuple(Rs), rmax(Rs)))
    for h in range(NH_B):
        o_ref[:, h * DH_B:(h + 1) * DH_B] = accs[h].astype(o_ref.dtype)


def sb_attention_past(P, row0, N, L, past_k, past_v):
    Lp = past_k.shape[2]
    tk = min(Lp, 256)
    blk = lambda c: pl.BlockSpec((L, D_B), lambda n: (row0 // L + n, c))
    return pl.pallas_call(
        functools.partial(_sb_past_kernel, L=L, tk=tk, n_blk=Lp // tk),
        out_shape=jax.ShapeDtypeStruct((N * L, D_B), BF16),
        grid=(N,),
        in_specs=[blk(3), blk(4), blk(5),
                  pl.BlockSpec(memory_space=pl.ANY), pl.BlockSpec(memory_space=pl.ANY)],
        out_specs=pl.BlockSpec((L, D_B), lambda n: (n, 0)),
        scratch_shapes=[pltpu.VMEM((NH_B, tk, DH_B), F32), pltpu.VMEM((NH_B, tk, DH_B), F32),
                        pltpu.SemaphoreType.DMA(()), pltpu.SemaphoreType.DMA(())],
        compiler_params=_params(("arbitrary",)),
        name="sb_attention_past",
    )(P, P, P, past_k, past_v)


def _merge_kernel(h_ref, ha_ref, hb_ref, wga_ref, wgb_ref, wpa_ref, wpb_ref, o_ref):
    h = h_ref[...]
    ga = jnp.dot(h, wga_ref[...], preferred_element_type=F32)
    gb = jnp.dot(h, wgb_ref[...], preferred_element_type=F32)
    pa = jnp.dot(ha_ref[...], wpa_ref[...], preferred_element_type=F32)
    pb = jnp.dot(hb_ref[...], wpb_ref[...], preferred_element_type=F32)
    o_ref[...] = (jax.nn.sigmoid(ga) * pa + jax.nn.sigmoid(gb) * pb).astype(o_ref.dtype)


def merge(h, row0, ha, hb, wga, wgb, wpa, wpb):
    Tn = ha.shape[0]
    D = h.shape[1]
    tm = _pick(math.gcd(Tn, row0) if row0 else Tn, (1024, 512, 256, 128, 64, 32, 16))
    tn = _pick(D, (512, 256, 128))
    row = lambda i, j: (i, 0)
    colw = lambda i, j: (0, j)
    return pl.pallas_call(
        _merge_kernel,
        out_shape=jax.ShapeDtypeStruct((Tn, D), BF16),
        grid=(Tn // tm, D // tn),
        in_specs=[pl.BlockSpec((tm, D), lambda i, j: (row0 // tm + i, 0)),
                  pl.BlockSpec((tm, D_A), row), pl.BlockSpec((tm, D_B), row),
                  pl.BlockSpec((D, tn), colw), pl.BlockSpec((D, tn), colw),
                  pl.BlockSpec((D_A, tn), colw), pl.BlockSpec((D_B, tn), colw)],
        out_specs=pl.BlockSpec((tm, tn), lambda i, j: (i, j)),
        compiler_params=_params(("parallel", "arbitrary")),
        name="merge",
    )(h, ha, hb, wga, wgb, wpa, wpb)


def _wout_route_kernel(mgp_ref, mgs_ref, xp_ref, xs_ref, wo_ref, g_ref, wr_ref, br_ref, x1_ref, t_ref, r_ref,
                       cnt_ref, run_s, *, n_first):
    i = pl.program_id(0)

    @pl.when(i == 0)
    def _():
        run_s[...] = jnp.zeros_like(run_s)

    x = jnp.where(i < n_first, xp_ref[...], xs_ref[...])
    mg = jnp.where(i < n_first, mgp_ref[...], mgs_ref[...])
    x1 = x + jnp.dot(mg, wo_ref[...], preferred_element_type=F32)
    x1_ref[...] = x1
    t = _rms(x1, g_ref[...])
    _store_row_major(t_ref, t)
    lg = jnp.dot(t.astype(BF16), wr_ref[...], preferred_element_type=F32) + br_ref[...]
    tm = lg.shape[0]
    lane = lax.broadcasted_iota(jnp.int32, (tm, LANES), 1)
    ninf = -jnp.inf
    glm = jnp.where(lane < N_GROUPS, lg, ninf)
    gmax = jnp.max(glm, axis=1, keepdims=True)
    gsel = jnp.min(jnp.where(glm == gmax, lane, LANES), axis=1, keepdims=True)
    p_sel = 1.0 / jnp.sum(jnp.exp(glm - gmax), axis=1, keepdims=True)
    lo = N_GROUPS + E_PER_GROUP * gsel
    elm = jnp.where((lane >= lo) & (lane < lo + E_PER_GROUP), lg, ninf)
    m1 = jnp.max(elm, axis=1, keepdims=True)
    i1 = jnp.min(jnp.where(elm == m1, lane, LANES), axis=1, keepdims=True)
    elm2 = jnp.where(lane == i1, ninf, elm)
    m2 = jnp.max(elm2, axis=1, keepdims=True)
    i2 = jnp.min(jnp.where(elm2 == m2, lane, LANES), axis=1, keepdims=True)
    e2 = jnp.exp(m2 - m1)
    w1 = p_sel / (1.0 + e2)
    w2 = p_sel * e2 / (1.0 + e2)
    e1 = (lane == i1 - N_GROUPS).astype(F32)
    e2h = (lane == i2 - N_GROUPS).astype(F32)
    both = e1 + e2h
    rr = lax.broadcasted_iota(jnp.int32, (tm, tm), 0)
    cc = lax.broadcasted_iota(jnp.int32, (tm, tm), 1)
    before = jnp.dot((cc < rr).astype(BF16), both.astype(BF16), preferred_element_type=F32) + run_s[...]
    rank1 = jnp.sum(e1 * before, axis=1, keepdims=True)
    rank2 = jnp.sum(e2h * before, axis=1, keepdims=True)
    run_s[...] = run_s[...] + jnp.sum(both, axis=0, keepdims=True)
    cnt_ref[...] = run_s[...]
    r = jnp.where(lane == 0, (i1 - N_GROUPS).astype(F32),
        jnp.where(lane == 1, (i2 - N_GROUPS).astype(F32),
        jnp.where(lane == 2, w1, jnp.where(lane == 3, w2,
        jnp.where(lane == 4, rank1, jnp.where(lane == 5, rank2, 0.0))))))
    r_ref[...] = r


def wout_route(mg_p, mg_s, xp, xs, wo, g_ffn, wr, br, tm):
    (Tp, D), Ts = xp.shape, xs.shape[0]
    T = Tp + Ts
    row = lambda i: (i, 0)
    fix = lambda i: (0, 0)
    return pl.pallas_call(
        functools.partial(_wout_route_kernel, n_first=Tp // tm),
        out_shape=(jax.ShapeDtypeStruct((T, D), F32), jax.ShapeDtypeStruct((T * ROW_PITCH, LANES), F32),
                   jax.ShapeDtypeStruct((T, LANES), F32), jax.ShapeDtypeStruct((1, LANES), F32)),
        grid=(T // tm,),
        in_specs=_two_stream_specs(tm, D, Tp // tm) + _two_stream_specs(tm, D, Tp // tm)
                 + [pl.BlockSpec((D, D), fix), pl.BlockSpec((1, D), fix), pl.BlockSpec((D, LANES), fix),
                    pl.BlockSpec((1, LANES), fix)],
        out_specs=(pl.BlockSpec((tm, D), row), pl.BlockSpec((tm * ROW_PITCH, LANES), row),
                   pl.BlockSpec((tm, LANES), row),
                   pl.BlockSpec((1, LANES), fix)),
        scratch_shapes=[pltpu.VMEM((1, LANES), F32)],
        compiler_params=_params(("arbitrary",)),
        name="wout_route",
    )(mg_p, mg_s, xp, xs, wo, g_ffn.reshape(1, D), wr, br)


IDX_RING = 8


def _expert_kernel(te_ref, nu_ref, src_hbm, dst_hbm, t_hbm, wg_ref, wu_ref, wd_ref, y_hbm,
                   sidx, didx, xb0, xb1, yb0, yb1, wg_s, wu_s, wd_s, gsem, ssem, isem, *, dump0):
    i = pl.program_id(0)
    nu = nu_ref[0]
    TE = EXPERT_TILE
    xb, yb = (xb0, xb1), (yb0, yb1)

    def table_copies(tile, seq):
        base = (seq & (IDX_RING - 1)) * TE
        return (pltpu.make_async_copy(src_hbm.at[pl.ds(tile * TE, TE)], sidx.at[pl.ds(base, TE)], isem.at[0]),
                pltpu.make_async_copy(dst_hbm.at[pl.ds(tile * TE, TE)], didx.at[pl.ds(base, TE)], isem.at[1]))

    def gather(seq, par):
        base = (seq & (IDX_RING - 1)) * TE
        return [pltpu.make_async_copy(t_hbm.at[pl.ds(sidx[base + r], ROW_CHUNKS), :],
                                      xb[par].at[pl.ds(r * ROW_PITCH, ROW_CHUNKS), :],
                                      gsem.at[par]) for r in range(TE)]

    def scatter(seq, par):
        base = (seq & (IDX_RING - 1)) * TE
        return [pltpu.make_async_copy(yb[par].at[pl.ds(r * ROW_PITCH, ROW_PITCH), :],
                                      y_hbm.at[pl.ds(didx[base + r], ROW_PITCH), :],
                                      ssem.at[par]) for r in range(TE)]

    def start(cps):
        rows = len(cps) == TE
        for n, cp in enumerate(cps):
            cp.start(priority=n % 2 if rows else 0)

    def wait(cps):
        for cp in cps:
            cp.wait()

    last = nu - 1

    @pl.when(i == 0)
    def _():
        first = table_copies(0, 0)
        start(first)
        wait(first)
        start(table_copies(jnp.minimum(1, last), 1))
        start(gather(0, 0))
        yb0[...] = jnp.zeros_like(yb0)
        yb1[...] = jnp.zeros_like(yb1)
        for r in range(TE):
            didx[(IDX_RING - 1) * TE + r] = (dump0 + TE + r) * ROW_PITCH
        fill = pltpu.make_async_copy(yb1, y_hbm.at[pl.ds(dump0 * ROW_PITCH, TE * ROW_PITCH), :], ssem.at[0])
        fill.start()
        fill.wait()

    @pl.when((i < nu) & ((i == 0) | (te_ref[i] != te_ref[jnp.maximum(i - 1, 0)])))
    def _():
        wg_s[...] = wg_ref[...].astype(BF16)
        wu_s[...] = wu_ref[...].astype(BF16)
        wd_s[...] = wd_ref[...].astype(BF16)

    def step(par):
        wait(table_copies(jnp.minimum(i + 1, last), i + 1))
        start(table_copies(jnp.minimum(i + 2, last), i + 2))

        @pl.when(i >= 1)
        def _():
            wait(scatter(i - 2, par))

        wait(gather(i, par))
        start(scatter(i - 1, 1 - par))
        start(gather(i + 1, 1 - par))
        x = _load_row_major(xb[par], TE).astype(BF16)
        a = jnp.dot(x, wg_s[...], preferred_element_type=F32)
        u = jnp.dot(x, wu_s[...], preferred_element_type=F32)
        hid = (a * jax.nn.sigmoid(a) * u).astype(BF16)
        y = jnp.dot(hid, wd_s[...], preferred_element_type=F32)
        for j in range(ROW_CHUNKS):
            yb[par][pl.ds(j, TE, stride=ROW_PITCH), :] = y[:, j * LANES:(j + 1) * LANES]

        @pl.when(i == last)
        def _():
            start(scatter(i, par))
            wait(scatter(i - 1, 1 - par))
            wait(scatter(i, par))
            wait(gather(i + 1, 1 - par))
            wait(table_copies(last, i + 2))

    for par in (0, 1):
        pl.when((i < nu) & (i % 2 == par))(functools.partial(step, par))


def experts(t, slot_src, slot_dst, tile_expert, n_used, w_gate, w_up, w_down):
    T = t.shape[0] // ROW_PITCH
    D, F = w_gate.shape[-2:]
    nt = slot_src.shape[0] // EXPERT_TILE
    wmap = lambda i, te, nu: (te[i], 0, 0)
    any_spec = pl.BlockSpec(memory_space=pl.ANY)
    tile_buf = pltpu.VMEM((EXPERT_TILE * ROW_PITCH, LANES), F32)
    grid_spec = pltpu.PrefetchScalarGridSpec(
        num_scalar_prefetch=2,
        grid=(nt,),
        in_specs=[any_spec, any_spec, any_spec,
                  pl.BlockSpec((None, D, F), wmap), pl.BlockSpec((None, D, F), wmap),
                  pl.BlockSpec((None, F, D), wmap)],
        out_specs=any_spec,
        scratch_shapes=[pltpu.SMEM((IDX_RING * EXPERT_TILE,), jnp.int32),
                        pltpu.SMEM((IDX_RING * EXPERT_TILE,), jnp.int32),
                        tile_buf, tile_buf, tile_buf, tile_buf,
                        pltpu.VMEM((D, F), BF16), pltpu.VMEM((D, F), BF16), pltpu.VMEM((F, D), BF16),
                        pltpu.SemaphoreType.DMA((2,)), pltpu.SemaphoreType.DMA((2,)),
                        pltpu.SemaphoreType.DMA((2,))],
    )
    return pl.pallas_call(
        functools.partial(_expert_kernel, dump0=2 * T),
        out_shape=jax.ShapeDtypeStruct(((2 * T + 2 * EXPERT_TILE) * ROW_PITCH, LANES), F32),
        grid_spec=grid_spec,
        compiler_params=_params(("arbitrary",)),
        name="moe_experts",
    )(tile_expert, n_used, slot_src, slot_dst, t, w_gate.reshape(N_EXPERTS, D, F),
      w_up.reshape(N_EXPERTS, D, F), w_down.reshape(N_EXPERTS, F, D))


def _combine_kernel(x1_ref, y0_ref, y1_ref, r_ref, pp_ref, ps_ref, wple_ref, wpg_ref, gple_ref, gfin_ref,
                    yp_ref, ys_ref, *, n_first):
    i = pl.program_id(0)
    p = jnp.where(i < n_first, pp_ref[...], ps_ref[...])
    r = r_ref[...]
    tm = r.shape[0]
    x2 = x1_ref[...] + r[:, 2:3] * _load_row_major(y0_ref, tm) + r[:, 3:4] * _load_row_major(y1_ref, tm)
    hp = _rms(x2, gple_ref[...]).astype(BF16)
    gate = jax.nn.sigmoid(jnp.dot(hp, wpg_ref[...], preferred_element_type=F32))
    emb = jnp.dot(p.astype(BF16), wple_ref[...], preferred_element_type=F32)
    y = _rms(x2 + emb * gate, gfin_ref[...])

    @pl.when(i < n_first)
    def _():
        yp_ref[...] = y

    @pl.when(i >= n_first)
    def _():
        ys_ref[...] = y


def combine(y_flat, x1, route, p_p, p_s, w_ple, w_ple_gate, g_ple, g_final, tm):
    T, D = x1.shape
    row = lambda i: (i, 0)
    row1 = lambda i: (i + T // tm, 0)
    fix = lambda i: (0, 0)
    Tp = p_p.shape[0]
    n_first = Tp // tm
    return pl.pallas_call(
        functools.partial(_combine_kernel, n_first=n_first),
        out_shape=(jax.ShapeDtypeStruct((Tp, D), F32), jax.ShapeDtypeStruct((T - Tp, D), F32)),
        grid=(T // tm,),
        in_specs=[pl.BlockSpec((tm, D), row), pl.BlockSpec((tm * ROW_PITCH, LANES), row),
                  pl.BlockSpec((tm * ROW_PITCH, LANES), row1),
                  pl.BlockSpec((tm, LANES), row)] + _two_stream_specs(tm, p_p.shape[1], n_first)
                 + [pl.BlockSpec(w_ple.shape, fix), pl.BlockSpec((D, D), fix),
                  pl.BlockSpec((1, D), fix), pl.BlockSpec((1, D), fix)],
        out_specs=tuple(_two_stream_specs(tm, D, n_first)),
        compiler_params=_params(("arbitrary",)),
        name="moe_combine_ple",
    )(x1, y_flat, y_flat, route, p_p, p_s, w_ple, w_ple_gate, g_ple.reshape(1, D), g_final.reshape(1, D))


def _routing_tables(route, counts, n_tiles):
    ids = route[:, 0:2].astype(jnp.int32).reshape(-1)
    rank = route[:, 4:6].astype(jnp.int32).reshape(-1)
    counts = counts.astype(jnp.int32)
    tiles = (counts + EXPERT_TILE - 1) // EXPERT_TILE
    tile_end = jnp.cumsum(tiles)
    tile_start = tile_end - tiles
    onehot = ids[:, None] == jnp.arange(N_EXPERTS, dtype=jnp.int32)[None, :]
    pos = jnp.sum(jnp.where(onehot, tile_start[None, :], 0), axis=1) * EXPERT_TILE + rank
    slot_assign = jnp.full((n_tiles * EXPERT_TILE,), -1, jnp.int32).at[pos].set(
        jnp.arange(ids.shape[0], dtype=jnp.int32), unique_indices=True)
    n_assign = ids.shape[0]
    slot = jnp.arange(n_tiles * EXPERT_TILE, dtype=jnp.int32)
    token = jnp.maximum(slot_assign, 0) // 2
    slot_src = token * ROW_PITCH
    dump = n_assign + ((slot // EXPERT_TILE) % 2) * EXPERT_TILE + slot % EXPERT_TILE
    slot_dst = jnp.where(slot_assign >= 0, (slot_assign % 2) * (n_assign // 2) + token, dump) * ROW_PITCH
    n_used = tile_end[-1]
    tidx = jnp.minimum(jnp.arange(n_tiles, dtype=jnp.int32), n_used - 1)
    tile_expert = jnp.sum((tidx[:, None] >= tile_end[None, :]).astype(jnp.int32), axis=1)
    return slot_src, slot_dst, tile_expert.astype(jnp.int32), n_used.reshape(1).astype(jnp.int32)


def kernel(x_prompt, x_sample, cache_k, cache_v, state_conv, state_C, state_n, state_m, p_prompt, p_sample,
           g_mix, w_in, b_if, conv_w, conv_b, g_head_a, w_proj_a, w_proj_b, w_out, g_ffn, w_route_g,
           b_route_g, w_route_e, b_route_e, w_exp_gate, w_exp_up, w_exp_down, g_ple, w_ple, w_ple_gate,
           g_final):
    assert w_in.shape[0] == 1, "single layer"
    B, S, D = x_prompt.shape
    DB, DS, _ = x_sample.shape
    Tp, Ts = B * S, DB * DS
    T = Tp + Ts
    xp, xs = x_prompt.reshape(Tp, D), x_sample.reshape(Ts, D)
    tm = _pick(math.gcd(Tp, Ts), (256, 128, 64, 32, 16, 8))

    wi = w_in[0]
    g0 = 2 * QK_A + D_A
    m0 = g0 + 2 * NH_A
    e0 = m0 + D_A + 3 * D_B
    w_main = jnp.concatenate([wi[:, :g0], wi[:, m0:e0]], axis=1).astype(BF16)
    w_gates = jnp.zeros((D, LANES), F32).at[:, :2 * NH_A].set(wi[:, g0:m0]).astype(BF16)
    wga = wi[:, e0:e0 + D].astype(BF16)
    wgb = wi[:, e0 + D:].astype(BF16)

    h = rms_cast(xp, xs, g_mix[0], tm)
    P = matmul(h, w_main, "proj_main")
    G = matmul(h, w_gates, "proj_gates")

    zeros = functools.partial(jnp.zeros, dtype=F32)
    ha_p, C_p, n_p, m_p, cv_p = mlstm(P, G, 0, B, S, b_if[0], conv_w[0], conv_b[0], g_head_a[0],
                                      zeros((B, CONV_W - 1, 2 * QK_A)), zeros((B, NH_A, DQK_A, DV_A)),
                                      zeros((B, NH_A, DQK_A)), zeros((B, NH_A)))
    ha_s, C_s, n_s, m_s, cv_s = mlstm(P, G, Tp, DB, DS, b_if[0], conv_w[0], conv_b[0], g_head_a[0],
                                      state_conv[0], state_C[0], state_n[0], state_m[0])
    hb_p = sb_attention(P, 0, B, S)
    hb_s = sb_attention_past(P, Tp, DB, DS, cache_k, cache_v)
    wpa, wpb = w_proj_a[0].astype(BF16), w_proj_b[0].astype(BF16)
    mg_p = merge(h, 0, ha_p, hb_p, wga, wgb, wpa, wpb)
    mg_s = merge(h, Tp, ha_s, hb_s, wga, wgb, wpa, wpb)

    wr = jnp.zeros((D, LANES), F32)
    wr = wr.at[:, :N_GROUPS].set(w_route_g[0])
    wr = wr.at[:, N_GROUPS:N_GROUPS + N_EXPERTS].set(
        jnp.transpose(w_route_e[0], (1, 0, 2)).reshape(D, N_EXPERTS))
    br = jnp.zeros((1, LANES), F32)
    br = br.at[0, :N_GROUPS].set(b_route_g[0])
    br = br.at[0, N_GROUPS:N_GROUPS + N_EXPERTS].set(b_route_e[0].reshape(-1))
    x1, t, route, counts = wout_route(mg_p, mg_s, xp, xs, w_out[0].astype(BF16), g_ffn[0], wr.astype(BF16), br, tm)

    n_tiles = (2 * T + N_EXPERTS * (EXPERT_TILE - 1)) // EXPERT_TILE + 1
    slot_src, slot_dst, tile_expert, n_used = _routing_tables(route, counts[0, :N_EXPERTS], n_tiles)
    y_flat = experts(t, slot_src, slot_dst, tile_expert, n_used, w_exp_gate[0], w_exp_up[0], w_exp_down[0])
    y_p, y_s = combine(y_flat, x1, route, p_prompt[0].reshape(Tp, -1), p_sample[0].reshape(Ts, -1),
                       w_ple[0].astype(BF16), w_ple_gate[0].astype(BF16), g_ple[0], g_final, tm)

    kq = P[:, 4 * D_B:5 * D_B]
    vq = P[:, 5 * D_B:6 * D_B]
    return (y_p.reshape(B, S, D), y_s.reshape(DB, DS, D),
            kq[:Tp].reshape(1, B, S, NH_B, DH_B), vq[:Tp].reshape(1, B, S, NH_B, DH_B),
            cv_p[None], C_p[None], n_p[None], m_p[None],
            kq[Tp:].reshape(1, DB, DS, NH_B, DH_B), vq[Tp:].reshape(1, DB, DS, NH_B, DH_B),
            cv_s[None], C_s[None], n_s[None], m_s[None])
```

```python
import functools
import math

import jax
import jax.numpy as jnp
from jax import lax
from jax.experimental import pallas as pl
from jax.experimental.pallas import tpu as pltpu

F32 = jnp.float32
BF16 = jnp.bfloat16
EPS = 1e-6

CHUNK = 128
NH_A, DQK_A, DV_A = 4, 128, 256
QK_A, D_A = NH_A * DQK_A, NH_A * DV_A
CONV_W = 4
NH_B, DH_B = 8, 128
D_B = NH_B * DH_B
N_GROUPS, E_PER_GROUP = 4, 8
N_EXPERTS = N_GROUPS * E_PER_GROUP
LANES = 128
EXPERT_TILE = 256
VMEM_LIMIT = 56 * 1024 * 1024


def _pick(n, cands):
    for c in cands:
        if n % c == 0:
            return c
    raise ValueError(f"no tile for {n} in {cands}")


def _params(sem):
    return pltpu.CompilerParams(dimension_semantics=sem, vmem_limit_bytes=VMEM_LIMIT)


def _rms(x, g):
    return x * lax.rsqrt(jnp.mean(x * x, axis=-1, keepdims=True) + EPS) * g


ROW_CHUNKS = 2048 // LANES
ROW_PITCH = 20


def _store_row_major(ref, x):
    rows = x.shape[0]
    for j in range(ROW_PITCH):
        piece = x[:, j * LANES:(j + 1) * LANES] if j < ROW_CHUNKS else jnp.zeros((rows, LANES), x.dtype)
        ref[pl.ds(j, rows, stride=ROW_PITCH), :] = piece


def _load_row_major(ref, rows):
    return jnp.concatenate([ref[pl.ds(j, rows, stride=ROW_PITCH), :] for j in range(ROW_CHUNKS)], axis=1)


def _log_sigmoid(x):
    return jnp.minimum(x, 0.0) - jnp.log(1.0 + jnp.exp(-jnp.abs(x)))


def _two_stream_specs(tm, D, n_first):
    return [pl.BlockSpec((tm, D), lambda i: (jnp.minimum(i, n_first - 1), 0)),
            pl.BlockSpec((tm, D), lambda i: (jnp.maximum(i - n_first, 0), 0))]


def _rms_kernel(xp_ref, xs_ref, g_ref, o_ref, *, n_first):
    x = jnp.where(pl.program_id(0) < n_first, xp_ref[...], xs_ref[...])
    o_ref[...] = _rms(x, g_ref[...]).astype(o_ref.dtype)


def rms_cast(xp, xs, g, tm):
    (Tp, D), Ts = xp.shape, xs.shape[0]
    T = Tp + Ts
    return pl.pallas_call(
        functools.partial(_rms_kernel, n_first=Tp // tm),
        out_shape=jax.ShapeDtypeStruct((T, D), BF16),
        grid=(T // tm,),
        in_specs=_two_stream_specs(tm, D, Tp // tm) + [pl.BlockSpec((1, D), lambda i: (0, 0))],
        out_specs=pl.BlockSpec((tm, D), lambda i: (i, 0)),
        compiler_params=_params(("arbitrary",)),
        name="rms_cast",
    )(xp, xs, g.reshape(1, D))


def _mm_kernel(a_ref, w_ref, o_ref):
    o_ref[...] = jnp.dot(a_ref[...], w_ref[...], preferred_element_type=F32)


def matmul(a, w, name):
    T, K = a.shape
    N = w.shape[1]
    tm = _pick(T, (1280, 640, 512, 256, 128, 64, 32, 16))
    tn = _pick(N, (512, 256, 128))
    return pl.pallas_call(
        _mm_kernel,
        out_shape=jax.ShapeDtypeStruct((T, N), F32),
        grid=(T // tm, N // tn),
        in_specs=[pl.BlockSpec((tm, K), lambda i, j: (i, 0)),
                  pl.BlockSpec((K, tn), lambda i, j: (0, j))],
        out_specs=pl.BlockSpec((tm, tn), lambda i, j: (i, j)),
        compiler_params=_params(("parallel", "arbitrary")),
        name=name,
    )(a, w)


def _mlstm_kernel(*refs, c, nc, nb):
    qk_refs, v_refs, o_refs, g_refs = (refs[k * nb:(k + 1) * nb] for k in range(4))
    (bif_ref, cw_ref, cb_ref, gh_ref, cbuf_ref, c0_ref, n0_ref, m0_ref,
     h_ref, cout_ref, nout_ref, mout_ref, convout_ref, xbuf, c_s, n_s, m_s) = refs[4 * nb:]
    j = pl.program_id(1)

    @pl.when(j == 0)
    def _():
        c_s[...] = c0_ref[...]
        n_s[...] = n0_ref[...]
        m_s[...] = m0_ref[...]
        xbuf[:, 5:8, :] = cbuf_ref[...]

    row = lax.broadcasted_iota(jnp.int32, (c, c), 0)
    col = lax.broadcasted_iota(jnp.int32, (c, c), 1)
    consts = dict(lane=lax.broadcasted_iota(jnp.int32, (c, LANES), 1), tril=(col <= row).astype(F32),
                  eye=(col == row).astype(F32), triu=(row <= col).astype(F32), causal=col <= row)
    last_rows = [
        _mlstm_chunk(qk_refs[b], v_refs[b], o_refs[b], g_refs[b], bif_ref, cw_ref, cb_ref, gh_ref,
                     h_ref.at[b], xbuf.at[b], c_s.at[b], n_s.at[b], m_s.at[b], consts, c)
        for b in range(nb)]

    @pl.when(j == nc - 1)
    def _():
        cout_ref[...] = c_s[...]
        nout_ref[...] = n_s[...]
        mout_ref[...] = m_s[...]
        for b in range(nb):
            convout_ref[b] = last_rows[b]


def _mlstm_chunk(qk_ref, v_ref, o_ref, g_ref, bif_ref, cw_ref, cb_ref, gh_ref, h_ref, xbuf, c_s, n_s, m_s,
                 consts, c):
    lane, tril, eye, triu, causal = (consts[k] for k in ("lane", "tril", "eye", "triu", "causal"))
    xbuf[8:8 + c, :] = qk_ref[...]
    y = cb_ref[...] + cw_ref[0:1, :] * xbuf[5:5 + c, :]
    for t in range(1, CONV_W):
        y = y + cw_ref[t:t + 1, :] * xbuf[5 + t:5 + t + c, :]
    last_rows = xbuf[c + 5:c + 8, :]
    xbuf[5:8, :] = last_rows
    qk = y * jax.nn.sigmoid(y)

    graw = g_ref[...] + bif_ref[...]
    xg = jnp.where(lane < NH_A, graw, jnp.where(lane < 2 * NH_A, _log_sigmoid(graw), 0.0))
    hp = lax.Precision.HIGHEST
    bc = jnp.dot(tril, xg, precision=hp, preferred_element_type=F32)
    dn0 = (((0,), (0,)), ((), ()))
    xt = lax.dot_general(xg, eye, dn0, precision=hp, preferred_element_type=F32)
    bt = lax.dot_general(xg, triu, dn0, precision=hp, preferred_element_type=F32)

    for h in range(NH_A):
        q = qk[:, h * DQK_A:(h + 1) * DQK_A]
        k = qk[:, QK_A + h * DQK_A:QK_A + (h + 1) * DQK_A] * (DQK_A ** -0.5)
        v = v_ref[:, h * DV_A:(h + 1) * DV_A]
        qb, kb, vb = q.astype(BF16), k.astype(BF16), v.astype(BF16)
        i_col = xg[:, h:h + 1]
        b_col = bc[:, NH_A + h:NH_A + h + 1]
        i_row = xt[h:h + 1, :]
        b_row = bt[NH_A + h:NH_A + h + 1, :]
        m_prev = m_s[0:1, h:h + 1]
        C = c_s[h]
        nvec = n_s[h:h + 1, :]

        dmat = jnp.where(causal, b_col - b_row + i_row, -jnp.inf)
        inter = b_col + m_prev
        m_t = jnp.maximum(inter, jnp.max(dmat, axis=1, keepdims=True))
        e = jnp.exp(dmat - m_t)
        s = lax.dot_general(qb, kb, (((1,), (1,)), ((), ())), preferred_element_type=F32) * e
        w_inter = jnp.exp(inter - m_t)
        num = (jnp.dot(s.astype(BF16), vb, preferred_element_type=F32)
               + w_inter * jnp.dot(qb, C.astype(BF16), preferred_element_type=F32))
        den = jnp.sum(s, axis=1, keepdims=True) + w_inter * jnp.sum(q * nvec, axis=1, keepdims=True)
        hh = num / jnp.maximum(jnp.abs(den), jnp.exp(-m_t))
        hh = hh * lax.rsqrt(jnp.mean(hh * hh, axis=1, keepdims=True) + EPS)
        hh = hh * gh_ref[:, h * DV_A:(h + 1) * DV_A]
        hh = jax.nn.sigmoid(o_ref[:, h * DV_A:(h + 1) * DV_A]) * hh
        h_ref[:, h * DV_A:(h + 1) * DV_A] = hh.astype(h_ref.dtype)

        b_end = b_col[c - 1:c, :]
        dec = b_end - b_col + i_col
        m_new = jnp.maximum(b_end + m_prev, jnp.max(dec, axis=0, keepdims=True))
        wk = jnp.exp(dec - m_new)
        w_old = jnp.exp(b_end + m_prev - m_new)
        kw = k * wk
        c_s[h] = w_old * C + lax.dot_general(kw.astype(BF16), vb, dn0, preferred_element_type=F32)
        n_s[h:h + 1, :] = w_old * nvec + jnp.sum(kw, axis=0, keepdims=True)
        m_s[0:1, h:h + 1] = m_new
    return last_rows


MLSTM_SEQS = 2


def mlstm(P, G, row0, N, L, b_if, conv_w, conv_b, g_head, conv_buf, C0, n0, m0):
    c = min(L, CHUNK)
    nc = L // c
    nb = MLSTM_SEQS if N % MLSTM_SEQS == 0 else 1
    base = row0 // c
    m0p = jnp.zeros((N, 1, LANES), F32).at[:, 0, :NH_A].set(m0)
    bif = jnp.zeros((1, LANES), F32).at[0, :2 * NH_A].set(b_if)

    def rows(width, col):
        return [pl.BlockSpec((c, width), lambda n, j, b=b: (base + (n * nb + b) * nc + j, col))
                for b in range(nb)]
    fix = lambda n, j: (0, 0)
    st4 = lambda n, j: (n, 0, 0, 0)
    st3 = lambda n, j: (n, 0, 0)
    outs = pl.pallas_call(
        functools.partial(_mlstm_kernel, c=c, nc=nc, nb=nb),
        out_shape=(jax.ShapeDtypeStruct((N, L, D_A), BF16),
                   jax.ShapeDtypeStruct((N, NH_A, DQK_A, DV_A), F32),
                   jax.ShapeDtypeStruct((N, NH_A, DQK_A), F32),
                   jax.ShapeDtypeStruct((N, 1, LANES), F32),
                   jax.ShapeDtypeStruct((N, CONV_W - 1, 2 * QK_A), F32)),
        grid=(N // nb, nc),
        in_specs=rows(2 * QK_A, 0) + rows(D_A, 1) + rows(D_A, 2) + rows(LANES, 0)
                 + [pl.BlockSpec((1, LANES), fix),
                    pl.BlockSpec((CONV_W, 2 * QK_A), fix),
                    pl.BlockSpec((1, 2 * QK_A), fix),
                    pl.BlockSpec((1, D_A), fix),
                    pl.BlockSpec((nb, CONV_W - 1, 2 * QK_A), st3),
                    pl.BlockSpec((nb, NH_A, DQK_A, DV_A), st4),
                    pl.BlockSpec((nb, NH_A, DQK_A), st3),
                    pl.BlockSpec((nb, 1, LANES), st3)],
        out_specs=(pl.BlockSpec((nb, c, D_A), lambda n, j: (n, j, 0)),
                   pl.BlockSpec((nb, NH_A, DQK_A, DV_A), st4),
                   pl.BlockSpec((nb, NH_A, DQK_A), st3),
                   pl.BlockSpec((nb, 1, LANES), st3),
                   pl.BlockSpec((nb, CONV_W - 1, 2 * QK_A), st3)),
        scratch_shapes=[pltpu.VMEM((nb, c + 8, 2 * QK_A), F32),
                        pltpu.VMEM((nb, NH_A, DQK_A, DV_A), F32),
                        pltpu.VMEM((nb, NH_A, DQK_A), F32),
                        pltpu.VMEM((nb, 1, LANES), F32)],
        compiler_params=_params(("parallel", "arbitrary")),
        name="mlstm",
    )(*([P] * (3 * nb) + [G] * nb), bif, conv_w, conv_b.reshape(1, -1), g_head.reshape(1, -1),
      conv_buf, C0, n0, m0p)
    h, C1, n1, m1, cv = outs
    return h.reshape(N * L, D_A), C1, n1, m1[:, 0, :NH_A], cv


STICK_UNDERFLOW = -110.0


def _later_matrix(tk):
    jr = lax.broadcasted_iota(jnp.int32, (tk, tk), 0)
    sc = lax.broadcasted_iota(jnp.int32, (tk, tk), 1)
    return (jr > sc).astype(BF16)


def _sb_blocks(q_bfs, k_blks, v_blks, Rs, later, mask):
    tq = q_bfs[0].shape[0]
    zs, l1mbs, parts = [], [], []
    for q_bf, k_blk in zip(q_bfs, k_blks):
        z = lax.dot_general(q_bf, k_blk.astype(BF16), (((1,), (1,)), ((), ())),
                            preferred_element_type=F32) * (DH_B ** -0.5)
        l1mb = -(jnp.maximum(z, 0.0) + jnp.log(1.0 + jnp.exp(-jnp.abs(z))))
        if mask is not None:
            l1mb = jnp.where(mask, l1mb, 0.0)
        hi = l1mb.astype(BF16)
        parts += [hi, (l1mb - hi.astype(F32)).astype(BF16)]
        zs.append(z)
        l1mbs.append(l1mb)
    suffix = jnp.dot(jnp.concatenate(parts, axis=0), later, preferred_element_type=F32)
    pvs, new_Rs = [], []
    for h, (z, l1mb) in enumerate(zip(zs, l1mbs)):
        rest = suffix[2 * h * tq:(2 * h + 1) * tq] + suffix[(2 * h + 1) * tq:(2 * h + 2) * tq] + Rs[h]
        a = jnp.exp(z + l1mb + rest)
        if mask is not None:
            a = jnp.where(mask, a, 0.0)
        pvs.append(jnp.dot(a.astype(BF16), v_blks[h].astype(BF16), preferred_element_type=F32))
        new_Rs.append(Rs[h] + jnp.sum(l1mb, axis=1, keepdims=True))
    return pvs, new_Rs


SB_HEADS = 2


def _sb_kernel(q_ref, k_ref, v_ref, o_ref, *, tq):
    qi = pl.program_id(2)
    rr = lax.broadcasted_iota(jnp.int32, (tq, tq), 0)
    cc = lax.broadcasted_iota(jnp.int32, (tq, tq), 1)
    later = _later_matrix(tq)
    cols = [slice(h * DH_B, (h + 1) * DH_B) for h in range(SB_HEADS)]
    q_bf = [q_ref[:, c].astype(BF16) for c in cols]
    def blocks(st, Rs, mask):
        return _sb_blocks(q_bf, [k_ref[pl.ds(st, tq), c] for c in cols], [v_ref[pl.ds(st, tq), c] for c in cols],
                          Rs, later, mask)

    accs, Rs = blocks(pl.multiple_of(qi * tq, tq), [jnp.zeros((tq, 1), F32)] * SB_HEADS, cc < rr)

    def rmax(Rs):
        return jnp.max(jnp.concatenate(Rs, axis=1))

    def cond(c):
        return jnp.logical_and(c[0] < qi, c[3] > STICK_UNDERFLOW)

    def body(carry):
        t, accs, Rs, _ = carry
        pvs, new_Rs = blocks(pl.multiple_of((qi - 1 - t) * tq, tq), list(Rs), None)
        new_accs = [acc + pv for acc, pv in zip(accs, pvs)]
        return t + 1, tuple(new_accs), tuple(new_Rs), rmax(new_Rs)

    _, accs, _, _ = lax.while_loop(cond, body, (jnp.int32(0), tuple(accs), tuple(Rs), rmax(Rs)))
    for h, c in enumerate(cols):
        o_ref[:, c] = accs[h].astype(o_ref.dtype)


QB_BLK, KB_BLK, VB_BLK = 3 * D_B // DH_B, 4 * D_B // DH_B, 5 * D_B // DH_B


def sb_attention(P, row0, N, L):
    tq = min(L, 256)
    nq = L // tq
    W = SB_HEADS * DH_B
    qb, kb, vb = QB_BLK // SB_HEADS, KB_BLK // SB_HEADS, VB_BLK // SB_HEADS
    return pl.pallas_call(
        functools.partial(_sb_kernel, tq=tq),
        out_shape=jax.ShapeDtypeStruct((N * L, D_B), BF16),
        grid=(N, NH_B // SB_HEADS, nq),
        in_specs=[pl.BlockSpec((tq, W), lambda n, h, i: (row0 // tq + n * nq + i, qb + h)),
                  pl.BlockSpec((L, W), lambda n, h, i: (row0 // L + n, kb + h)),
                  pl.BlockSpec((L, W), lambda n, h, i: (row0 // L + n, vb + h))],
        out_specs=pl.BlockSpec((tq, W), lambda n, h, i: (n * nq + i, h)),
        compiler_params=_params(("parallel", "parallel", "arbitrary")),
        name="sb_attention",
    )(P, P, P)


def _sb_past_kernel(q_ref, k_ref, v_ref, pk_hbm, pv_hbm, o_ref, kbuf, vbuf, ksem, vsem, *, L, tk, n_blk):
    n = pl.program_id(0)

    def copies(blk):
        st = blk * tk
        return ([pltpu.make_async_copy(pk_hbm.at[0, n, pl.ds(st, tk), h, :], kbuf.at[h], ksem)
                 for h in range(NH_B)]
                + [pltpu.make_async_copy(pv_hbm.at[0, n, pl.ds(st, tk), h, :], vbuf.at[h], vsem)
                   for h in range(NH_B)])

    rr = lax.broadcasted_iota(jnp.int32, (L, L), 0)
    cc = lax.broadcasted_iota(jnp.int32, (L, L), 1)
    later_new = _later_matrix(L)
    later = _later_matrix(tk)
    sls = [slice(h * DH_B, (h + 1) * DH_B) for h in range(NH_B)]
    q_bf = [q_ref[:, sl].astype(BF16) for sl in sls]
    accs, Rs = _sb_blocks(q_bf, [k_ref[:, sl] for sl in sls], [v_ref[:, sl] for sl in sls],
                          [jnp.zeros((L, 1), F32)] * NH_B, later_new, cc < rr)

    def rmax(Rs):
        return jnp.max(jnp.concatenate(Rs, axis=1))

    def cond(c):
        return jnp.logical_and(c[0] < n_blk, c[3] > STICK_UNDERFLOW)

    def body(c):
        t, accs, Rs, _ = c
        cps = copies(n_blk - 1 - t)
        for cp in cps:
            cp.start()
        for cp in cps:
            cp.wait()
        pvs, new_Rs = _sb_blocks(q_bf, [kbuf[h] for h in range(NH_B)], [vbuf[h] for h in range(NH_B)],
                                 list(Rs), later, None)
        new_accs = [acc + pv for acc, pv in zip(accs, pvs)]
        return t + 1, tuple(new_accs), tuple(new_Rs), rmax(new_Rs)

    _, accs, _, _ = lax.while_loop(cond, body, (jnp.int32(0), tuple(accs), tuple(Rs), rmax(Rs)))
    for p, sl in enumerate(sls):
        o_ref[:, sl] = accs[p].astype(o_ref.dtype)


def sb_attention_past(P, row0, N, L, past_k, past_v):
    Lp = past_k.shape[2]
    tk = min(Lp, 256)
    blk = lambda c: pl.BlockSpec((L, D_B), lambda n: (row0 // L + n, c))
    return pl.pallas_call(
        functools.partial(_sb_past_kernel, L=L, tk=tk, n_blk=Lp // tk),
        out_shape=jax.ShapeDtypeStruct((N * L, D_B), BF16),
        grid=(N,),
        in_specs=[blk(3), blk(4), blk(5),
                  pl.BlockSpec(memory_space=pl.ANY), pl.BlockSpec(memory_space=pl.ANY)],
        out_specs=pl.BlockSpec((L, D_B), lambda n: (n, 0)),
        scratch_shapes=[pltpu.VMEM((NH_B, tk, DH_B), F32), pltpu.VMEM((NH_B, tk, DH_B), F32),
                        pltpu.SemaphoreType.DMA(()), pltpu.SemaphoreType.DMA(())],
        compiler_params=_params(("arbitrary",)),
        name="sb_attention_past",
    )(P, P, P, past_k, past_v)


def _merge_kernel(h_ref, ha_ref, hb_ref, wga_ref, wgb_ref, wpa_ref, wpb_ref, o_ref):
    h = h_ref[...]
    ga = jnp.dot(h, wga_ref[...], preferred_element_type=F32)
    gb = jnp.dot(h, wgb_ref[...], preferred_element_type=F32)
    pa = jnp.dot(ha_ref[...], wpa_ref[...], preferred_element_type=F32)
    pb = jnp.dot(hb_ref[...], wpb_ref[...], preferred_element_type=F32)
    o_ref[...] = (jax.nn.sigmoid(ga) * pa + jax.nn.sigmoid(gb) * pb).astype(o_ref.dtype)


def merge(h, row0, ha, hb, wga, wgb, wpa, wpb):
    Tn = ha.shape[0]
    D = h.shape[1]
    tm = _pick(math.gcd(Tn, row0) if row0 else Tn, (1024, 512, 256, 128, 64, 32, 16))
    tn = _pick(D, (512, 256, 128))
    row = lambda i, j: (i, 0)
    colw = lambda i, j: (0, j)
    return pl.pallas_call(
        _merge_kernel,
        out_shape=jax.ShapeDtypeStruct((Tn, D), BF16),
        grid=(Tn // tm, D // tn),
        in_specs=[pl.BlockSpec((tm, D), lambda i, j: (row0 // tm + i, 0)),
                  pl.BlockSpec((tm, D_A), row), pl.BlockSpec((tm, D_B), row),
                  pl.BlockSpec((D, tn), colw), pl.BlockSpec((D, tn), colw),
                  pl.BlockSpec((D_A, tn), colw), pl.BlockSpec((D_B, tn), colw)],
        out_specs=pl.BlockSpec((tm, tn), lambda i, j: (i, j)),
        compiler_params=_params(("parallel", "arbitrary")),
        name="merge",
    )(h, ha, hb, wga, wgb, wpa, wpb)


def _wout_route_kernel(mgp_ref, mgs_ref, xp_ref, xs_ref, wo_ref, g_ref, wr_ref, br_ref, x1_ref, t_ref, r_ref,
                       cnt_ref, run_s, *, n_first):
    i = pl.program_id(0)

    @pl.when(i == 0)
    def _():
        run_s[...] = jnp.zeros_like(run_s)

    x = jnp.where(i < n_first, xp_ref[...], xs_ref[...])
    mg = jnp.where(i < n_first, mgp_ref[...], mgs_ref[...])
    x1 = x + jnp.dot(mg, wo_ref[...], preferred_element_type=F32)
    x1_ref[...] = x1
    t = _rms(x1, g_ref[...])
    _store_row_major(t_ref, t)
    lg = jnp.dot(t.astype(BF16), wr_ref[...], preferred_element_type=F32) + br_ref[...]
    tm = lg.shape[0]
    lane = lax.broadcasted_iota(jnp.int32, (tm, LANES), 1)
    ninf = -jnp.inf
    glm = jnp.where(lane < N_GROUPS, lg, ninf)
    gmax = jnp.max(glm, axis=1, keepdims=True)
    gsel = jnp.min(jnp.where(glm == gmax, lane, LANES), axis=1, keepdims=True)
    p_sel = 1.0 / jnp.sum(jnp.exp(glm - gmax), axis=1, keepdims=True)
    lo = N_GROUPS + E_PER_GROUP * gsel
    elm = jnp.where((lane >= lo) & (lane < lo + E_PER_GROUP), lg, ninf)
    m1 = jnp.max(elm, axis=1, keepdims=True)
    i1 = jnp.min(jnp.where(elm == m1, lane, LANES), axis=1, keepdims=True)
    elm2 = jnp.where(lane == i1, ninf, elm)
    m2 = jnp.max(elm2, axis=1, keepdims=True)
    i2 = jnp.min(jnp.where(elm2 == m2, lane, LANES), axis=1, keepdims=True)
    e2 = jnp.exp(m2 - m1)
    w1 = p_sel / (1.0 + e2)
    w2 = p_sel * e2 / (1.0 + e2)
    e1 = (lane == i1 - N_GROUPS).astype(F32)
    e2h = (lane == i2 - N_GROUPS).astype(F32)
    both = e1 + e2h
    rr = lax.broadcasted_iota(jnp.int32, (tm, tm), 0)
    cc = lax.broadcasted_iota(jnp.int32, (tm, tm), 1)
    before = jnp.dot((cc < rr).astype(BF16), both.astype(BF16), preferred_element_type=F32) + run_s[...]
    rank1 = jnp.sum(e1 * before, axis=1, keepdims=True)
    rank2 = jnp.sum(e2h * before, axis=1, keepdims=True)
    run_s[...] = run_s[...] + jnp.sum(both, axis=0, keepdims=True)
    cnt_ref[...] = run_s[...]
    r = jnp.where(lane == 0, (i1 - N_GROUPS).astype(F32),
        jnp.where(lane == 1, (i2 - N_GROUPS).astype(F32),
        jnp.where(lane == 2, w1, jnp.where(lane == 3, w2,
        jnp.where(lane == 4, rank1, jnp.where(lane == 5, rank2, 0.0))))))
    r_ref[...] = r


def wout_route(mg_p, mg_s, xp, xs, wo, g_ffn, wr, br, tm):
    (Tp, D), Ts = xp.shape, xs.shape[0]
    T = Tp + Ts
    row = lambda i: (i, 0)
    fix = lambda i: (0, 0)
    return pl.pallas_call(
        functools.partial(_wout_route_kernel, n_first=Tp // tm),
        out_shape=(jax.ShapeDtypeStruct((T, D), F32), jax.ShapeDtypeStruct((T * ROW_PITCH, LANES), F32),
                   jax.ShapeDtypeStruct((T, LANES), F32), jax.ShapeDtypeStruct((1, LANES), F32)),
        grid=(T // tm,),
        in_specs=_two_stream_specs(tm, D, Tp // tm) + _two_stream_specs(tm, D, Tp // tm)
                 + [pl.BlockSpec((D, D), fix), pl.BlockSpec((1, D), fix), pl.BlockSpec((D, LANES), fix),
                    pl.BlockSpec((1, LANES), fix)],
        out_specs=(pl.BlockSpec((tm, D), row), pl.BlockSpec((tm * ROW_PITCH, LANES), row),
                   pl.BlockSpec((tm, LANES), row),
                   pl.BlockSpec((1, LANES), fix)),
        scratch_shapes=[pltpu.VMEM((1, LANES), F32)],
        compiler_params=_params(("arbitrary",)),
        name="wout_route",
    )(mg_p, mg_s, xp, xs, wo, g_ffn.reshape(1, D), wr, br)


IDX_RING = 8


def _expert_kernel(te_ref, nu_ref, src_hbm, dst_hbm, t_hbm, wg_ref, wu_ref, wd_ref, y_hbm,
                   sidx, didx, xb0, xb1, yb0, yb1, wg_s, wu_s, wd_s, gsem, ssem, isem, *, dump0):
    i = pl.program_id(0)
    nu = nu_ref[0]
    TE = EXPERT_TILE
    xb, yb = (xb0, xb1), (yb0, yb1)

    def table_copies(tile, seq):
        base = (seq & (IDX_RING - 1)) * TE
        return (pltpu.make_async_copy(src_hbm.at[pl.ds(tile * TE, TE)], sidx.at[pl.ds(base, TE)], isem.at[0]),
                pltpu.make_async_copy(dst_hbm.at[pl.ds(tile * TE, TE)], didx.at[pl.ds(base, TE)], isem.at[1]))

    def gather(seq, par):
        base = (seq & (IDX_RING - 1)) * TE
        return [pltpu.make_async_copy(t_hbm.at[pl.ds(sidx[base + r], ROW_CHUNKS), :],
                                      xb[par].at[pl.ds(r * ROW_PITCH, ROW_CHUNKS), :],
                                      gsem.at[par]) for r in range(TE)]

    def scatter(seq, par):
        base = (seq & (IDX_RING - 1)) * TE
        return [pltpu.make_async_copy(yb[par].at[pl.ds(r * ROW_PITCH, ROW_PITCH), :],
                                      y_hbm.at[pl.ds(didx[base + r], ROW_PITCH), :],
                                      ssem.at[par]) for r in range(TE)]

    def start(cps):
        rows = len(cps) == TE
        for n, cp in enumerate(cps):
            cp.start(priority=n % 2 if rows else 0)

    def wait(cps):
        for cp in cps:
            cp.wait()

    last = nu - 1

    @pl.when(i == 0)
    def _():
        first = table_copies(0, 0)
        start(first)
        wait(first)
        start(table_copies(jnp.minimum(1, last), 1))
        start(gather(0, 0))
        yb0[...] = jnp.zeros_like(yb0)
        yb1[...] = jnp.zeros_like(yb1)
        for r in range(TE):
            didx[(IDX_RING - 1) * TE + r] = (dump0 + TE + r) * ROW_PITCH
        fill = pltpu.make_async_copy(yb1, y_hbm.at[pl.ds(dump0 * ROW_PITCH, TE * ROW_PITCH), :], ssem.at[0])
        fill.start()
        fill.wait()

    @pl.when((i < nu) & ((i == 0) | (te_ref[i] != te_ref[jnp.maximum(i - 1, 0)])))
    def _():
        wg_s[...] = wg_ref[...].astype(BF16)
        wu_s[...] = wu_ref[...].astype(BF16)
        wd_s[...] = wd_ref[...].astype(BF16)

    def step(par):
        wait(table_copies(jnp.minimum(i + 1, last), i + 1))
        start(table_copies(jnp.minimum(i + 2, last), i + 2))

        @pl.when(i >= 1)
        def _():
            wait(scatter(i - 2, par))

        wait(gather(i, par))
        start(scatter(i - 1, 1 - par))
        start(gather(i + 1, 1 - par))
        x = _load_row_major(xb[par], TE).astype(BF16)
        a = jnp.dot(x, wg_s[...], preferred_element_type=F32)
        u = jnp.dot(x, wu_s[...], preferred_element_type=F32)
        hid = (a * jax.nn.sigmoid(a) * u).astype(BF16)
        y = jnp.dot(hid, wd_s[...], preferred_element_type=F32)
        for j in range(ROW_CHUNKS):
            yb[par][pl.ds(j, TE, stride=ROW_PITCH), :] = y[:, j * LANES:(j + 1) * LANES]

        @pl.when(i == last)
        def _():
            start(scatter(i, par))
            wait(scatter(i - 1, 1 - par))
            wait(scatter(i, par))
            wait(gather(i + 1, 1 - par))
            wait(table_copies(last, i + 2))

    for par in (0, 1):
        pl.when((i < nu) & (i % 2 == par))(functools.partial(step, par))


def experts(t, slot_src, slot_dst, tile_expert, n_used, w_gate, w_up, w_down):
    T = t.shape[0] // ROW_PITCH
    D, F = w_gate.shape[-2:]
    nt = slot_src.shape[0] // EXPERT_TILE
    wmap = lambda i, te, nu: (te[i], 0, 0)
    any_spec = pl.BlockSpec(memory_space=pl.ANY)
    tile_buf = pltpu.VMEM((EXPERT_TILE * ROW_PITCH, LANES), F32)
    grid_spec = pltpu.PrefetchScalarGridSpec(
        num_scalar_prefetch=2,
        grid=(nt,),
        in_specs=[any_spec, any_spec, any_spec,
                  pl.BlockSpec((None, D, F), wmap), pl.BlockSpec((None, D, F), wmap),
                  pl.BlockSpec((None, F, D), wmap)],
        out_specs=any_spec,
        scratch_shapes=[pltpu.SMEM((IDX_RING * EXPERT_TILE,), jnp.int32),
                        pltpu.SMEM((IDX_RING * EXPERT_TILE,), jnp.int32),
                        tile_buf, tile_buf, tile_buf, tile_buf,
                        pltpu.VMEM((D, F), BF16), pltpu.VMEM((D, F), BF16), pltpu.VMEM((F, D), BF16),
                        pltpu.SemaphoreType.DMA((2,)), pltpu.SemaphoreType.DMA((2,)),
                        pltpu.SemaphoreType.DMA((2,))],
    )
    return pl.pallas_call(
        functools.partial(_expert_kernel, dump0=2 * T),
        out_shape=jax.ShapeDtypeStruct(((2 * T + 2 * EXPERT_TILE) * ROW_PITCH, LANES), F32),
        grid_spec=grid_spec,
        compiler_params=_params(("arbitrary",)),
        name="moe_experts",
    )(tile_expert, n_used, slot_src, slot_dst, t, w_gate.reshape(N_EXPERTS, D, F),
      w_up.reshape(N_EXPERTS, D, F), w_down.reshape(N_EXPERTS, F, D))


def _combine_kernel(x1_ref, y0_ref, y1_ref, r_ref, pp_ref, ps_ref, wple_ref, wpg_ref, gple_ref, gfin_ref,
                    yp_ref, ys_ref, *, n_first):
    i = pl.program_id(0)
    p = jnp.where(i < n_first, pp_ref[...], ps_ref[...])
    r = r_ref[...]
    tm = r.shape[0]
    x2 = x1_ref[...] + r[:, 2:3] * _load_row_major(y0_ref, tm) + r[:, 3:4] * _load_row_major(y1_ref, tm)
    hp = _rms(x2, gple_ref[...]).astype(BF16)
    gate = jax.nn.sigmoid(jnp.dot(hp, wpg_ref[...], preferred_element_type=F32))
    emb = jnp.dot(p.astype(BF16), wple_ref[...], preferred_element_type=F32)
    y = _rms(x2 + emb * gate, gfin_ref[...])

    @pl.when(i < n_first)
    def _():
        yp_ref[...] = y

    @pl.when(i >= n_first)
    def _():
        ys_ref[...] = y


def combine(y_flat, x1, route, p_p, p_s, w_ple, w_ple_gate, g_ple, g_final, tm):
    T, D = x1.shape
    row = lambda i: (i, 0)
    row1 = lambda i: (i + T // tm, 0)
    fix = lambda i: (0, 0)
    Tp = p_p.shape[0]
    n_first = Tp // tm
    return pl.pallas_call(
        functools.partial(_combine_kernel, n_first=n_first),
        out_shape=(jax.ShapeDtypeStruct((Tp, D), F32), jax.ShapeDtypeStruct((T - Tp, D), F32)),
        grid=(T // tm,),
        in_specs=[pl.BlockSpec((tm, D), row), pl.BlockSpec((tm * ROW_PITCH, LANES), row),
                  pl.BlockSpec((tm * ROW_PITCH, LANES), row1),
                  pl.BlockSpec((tm, LANES), row)] + _two_stream_specs(tm, p_p.shape[1], n_first)
                 + [pl.BlockSpec(w_ple.shape, fix), pl.BlockSpec((D, D), fix),
                  pl.BlockSpec((1, D), fix), pl.BlockSpec((1, D), fix)],
        out_specs=tuple(_two_stream_specs(tm, D, n_first)),
        compiler_params=_params(("arbitrary",)),
        name="moe_combine_ple",
    )(x1, y_flat, y_flat, route, p_p, p_s, w_ple, w_ple_gate, g_ple.reshape(1, D), g_final.reshape(1, D))


def _routing_tables(route, counts, n_tiles):
    ids = route[:, 0:2].astype(jnp.int32).reshape(-1)
    rank = route[:, 4:6].astype(jnp.int32).reshape(-1)
    counts = counts.astype(jnp.int32)
    tiles = (counts + EXPERT_TILE - 1) // EXPERT_TILE
    tile_end = jnp.cumsum(tiles)
    tile_start = tile_end - tiles
    onehot = ids[:, None] == jnp.arange(N_EXPERTS, dtype=jnp.int32)[None, :]
    pos = jnp.sum(jnp.where(onehot, tile_start[None, :], 0), axis=1) * EXPERT_TILE + rank
    slot_assign = jnp.full((n_tiles * EXPERT_TILE,), -1, jnp.int32).at[pos].set(
        jnp.arange(ids.shape[0], dtype=jnp.int32), unique_indices=True)
    n_assign = ids.shape[0]
    slot = jnp.arange(n_tiles * EXPERT_TILE, dtype=jnp.int32)
    token = jnp.maximum(slot_assign, 0) // 2
    slot_src = token * ROW_PITCH
    dump = n_assign + ((slot // EXPERT_TILE) % 2) * EXPERT_TILE + slot % EXPERT_TILE
    slot_dst = jnp.where(slot_assign >= 0, (slot_assign % 2) * (n_assign // 2) + token, dump) * ROW_PITCH
    n_used = tile_end[-1]
    tidx = jnp.minimum(jnp.arange(n_tiles, dtype=jnp.int32), n_used - 1)
    tile_expert = jnp.sum((tidx[:, None] >= tile_end[None, :]).astype(jnp.int32), axis=1)
    return slot_src, slot_dst, tile_expert.astype(jnp.int32), n_used.reshape(1).astype(jnp.int32)


def kernel(x_prompt, x_sample, cache_k, cache_v, state_conv, state_C, state_n, state_m, p_prompt, p_sample,
           g_mix, w_in, b_if, conv_w, conv_b, g_head_a, w_proj_a, w_proj_b, w_out, g_ffn, w_route_g,
           b_route_g, w_route_e, b_route_e, w_exp_gate, w_exp_up, w_exp_down, g_ple, w_ple, w_ple_gate,
           g_final):
    assert w_in.shape[0] == 1, "single layer"
    B, S, D = x_prompt.shape
    DB, DS, _ = x_sample.shape
    Tp, Ts = B * S, DB * DS
    T = Tp + Ts
    xp, xs = x_prompt.reshape(Tp, D), x_sample.reshape(Ts, D)
    tm = _pick(math.gcd(Tp, Ts), (256, 128, 64, 32, 16, 8))

    wi = w_in[0]
    g0 = 2 * QK_A + D_A
    m0 = g0 + 2 * NH_A
    e0 = m0 + D_A + 3 * D_B
    w_main = jnp.concatenate([wi[:, :g0], wi[:, m0:e0]], axis=1).astype(BF16)
    w_gates = jnp.zeros((D, LANES), F32).at[:, :2 * NH_A].set(wi[:, g0:m0]).astype(BF16)
    wga = wi[:, e0:e0 + D].astype(BF16)
    wgb = wi[:, e0 + D:].astype(BF16)

    h = rms_cast(xp, xs, g_mix[0], tm)
    P = matmul(h, w_main, "proj_main")
    G = matmul(h, w_gates, "proj_gates")

    zeros = functools.partial(jnp.zeros, dtype=F32)
    ha_p, C_p, n_p, m_p, cv_p = mlstm(P, G, 0, B, S, b_if[0], conv_w[0], conv_b[0], g_head_a[0],
                                      zeros((B, CONV_W - 1, 2 * QK_A)), zeros((B, NH_A, DQK_A, DV_A)),
                                      zeros((B, NH_A, DQK_A)), zeros((B, NH_A)))
    ha_s, C_s, n_s, m_s, cv_s = mlstm(P, G, Tp, DB, DS, b_if[0], conv_w[0], conv_b[0], g_head_a[0],
                                      state_conv[0], state_C[0], state_n[0], state_m[0])
    hb_p = sb_attention(P, 0, B, S)
    hb_s = sb_attention_past(P, Tp, DB, DS, cache_k, cache_v)
    wpa, wpb = w_proj_a[0].astype(BF16), w_proj_b[0].astype(BF16)
    mg_p = merge(h, 0, ha_p, hb_p, wga, wgb, wpa, wpb)
    mg_s = merge(h, Tp, ha_s, hb_s, wga, wgb, wpa, wpb)

    wr = jnp.zeros((D, LANES), F32)
    wr = wr.at[:, :N_GROUPS].set(w_route_g[0])
    wr = wr.at[:, N_GROUPS:N_GROUPS + N_EXPERTS].set(
        jnp.transpose(w_route_e[0], (1, 0, 2)).reshape(D, N_EXPERTS))
    br = jnp.zeros((1, LANES), F32)
    br = br.at[0, :N_GROUPS].set(b_route_g[0])
    br = br.at[0, N_GROUPS:N_GROUPS + N_EXPERTS].set(b_route_e[0].reshape(-1))
    x1, t, route, counts = wout_route(mg_p, mg_s, xp, xs, w_out[0].astype(BF16), g_ffn[0], wr.astype(BF16), br, tm)

    n_tiles = (2 * T + N_EXPERTS * (EXPERT_TILE - 1)) // EXPERT_TILE + 1
    slot_src, slot_dst, tile_expert, n_used = _routing_tables(route, counts[0, :N_EXPERTS], n_tiles)
    y_flat = experts(t, slot_src, slot_dst, tile_expert, n_used, w_exp_gate[0], w_exp_up[0], w_exp_down[0])
    y_p, y_s = combine(y_flat, x1, route, p_prompt[0].reshape(Tp, -1), p_sample[0].reshape(Ts, -1),
                       w_ple[0].astype(BF16), w_ple_gate[0].astype(BF16), g_ple[0], g_final, tm)

    kq = P[:, 4 * D_B:5 * D_B]
    vq = P[:, 5 * D_B:6 * D_B]
    return (y_p.reshape(B, S, D), y_s.reshape(DB, DS, D),
            kq[:Tp].reshape(1, B, S, NH_B, DH_B), vq[:Tp].reshape(1, B, S, NH_B, DH_B),
            cv_p[None], C_p[None], n_p[None], m_p[None],
            kq[Tp:].reshape(1, DB, DS, NH_B, DH_B), vq[Tp:].reshape(1, DB, DS, NH_B, DH_B),
            cv_s[None], C_s[None], n_s[None], m_s[None])
```

```python
import functools
import math

import jax
import jax.numpy as jnp
from jax import lax
from jax.experimental import pallas as pl
from jax.experimental.pallas import tpu as pltpu

F32 = jnp.float32
BF16 = jnp.bfloat16
EPS = 1e-6

CHUNK = 128
NH_A, DQK_A, DV_A = 4, 128, 256
QK_A, D_A = NH_A * DQK_A, NH_A * DV_A
CONV_W = 4
NH_B, DH_B = 8, 128
D_B = NH_B * DH_B
N_GROUPS, E_PER_GROUP = 4, 8
N_EXPERTS = N_GROUPS * E_PER_GROUP
LANES = 128
EXPERT_TILE = 256
VMEM_LIMIT = 56 * 1024 * 1024


def _pick(n, cands):
    for c in cands:
        if n % c == 0:
            return c
    raise ValueError(f"no tile for {n} in {cands}")


def _params(sem):
    return pltpu.CompilerParams(dimension_semantics=sem, vmem_limit_bytes=VMEM_LIMIT)


def _rms(x, g):
    return x * lax.rsqrt(jnp.mean(x * x, axis=-1, keepdims=True) + EPS) * g


ROW_CHUNKS = 2048 // LANES
ROW_PITCH = 20


def _store_row_major(ref, x):
    rows = x.shape[0]
    for j in range(ROW_PITCH):
        piece = x[:, j * LANES:(j + 1) * LANES] if j < ROW_CHUNKS else jnp.zeros((rows, LANES), x.dtype)
        ref[pl.ds(j, rows, stride=ROW_PITCH), :] = piece


def _load_row_major(ref, rows):
    return jnp.concatenate([ref[pl.ds(j, rows, stride=ROW_PITCH), :] for j in range(ROW_CHUNKS)], axis=1)


def _log_sigmoid(x):
    return jnp.minimum(x, 0.0) - jnp.log(1.0 + jnp.exp(-jnp.abs(x)))


def _two_stream_specs(tm, D, n_first):
    return [pl.BlockSpec((tm, D), lambda i: (jnp.minimum(i, n_first - 1), 0)),
            pl.BlockSpec((tm, D), lambda i: (jnp.maximum(i - n_first, 0), 0))]


def _rms_kernel(xp_ref, xs_ref, g_ref, o_ref, *, n_first):
    x = jnp.where(pl.program_id(0) < n_first, xp_ref[...], xs_ref[...])
    o_ref[...] = _rms(x, g_ref[...]).astype(o_ref.dtype)


def rms_cast(xp, xs, g, tm):
    (Tp, D), Ts = xp.shape, xs.shape[0]
    T = Tp + Ts
    return pl.pallas_call(
        functools.partial(_rms_kernel, n_first=Tp // tm),
        out_shape=jax.ShapeDtypeStruct((T, D), BF16),
        grid=(T // tm,),
        in_specs=_two_stream_specs(tm, D, Tp // tm) + [pl.BlockSpec((1, D), lambda i: (0, 0))],
        out_specs=pl.BlockSpec((tm, D), lambda i: (i, 0)),
        compiler_params=_params(("arbitrary",)),
        name="rms_cast",
    )(xp, xs, g.reshape(1, D))


def _mm_kernel(a_ref, w_ref, o_ref):
    o_ref[...] = jnp.dot(a_ref[...], w_ref[...], preferred_element_type=F32)


def matmul(a, w, name):
    T, K = a.shape
    N = w.shape[1]
    tm = _pick(T, (1280, 640, 512, 256, 128, 64, 32, 16))
    tn = _pick(N, (512, 256, 128))
    return pl.pallas_call(
        _mm_kernel,
        out_shape=jax.ShapeDtypeStruct((T, N), F32),
        grid=(T // tm, N // tn),
        in_specs=[pl.BlockSpec((tm, K), lambda i, j: (i, 0)),
                  pl.BlockSpec((K, tn), lambda i, j: (0, j))],
        out_specs=pl.BlockSpec((tm, tn), lambda i, j: (i, j)),
        compiler_params=_params(("parallel", "arbitrary")),
        name=name,
    )(a, w)


def _mlstm_kernel(*refs, c, nc, nb):
    qk_refs, v_refs, o_refs, g_refs = (refs[k * nb:(k + 1) * nb] for k in range(4))
    (bif_ref, cw_ref, cb_ref, gh_ref, cbuf_ref, c0_ref, n0_ref, m0_ref,
     h_ref, cout_ref, nout_ref, mout_ref, convout_ref, xbuf, c_s, n_s, m_s) = refs[4 * nb:]
    j = pl.program_id(1)

    @pl.when(j == 0)
    def _():
        c_s[...] = c0_ref[...]
        n_s[...] = n0_ref[...]
        m_s[...] = m0_ref[...]
        xbuf[:, 5:8, :] = cbuf_ref[...]

    row = lax.broadcasted_iota(jnp.int32, (c, c), 0)
    col = lax.broadcasted_iota(jnp.int32, (c, c), 1)
    consts = dict(lane=lax.broadcasted_iota(jnp.int32, (c, LANES), 1), tril=(col <= row).astype(F32),
                  eye=(col == row).astype(F32), triu=(row <= col).astype(F32), causal=col <= row)
    last_rows = [
        _mlstm_chunk(qk_refs[b], v_refs[b], o_refs[b], g_refs[b], bif_ref, cw_ref, cb_ref, gh_ref,
                     h_ref.at[b], xbuf.at[b], c_s.at[b], n_s.at[b], m_s.at[b], consts, c)
        for b in range(nb)]

    @pl.when(j == nc - 1)
    def _():
        cout_ref[...] = c_s[...]
        nout_ref[...] = n_s[...]
        mout_ref[...] = m_s[...]
        for b in range(nb):
            convout_ref[b] = last_rows[b]


def _mlstm_chunk(qk_ref, v_ref, o_ref, g_ref, bif_ref, cw_ref, cb_ref, gh_ref, h_ref, xbuf, c_s, n_s, m_s,
                 consts, c):
    lane, tril, eye, triu, causal = (consts[k] for k in ("lane", "tril", "eye", "triu", "causal"))
    xbuf[8:8 + c, :] = qk_ref[...]
    y = cb_ref[...] + cw_ref[0:1, :] * xbuf[5:5 + c, :]
    for t in range(1, CONV_W):
        y = y + cw_ref[t:t + 1, :] * xbuf[5 + t:5 + t + c, :]
    last_rows = xbuf[c + 5:c + 8, :]
    xbuf[5:8, :] = last_rows
    qk = y * jax.nn.sigmoid(y)

    graw = g_ref[...] + bif_ref[...]
    xg = jnp.where(lane < NH_A, graw, jnp.where(lane < 2 * NH_A, _log_sigmoid(graw), 0.0))
    hp = lax.Precision.HIGHEST
    bc = jnp.dot(tril, xg, precision=hp, preferred_element_type=F32)
    dn0 = (((0,), (0,)), ((), ()))
    xt = lax.dot_general(xg, eye, dn0, precision=hp, preferred_element_type=F32)
    bt = lax.dot_general(xg, triu, dn0, precision=hp, preferred_element_type=F32)

    for h in range(NH_A):
        q = qk[:, h * DQK_A:(h + 1) * DQK_A]
        k = qk[:, QK_A + h * DQK_A:QK_A + (h + 1) * DQK_A] * (DQK_A ** -0.5)
        v = v_ref[:, h * DV_A:(h + 1) * DV_A]
        qb, kb, vb = q.astype(BF16), k.astype(BF16), v.astype(BF16)
        i_col = xg[:, h:h + 1]
        b_col = bc[:, NH_A + h:NH_A + h + 1]
        i_row = xt[h:h + 1, :]
        b_row = bt[NH_A + h:NH_A + h + 1, :]
        m_prev = m_s[0:1, h:h + 1]
        C = c_s[h]
        nvec = n_s[h:h + 1, :]

        dmat = jnp.where(causal, b_col - b_row + i_row, -jnp.inf)
        inter = b_col + m_prev
        m_t = jnp.maximum(inter, jnp.max(dmat, axis=1, keepdims=True))
        e = jnp.exp(dmat - m_t)
        s = lax.dot_general(qb, kb, (((1,), (1,)), ((), ())), preferred_element_type=F32) * e
        w_inter = jnp.exp(inter - m_t)
        num = (jnp.dot(s.astype(BF16), vb, preferred_element_type=F32)
               + w_inter * jnp.dot(qb, C.astype(BF16), preferred_element_type=F32))
        den = jnp.sum(s, axis=1, keepdims=True) + w_inter * jnp.sum(q * nvec, axis=1, keepdims=True)
        hh = num / jnp.maximum(jnp.abs(den), jnp.exp(-m_t))
        hh = hh * lax.rsqrt(jnp.mean(hh * hh, axis=1, keepdims=True) + EPS)
        hh = hh * gh_ref[:, h * DV_A:(h + 1) * DV_A]
        hh = jax.nn.sigmoid(o_ref[:, h * DV_A:(h + 1) * DV_A]) * hh
        h_ref[:, h * DV_A:(h + 1) * DV_A] = hh.astype(h_ref.dtype)

        b_end = b_col[c - 1:c, :]
        dec = b_end - b_col + i_col
        m_new = jnp.maximum(b_end + m_prev, jnp.max(dec, axis=0, keepdims=True))
        wk = jnp.exp(dec - m_new)
        w_old = jnp.exp(b_end + m_prev - m_new)
        kw = k * wk
        c_s[h] = w_old * C + lax.dot_general(kw.astype(BF16), vb, dn0, preferred_element_type=F32)
        n_s[h:h + 1, :] = w_old * nvec + jnp.sum(kw, axis=0, keepdims=True)
        m_s[0:1, h:h + 1] = m_new
    return last_rows


MLSTM_SEQS = 2


def mlstm(P, G, row0, N, L, b_if, conv_w, conv_b, g_head, conv_buf, C0, n0, m0):
    c = min(L, CHUNK)
    nc = L // c
    nb = MLSTM_SEQS if N % MLSTM_SEQS == 0 else 1
    base = row0 // c
    m0p = jnp.zeros((N, 1, LANES), F32).at[:, 0, :NH_A].set(m0)
    bif = jnp.zeros((1, LANES), F32).at[0, :2 * NH_A].set(b_if)

    def rows(width, col):
        return [pl.BlockSpec((c, width), lambda n, j, b=b: (base + (n * nb + b) * nc + j, col))
                for b in range(nb)]
    fix = lambda n, j: (0, 0)
    st4 = lambda n, j: (n, 0, 0, 0)
    st3 = lambda n, j: (n, 0, 0)
    outs = pl.pallas_call(
        functools.partial(_mlstm_kernel, c=c, nc=nc, nb=nb),
        out_shape=(jax.ShapeDtypeStruct((N, L, D_A), BF16),
                   jax.ShapeDtypeStruct((N, NH_A, DQK_A, DV_A), F32),
                   jax.ShapeDtypeStruct((N, NH_A, DQK_A), F32),
                   jax.ShapeDtypeStruct((N, 1, LANES), F32),
                   jax.ShapeDtypeStruct((N, CONV_W - 1, 2 * QK_A), F32)),
        grid=(N // nb, nc),
        in_specs=rows(2 * QK_A, 0) + rows(D_A, 1) + rows(D_A, 2) + rows(LANES, 0)
                 + [pl.BlockSpec((1, LANES), fix),
                    pl.BlockSpec((CONV_W, 2 * QK_A), fix),
                    pl.BlockSpec((1, 2 * QK_A), fix),
                    pl.BlockSpec((1, D_A), fix),
                    pl.BlockSpec((nb, CONV_W - 1, 2 * QK_A), st3),
                    pl.BlockSpec((nb, NH_A, DQK_A, DV_A), st4),
                    pl.BlockSpec((nb, NH_A, DQK_A), st3),
                    pl.BlockSpec((nb, 1, LANES), st3)],
        out_specs=(pl.BlockSpec((nb, c, D_A), lambda n, j: (n, j, 0)),
                   pl.BlockSpec((nb, NH_A, DQK_A, DV_A), st4),
                   pl.BlockSpec((nb, NH_A, DQK_A), st3),
                   pl.BlockSpec((nb, 1, LANES), st3),
                   pl.BlockSpec((nb, CONV_W - 1, 2 * QK_A), st3)),
        scratch_shapes=[pltpu.VMEM((nb, c + 8, 2 * QK_A), F32),
                        pltpu.VMEM((nb, NH_A, DQK_A, DV_A), F32),
                        pltpu.VMEM((nb, NH_A, DQK_A), F32),
                        pltpu.VMEM((nb, 1, LANES), F32)],
        compiler_params=_params(("parallel", "arbitrary")),
        name="mlstm",
    )(*([P] * (3 * nb) + [G] * nb), bif, conv_w, conv_b.reshape(1, -1), g_head.reshape(1, -1),
      conv_buf, C0, n0, m0p)
    h, C1, n1, m1, cv = outs
    return h.reshape(N * L, D_A), C1, n1, m1[:, 0, :NH_A], cv


STICK_UNDERFLOW = -110.0


def _later_matrix(tk):
    jr = lax.broadcasted_iota(jnp.int32, (tk, tk), 0)
    sc = lax.broadcasted_iota(jnp.int32, (tk, tk), 1)
    return (jr > sc).astype(BF16)


def _sb_blocks(q_bfs, k_blks, v_blks, Rs, later, mask):
    tq = q_bfs[0].shape[0]
    zs, l1mbs, parts = [], [], []
    for q_bf, k_blk in zip(q_bfs, k_blks):
        z = lax.dot_general(q_bf, k_blk.astype(BF16), (((1,), (1,)), ((), ())),
                            preferred_element_type=F32) * (DH_B ** -0.5)
        l1mb = -(jnp.maximum(z, 0.0) + jnp.log(1.0 + jnp.exp(-jnp.abs(z))))
        if mask is not None:
            l1mb = jnp.where(mask, l1mb, 0.0)
        hi = l1mb.astype(BF16)
        parts += [hi, (l1mb - hi.astype(F32)).astype(BF16)]
        zs.append(z)
        l1mbs.append(l1mb)
    suffix = jnp.dot(jnp.concatenate(parts, axis=0), later, preferred_element_type=F32)
    pvs, new_Rs = [], []
    for h, (z, l1mb) in enumerate(zip(zs, l1mbs)):
        rest = suffix[2 * h * tq:(2 * h + 1) * tq] + suffix[(2 * h + 1) * tq:(2 * h + 2) * tq] + Rs[h]
        a = jnp.exp(z + l1mb + rest)
        if mask is not None:
            a = jnp.where(mask, a, 0.0)
        pvs.append(jnp.dot(a.astype(BF16), v_blks[h].astype(BF16), preferred_element_type=F32))
        new_Rs.append(Rs[h] + jnp.sum(l1mb, axis=1, keepdims=True))
    return pvs, new_Rs


SB_HEADS = 4


def _sb_kernel(q_ref, k_ref, v_ref, o_ref, *, tq):
    qi = pl.program_id(2)
    rr = lax.broadcasted_iota(jnp.int32, (tq, tq), 0)
    cc = lax.broadcasted_iota(jnp.int32, (tq, tq), 1)
    later = _later_matrix(tq)
    cols = [slice(h * DH_B, (h + 1) * DH_B) for h in range(SB_HEADS)]
    q_bf = [q_ref[:, c].astype(BF16) for c in cols]
    def blocks(st, Rs, mask):
        return _sb_blocks(q_bf, [k_ref[pl.ds(st, tq), c] for c in cols], [v_ref[pl.ds(st, tq), c] for c in cols],
                          Rs, later, mask)

    accs, Rs = blocks(pl.multiple_of(qi * tq, tq), [jnp.zeros((tq, 1), F32)] * SB_HEADS, cc < rr)

    def rmax(Rs):
        return jnp.max(jnp.concatenate(Rs, axis=1))

    def cond(c):
        return jnp.logical_and(c[0] < qi, c[3] > STICK_UNDERFLOW)

    def body(carry):
        t, accs, Rs, _ = carry
        pvs, new_Rs = blocks(pl.multiple_of((qi - 1 - t) * tq, tq), list(Rs), None)
        new_accs = [acc + pv for acc, pv in zip(accs, pvs)]
        return t + 1, tuple(new_accs), tuple(new_Rs), rmax(new_Rs)

    _, accs, _, _ = lax.while_loop(cond, body, (jnp.int32(0), tuple(accs), tuple(Rs), rmax(Rs)))
    for h, c in enumerate(cols):
        o_ref[:, c] = accs[h].astype(o_ref.dtype)


QB_BLK, KB_BLK, VB_BLK = 3 * D_B // DH_B, 4 * D_B // DH_B, 5 * D_B // DH_B


def sb_attention(P, row0, N, L):
    tq = min(L, 256)
    nq = L // tq
    W = SB_HEADS * DH_B
    qb, kb, vb = QB_BLK // SB_HEADS, KB_BLK // SB_HEADS, VB_BLK // SB_HEADS
    return pl.pallas_call(
        functools.partial(_sb_kernel, tq=tq),
        out_shape=jax.ShapeDtypeStruct((N * L, D_B), BF16),
        grid=(N, NH_B // SB_HEADS, nq),
        in_specs=[pl.BlockSpec((tq, W), lambda n, h, i: (row0 // tq + n * nq + i, qb + h)),
                  pl.BlockSpec((L, W), lambda n, h, i: (row0 // L + n, kb + h), pipeline_mode=pl.Buffered(1)),
                  pl.BlockSpec((L, W), lambda n, h, i: (row0 // L + n, vb + h), pipeline_mode=pl.Buffered(1))],
        out_specs=pl.BlockSpec((tq, W), lambda n, h, i: (n * nq + i, h)),
        compiler_params=_params(("parallel", "parallel", "arbitrary")),
        name="sb_attention",
    )(P, P, P)


def _sb_past_kernel(q_ref, k_ref, v_ref, pk_hbm, pv_hbm, o_ref, kbuf, vbuf, ksem, vsem, *, L, tk, n_blk):
    n = pl.program_id(0)

    def copies(blk):
        st = blk * tk
        return ([pltpu.make_async_copy(pk_hbm.at[0, n, pl.ds(st, tk), h, :], kbuf.at[h], ksem)
                 for h in range(NH_B)]
                + [pltpu.make_async_copy(pv_hbm.at[0, n, pl.ds(st, tk), h, :], vbuf.at[h], vsem)
                   for h in range(NH_B)])

    rr = lax.broadcasted_iota(jnp.int32, (L, L), 0)
    cc = lax.broadcasted_iota(jnp.int32, (L, L), 1)
    later_new = _later_matrix(L)
    later = _later_matrix(tk)
    sls = [slice(h * DH_B, (h + 1) * DH_B) for h in range(NH_B)]
    q_bf = [q_ref[:, sl].astype(BF16) for sl in sls]
    accs, Rs = _sb_blocks(q_bf, [k_ref[:, sl] for sl in sls], [v_ref[:, sl] for sl in sls],
                          [jnp.zeros((L, 1), F32)] * NH_B, later_new, cc < rr)

    def rmax(Rs):
        return jnp.max(jnp.concatenate(Rs, axis=1))

    def cond(c):
        return jnp.logical_and(c[0] < n_blk, c[3] > STICK_UNDERFLOW)

    def body(c):
        t, accs, Rs, _ = c
        cps = copies(n_blk - 1 - t)
        for cp in cps:
            cp.start()
        for cp in cps:
            cp.wait()
        pvs, new_Rs = _sb_blocks(q_bf, [kbuf[h] for h in range(NH_B)], [vbuf[h] for h in range(NH_B)],
                                 list(Rs), later, None)
        new_accs = [acc + pv for acc, pv in zip(accs, pvs)]
        return t + 1, tuple(new_accs), tuple(new_Rs), rmax(new_Rs)

    _, accs, _, _ = lax.while_loop(cond, body, (jnp.int32(0), tuple(accs), tuple(Rs), rmax(Rs)))
    for p, sl in enumerate(sls):
        o_ref[:, sl] = accs[p].astype(o_ref.dtype)


def sb_attention_past(P, row0, N, L, past_k, past_v):
    Lp = past_k.shape[2]
    tk = min(Lp, 256)
    blk = lambda c: pl.BlockSpec((L, D_B), lambda n: (row0 // L + n, c))
    return pl.pallas_call(
        functools.partial(_sb_past_kernel, L=L, tk=tk, n_blk=Lp // tk),
        out_shape=jax.ShapeDtypeStruct((N * L, D_B), BF16),
        grid=(N,),
        in_specs=[blk(3), blk(4), blk(5),
                  pl.BlockSpec(memory_space=pl.ANY), pl.BlockSpec(memory_space=pl.ANY)],
        out_specs=pl.BlockSpec((L, D_B), lambda n: (n, 0)),
        scratch_shapes=[pltpu.VMEM((NH_B, tk, DH_B), F32), pltpu.VMEM((NH_B, tk, DH_B), F32),
                        pltpu.SemaphoreType.DMA(()), pltpu.SemaphoreType.DMA(())],
        compiler_params=_params(("arbitrary",)),
        name="sb_attention_past",
    )(P, P, P, past_k, past_v)


def _kv_layout_kernel(kv_ref, kp_ref, ks_ref, vp_ref, vs_ref, *, n_first):
    tm = kv_ref.shape[0]
    first = pl.program_id(0) < n_first

    def relayout(k_out, v_out):
        for which, out in enumerate((k_out, v_out)):
            for h in range(NH_B):
                out[pl.ds(h, tm, stride=NH_B), :] = kv_ref[:, which * D_B + h * DH_B:which * D_B + (h + 1) * DH_B]

    pl.when(first)(functools.partial(relayout, kp_ref, vp_ref))
    pl.when(jnp.logical_not(first))(functools.partial(relayout, ks_ref, vs_ref))


def kv_outputs(P, Tp, tm):
    T = P.shape[0]
    n_first = Tp // tm
    outs = _two_stream_specs(tm * NH_B, DH_B, n_first)
    shapes = [jax.ShapeDtypeStruct((Tp * NH_B, DH_B), F32), jax.ShapeDtypeStruct(((T - Tp) * NH_B, DH_B), F32)]
    return pl.pallas_call(
        functools.partial(_kv_layout_kernel, n_first=n_first),
        out_shape=tuple(shapes + shapes),
        grid=(T // tm,),
        in_specs=[pl.BlockSpec((tm, 2 * D_B), lambda i: (i, 2))],
        out_specs=tuple(outs + outs),
        compiler_params=_params(("arbitrary",)),
        name="kv_layout",
    )(P)


def _merge_kernel(h_ref, ha_ref, hb_ref, wga_ref, wgb_ref, wpa_ref, wpb_ref, o_ref):
    h = h_ref[...]
    ga = jnp.dot(h, wga_ref[...], preferred_element_type=F32)
    gb = jnp.dot(h, wgb_ref[...], preferred_element_type=F32)
    pa = jnp.dot(ha_ref[...], wpa_ref[...], preferred_element_type=F32)
    pb = jnp.dot(hb_ref[...], wpb_ref[...], preferred_element_type=F32)
    o_ref[...] = (jax.nn.sigmoid(ga) * pa + jax.nn.sigmoid(gb) * pb).astype(o_ref.dtype)


def merge(h, row0, ha, hb, wga, wgb, wpa, wpb):
    Tn = ha.shape[0]
    D = h.shape[1]
    tm = _pick(math.gcd(Tn, row0) if row0 else Tn, (1024, 512, 256, 128, 64, 32, 16))
    tn = _pick(D, (512, 256, 128))
    row = lambda i, j: (i, 0)
    colw = lambda i, j: (0, j)
    return pl.pallas_call(
        _merge_kernel,
        out_shape=jax.ShapeDtypeStruct((Tn, D), BF16),
        grid=(Tn // tm, D // tn),
        in_specs=[pl.BlockSpec((tm, D), lambda i, j: (row0 // tm + i, 0)),
                  pl.BlockSpec((tm, D_A), row), pl.BlockSpec((tm, D_B), row),
                  pl.BlockSpec((D, tn), colw), pl.BlockSpec((D, tn), colw),
                  pl.BlockSpec((D_A, tn), colw), pl.BlockSpec((D_B, tn), colw)],
        out_specs=pl.BlockSpec((tm, tn), lambda i, j: (i, j)),
        compiler_params=_params(("parallel", "arbitrary")),
        name="merge",
    )(h, ha, hb, wga, wgb, wpa, wpb)


def _wout_route_kernel(mgp_ref, mgs_ref, xp_ref, xs_ref, wo_ref, g_ref, wr_ref, br_ref, x1_ref, t_ref, r_ref,
                       cnt_ref, run_s, *, n_first):
    i = pl.program_id(0)

    @pl.when(i == 0)
    def _():
        run_s[...] = jnp.zeros_like(run_s)

    x = jnp.where(i < n_first, xp_ref[...], xs_ref[...])
    mg = jnp.where(i < n_first, mgp_ref[...], mgs_ref[...])
    x1 = x + jnp.dot(mg, wo_ref[...], preferred_element_type=F32)
    x1_ref[...] = x1
    t = _rms(x1, g_ref[...])
    _store_row_major(t_ref, t)
    lg = jnp.dot(t.astype(BF16), wr_ref[...], preferred_element_type=F32) + br_ref[...]
    tm = lg.shape[0]
    lane = lax.broadcasted_iota(jnp.int32, (tm, LANES), 1)
    ninf = -jnp.inf
    glm = jnp.where(lane < N_GROUPS, lg, ninf)
    gmax = jnp.max(glm, axis=1, keepdims=True)
    gsel = jnp.min(jnp.where(glm == gmax, lane, LANES), axis=1, keepdims=True)
    p_sel = 1.0 / jnp.sum(jnp.exp(glm - gmax), axis=1, keepdims=True)
    lo = N_GROUPS + E_PER_GROUP * gsel
    elm = jnp.where((lane >= lo) & (lane < lo + E_PER_GROUP), lg, ninf)
    m1 = jnp.max(elm, axis=1, keepdims=True)
    i1 = jnp.min(jnp.where(elm == m1, lane, LANES), axis=1, keepdims=True)
    elm2 = jnp.where(lane == i1, ninf, elm)
    m2 = jnp.max(elm2, axis=1, keepdims=True)
    i2 = jnp.min(jnp.where(elm2 == m2, lane, LANES), axis=1, keepdims=True)
    e2 = jnp.exp(m2 - m1)
    w1 = p_sel / (1.0 + e2)
    w2 = p_sel * e2 / (1.0 + e2)
    e1 = (lane == i1 - N_GROUPS).astype(F32)
    e2h = (lane == i2 - N_GROUPS).astype(F32)
    both = e1 + e2h
    rr = lax.broadcasted_iota(jnp.int32, (tm, tm), 0)
    cc = lax.broadcasted_iota(jnp.int32, (tm, tm), 1)
    before = jnp.dot((cc < rr).astype(BF16), both.astype(BF16), preferred_element_type=F32) + run_s[...]
    rank1 = jnp.sum(e1 * before, axis=1, keepdims=True)
    rank2 = jnp.sum(e2h * before, axis=1, keepdims=True)
    run_s[...] = run_s[...] + jnp.sum(both, axis=0, keepdims=True)
    cnt_ref[...] = run_s[...]
    r = jnp.where(lane == 0, (i1 - N_GROUPS).astype(F32),
        jnp.where(lane == 1, (i2 - N_GROUPS).astype(F32),
        jnp.where(lane == 2, w1, jnp.where(lane == 3, w2,
        jnp.where(lane == 4, rank1, jnp.where(lane == 5, rank2, 0.0))))))
    r_ref[...] = r


def wout_route(mg_p, mg_s, xp, xs, wo, g_ffn, wr, br, tm):
    (Tp, D), Ts = xp.shape, xs.shape[0]
    T = Tp + Ts
    row = lambda i: (i, 0)
    fix = lambda i: (0, 0)
    return pl.pallas_call(
        functools.partial(_wout_route_kernel, n_first=Tp // tm),
        out_shape=(jax.ShapeDtypeStruct((T, D), F32), jax.ShapeDtypeStruct((T * ROW_PITCH, LANES), F32),
                   jax.ShapeDtypeStruct((T, LANES), F32), jax.ShapeDtypeStruct((1, LANES), F32)),
        grid=(T // tm,),
        in_specs=_two_stream_specs(tm, D, Tp // tm) + _two_stream_specs(tm, D, Tp // tm)
                 + [pl.BlockSpec((D, D), fix), pl.BlockSpec((1, D), fix), pl.BlockSpec((D, LANES), fix),
                    pl.BlockSpec((1, LANES), fix)],
        out_specs=(pl.BlockSpec((tm, D), row), pl.BlockSpec((tm * ROW_PITCH, LANES), row),
                   pl.BlockSpec((tm, LANES), row),
                   pl.BlockSpec((1, LANES), fix)),
        scratch_shapes=[pltpu.VMEM((1, LANES), F32)],
        compiler_params=_params(("arbitrary",)),
        name="wout_route",
    )(mg_p, mg_s, xp, xs, wo, g_ffn.reshape(1, D), wr, br)


IDX_RING = 8


def _expert_kernel(te_ref, nu_ref, src_hbm, dst_hbm, t_hbm, wg_ref, wu_ref, wd_ref, y_hbm,
                   sidx, didx, xb0, xb1, yb0, yb1, wg_s, wu_s, wd_s, gsem, ssem, isem, *, dump0):
    i = pl.program_id(0)
    nu = nu_ref[0]
    TE = EXPERT_TILE
    xb, yb = (xb0, xb1), (yb0, yb1)

    def table_copies(tile, seq):
        base = (seq & (IDX_RING - 1)) * TE
        return (pltpu.make_async_copy(src_hbm.at[pl.ds(tile * TE, TE)], sidx.at[pl.ds(base, TE)], isem.at[0]),
                pltpu.make_async_copy(dst_hbm.at[pl.ds(tile * TE, TE)], didx.at[pl.ds(base, TE)], isem.at[1]))

    def gather(seq, par):
        base = (seq & (IDX_RING - 1)) * TE
        return [pltpu.make_async_copy(t_hbm.at[pl.ds(sidx[base + r], ROW_CHUNKS), :],
                                      xb[par].at[pl.ds(r * ROW_PITCH, ROW_CHUNKS), :],
                                      gsem.at[par]) for r in range(TE)]

    def scatter(seq, par):
        base = (seq & (IDX_RING - 1)) * TE
        return [pltpu.make_async_copy(yb[par].at[pl.ds(r * ROW_PITCH, ROW_PITCH), :],
                                      y_hbm.at[pl.ds(didx[base + r], ROW_PITCH), :],
                                      ssem.at[par]) for r in range(TE)]

    def start(cps):
        rows = len(cps) == TE
        for n, cp in enumerate(cps):
            cp.start(priority=n % 2 if rows else 0)

    def wait(cps):
        for cp in cps:
            cp.wait()

    last = nu - 1

    @pl.when(i == 0)
    def _():
        first = table_copies(0, 0)
        start(first)
        wait(first)
        start(table_copies(jnp.minimum(1, last), 1))
        start(gather(0, 0))
        yb0[...] = jnp.zeros_like(yb0)
        yb1[...] = jnp.zeros_like(yb1)
        for r in range(TE):
            didx[(IDX_RING - 1) * TE + r] = (dump0 + TE + r) * ROW_PITCH
        fill = pltpu.make_async_copy(yb1, y_hbm.at[pl.ds(dump0 * ROW_PITCH, TE * ROW_PITCH), :], ssem.at[0])
        fill.start()
        fill.wait()

    @pl.when((i < nu) & ((i == 0) | (te_ref[i] != te_ref[jnp.maximum(i - 1, 0)])))
    def _():
        wg_s[...] = wg_ref[...].astype(BF16)
        wu_s[...] = wu_ref[...].astype(BF16)
        wd_s[...] = wd_ref[...].astype(BF16)

    def step(par):
        wait(table_copies(jnp.minimum(i + 1, last), i + 1))
        start(table_copies(jnp.minimum(i + 2, last), i + 2))

        @pl.when(i >= 1)
        def _():
            wait(scatter(i - 2, par))

        wait(gather(i, par))
        start(scatter(i - 1, 1 - par))
        start(gather(i + 1, 1 - par))
        x = _load_row_major(xb[par], TE).astype(BF16)
        a = jnp.dot(x, wg_s[...], preferred_element_type=F32)
        u = jnp.dot(x, wu_s[...], preferred_element_type=F32)
        hid = (a * jax.nn.sigmoid(a) * u).astype(BF16)
        y = jnp.dot(hid, wd_s[...], preferred_element_type=F32)
        for j in range(ROW_CHUNKS):
            yb[par][pl.ds(j, TE, stride=ROW_PITCH), :] = y[:, j * LANES:(j + 1) * LANES]

        @pl.when(i == last)
        def _():
            start(scatter(i, par))
            wait(scatter(i - 1, 1 - par))
            wait(scatter(i, par))
            wait(gather(i + 1, 1 - par))
            wait(table_copies(last, i + 2))

    for par in (0, 1):
        pl.when((i < nu) & (i % 2 == par))(functools.partial(step, par))


def experts(t, slot_src, slot_dst, tile_expert, n_used, w_gate, w_up, w_down):
    T = t.shape[0] // ROW_PITCH
    D, F = w_gate.shape[-2:]
    nt = slot_src.shape[0] // EXPERT_TILE
    wmap = lambda i, te, nu: (te[i], 0, 0)
    any_spec = pl.BlockSpec(memory_space=pl.ANY)
    tile_buf = pltpu.VMEM((EXPERT_TILE * ROW_PITCH, LANES), F32)
    grid_spec = pltpu.PrefetchScalarGridSpec(
        num_scalar_prefetch=2,
        grid=(nt,),
        in_specs=[any_spec, any_spec, any_spec,
                  pl.BlockSpec((None, D, F), wmap), pl.BlockSpec((None, D, F), wmap),
                  pl.BlockSpec((None, F, D), wmap)],
        out_specs=any_spec,
        scratch_shapes=[pltpu.SMEM((IDX_RING * EXPERT_TILE,), jnp.int32),
                        pltpu.SMEM((IDX_RING * EXPERT_TILE,), jnp.int32),
                        tile_buf, tile_buf, tile_buf, tile_buf,
                        pltpu.VMEM((D, F), BF16), pltpu.VMEM((D, F), BF16), pltpu.VMEM((F, D), BF16),
                        pltpu.SemaphoreType.DMA((2,)), pltpu.SemaphoreType.DMA((2,)),
                        pltpu.SemaphoreType.DMA((2,))],
    )
    return pl.pallas_call(
        functools.partial(_expert_kernel, dump0=2 * T),
        out_shape=jax.ShapeDtypeStruct(((2 * T + 2 * EXPERT_TILE) * ROW_PITCH, LANES), F32),
        grid_spec=grid_spec,
        compiler_params=_params(("arbitrary",)),
        name="moe_experts",
    )(tile_expert, n_used, slot_src, slot_dst, t, w_gate.reshape(N_EXPERTS, D, F),
      w_up.reshape(N_EXPERTS, D, F), w_down.reshape(N_EXPERTS, F, D))


def _combine_kernel(x1_ref, y0_ref, y1_ref, r_ref, pp_ref, ps_ref, wple_ref, wpg_ref, gple_ref, gfin_ref,
                    yp_ref, ys_ref, *, n_first):
    i = pl.program_id(0)
    p = jnp.where(i < n_first, pp_ref[...], ps_ref[...])
    r = r_ref[...]
    tm = r.shape[0]
    x2 = x1_ref[...] + r[:, 2:3] * _load_row_major(y0_ref, tm) + r[:, 3:4] * _load_row_major(y1_ref, tm)
    hp = _rms(x2, gple_ref[...]).astype(BF16)
    gate = jax.nn.sigmoid(jnp.dot(hp, wpg_ref[...], preferred_element_type=F32))
    emb = jnp.dot(p.astype(BF16), wple_ref[...], preferred_element_type=F32)
    y = _rms(x2 + emb * gate, gfin_ref[...])

    @pl.when(i < n_first)
    def _():
        yp_ref[...] = y

    @pl.when(i >= n_first)
    def _():
        ys_ref[...] = y


def combine(y_flat, x1, route, p_p, p_s, w_ple, w_ple_gate, g_ple, g_final, tm):
    T, D = x1.shape
    row = lambda i: (i, 0)
    row1 = lambda i: (i + T // tm, 0)
    fix = lambda i: (0, 0)
    Tp = p_p.shape[0]
    n_first = Tp // tm
    return pl.pallas_call(
        functools.partial(_combine_kernel, n_first=n_first),
        out_shape=(jax.ShapeDtypeStruct((Tp, D), F32), jax.ShapeDtypeStruct((T - Tp, D), F32)),
        grid=(T // tm,),
        in_specs=[pl.BlockSpec((tm, D), row), pl.BlockSpec((tm * ROW_PITCH, LANES), row),
                  pl.BlockSpec((tm * ROW_PITCH, LANES), row1),
                  pl.BlockSpec((tm, LANES), row)] + _two_stream_specs(tm, p_p.shape[1], n_first)
                 + [pl.BlockSpec(w_ple.shape, fix), pl.BlockSpec((D, D), fix),
                  pl.BlockSpec((1, D), fix), pl.BlockSpec((1, D), fix)],
        out_specs=tuple(_two_stream_specs(tm, D, n_first)),
        compiler_params=_params(("arbitrary",)),
        name="moe_combine_ple",
    )(x1, y_flat, y_flat, route, p_p, p_s, w_ple, w_ple_gate, g_ple.reshape(1, D), g_final.reshape(1, D))


def _routing_tables(route, counts, n_tiles):
    ids = route[:, 0:2].astype(jnp.int32).reshape(-1)
    rank = route[:, 4:6].astype(jnp.int32).reshape(-1)
    counts = counts.astype(jnp.int32)
    tiles = (counts + EXPERT_TILE - 1) // EXPERT_TILE
    tile_end = jnp.cumsum(tiles)
    tile_start = tile_end - tiles
    onehot = ids[:, None] == jnp.arange(N_EXPERTS, dtype=jnp.int32)[None, :]
    pos = jnp.sum(jnp.where(onehot, tile_start[None, :], 0), axis=1) * EXPERT_TILE + rank
    slot_assign = jnp.full((n_tiles * EXPERT_TILE,), -1, jnp.int32).at[pos].set(
        jnp.arange(ids.shape[0], dtype=jnp.int32), unique_indices=True)
    n_assign = ids.shape[0]
    slot = jnp.arange(n_tiles * EXPERT_TILE, dtype=jnp.int32)
    token = jnp.maximum(slot_assign, 0) // 2
    slot_src = token * ROW_PITCH
    dump = n_assign + ((slot // EXPERT_TILE) % 2) * EXPERT_TILE + slot % EXPERT_TILE
    slot_dst = jnp.where(slot_assign >= 0, (slot_assign % 2) * (n_assign // 2) + token, dump) * ROW_PITCH
    n_used = tile_end[-1]
    tidx = jnp.minimum(jnp.arange(n_tiles, dtype=jnp.int32), n_used - 1)
    tile_expert = jnp.sum((tidx[:, None] >= tile_end[None, :]).astype(jnp.int32), axis=1)
    return slot_src, slot_dst, tile_expert.astype(jnp.int32), n_used.reshape(1).astype(jnp.int32)


def kernel(x_prompt, x_sample, cache_k, cache_v, state_conv, state_C, state_n, state_m, p_prompt, p_sample,
           g_mix, w_in, b_if, conv_w, conv_b, g_head_a, w_proj_a, w_proj_b, w_out, g_ffn, w_route_g,
           b_route_g, w_route_e, b_route_e, w_exp_gate, w_exp_up, w_exp_down, g_ple, w_ple, w_ple_gate,
           g_final):
    assert w_in.shape[0] == 1, "single layer"
    B, S, D = x_prompt.shape
    DB, DS, _ = x_sample.shape
    Tp, Ts = B * S, DB * DS
    T = Tp + Ts
    xp, xs = x_prompt.reshape(Tp, D), x_sample.reshape(Ts, D)
    tm = _pick(math.gcd(Tp, Ts), (256, 128, 64, 32, 16, 8))

    wi = w_in[0]
    g0 = 2 * QK_A + D_A
    m0 = g0 + 2 * NH_A
    e0 = m0 + D_A + 3 * D_B
    w_main = jnp.concatenate([wi[:, :g0], wi[:, m0:e0]], axis=1).astype(BF16)
    w_gates = jnp.zeros((D, LANES), F32).at[:, :2 * NH_A].set(wi[:, g0:m0]).astype(BF16)
    wga = wi[:, e0:e0 + D].astype(BF16)
    wgb = wi[:, e0 + D:].astype(BF16)

    h = rms_cast(xp, xs, g_mix[0], tm)
    P = matmul(h, w_main, "proj_main")
    G = matmul(h, w_gates, "proj_gates")

    zeros = functools.partial(jnp.zeros, dtype=F32)
    ha_p, C_p, n_p, m_p, cv_p = mlstm(P, G, 0, B, S, b_if[0], conv_w[0], conv_b[0], g_head_a[0],
                                      zeros((B, CONV_W - 1, 2 * QK_A)), zeros((B, NH_A, DQK_A, DV_A)),
                                      zeros((B, NH_A, DQK_A)), zeros((B, NH_A)))
    ha_s, C_s, n_s, m_s, cv_s = mlstm(P, G, Tp, DB, DS, b_if[0], conv_w[0], conv_b[0], g_head_a[0],
                                      state_conv[0], state_C[0], state_n[0], state_m[0])
    hb_p = sb_attention(P, 0, B, S)
    hb_s = sb_attention_past(P, Tp, DB, DS, cache_k, cache_v)
    wpa, wpb = w_proj_a[0].astype(BF16), w_proj_b[0].astype(BF16)
    mg_p = merge(h, 0, ha_p, hb_p, wga, wgb, wpa, wpb)
    mg_s = merge(h, Tp, ha_s, hb_s, wga, wgb, wpa, wpb)

    wr = jnp.zeros((D, LANES), F32)
    wr = wr.at[:, :N_GROUPS].set(w_route_g[0])
    wr = wr.at[:, N_GROUPS:N_GROUPS + N_EXPERTS].set(
        jnp.transpose(w_route_e[0], (1, 0, 2)).reshape(D, N_EXPERTS))
    br = jnp.zeros((1, LANES), F32)
    br = br.at[0, :N_GROUPS].set(b_route_g[0])
    br = br.at[0, N_GROUPS:N_GROUPS + N_EXPERTS].set(b_route_e[0].reshape(-1))
    x1, t, route, counts = wout_route(mg_p, mg_s, xp, xs, w_out[0].astype(BF16), g_ffn[0], wr.astype(BF16), br, tm)

    n_tiles = (2 * T + N_EXPERTS * (EXPERT_TILE - 1)) // EXPERT_TILE + 1
    slot_src, slot_dst, tile_expert, n_used = _routing_tables(route, counts[0, :N_EXPERTS], n_tiles)
    y_flat = experts(t, slot_src, slot_dst, tile_expert, n_used, w_exp_gate[0], w_exp_up[0], w_exp_down[0])
    y_p, y_s = combine(y_flat, x1, route, p_prompt[0].reshape(Tp, -1), p_sample[0].reshape(Ts, -1),
                       w_ple[0].astype(BF16), w_ple_gate[0].astype(BF16), g_ple[0], g_final, tm)

    k_p, k_s, v_p, v_s = kv_outputs(P, Tp, tm)
    return (y_p.reshape(B, S, D), y_s.reshape(DB, DS, D),
            k_p.reshape(1, B, S, NH_B, DH_B), v_p.reshape(1, B, S, NH_B, DH_B),
            cv_p[None], C_p[None], n_p[None], m_p[None],
            k_s.reshape(1, DB, DS, NH_B, DH_B), v_s.reshape(1, DB, DS, NH_B, DH_B),
            cv_s[None], C_s[None], n_s[None], m_s[None])
```

```python
import functools
import math

import jax
import jax.numpy as jnp
from jax import lax
from jax.experimental import pallas as pl
from jax.experimental.pallas import tpu as pltpu

F32 = jnp.float32
BF16 = jnp.bfloat16
EPS = 1e-6

CHUNK = 128
NH_A, DQK_A, DV_A = 4, 128, 256
QK_A, D_A = NH_A * DQK_A, NH_A * DV_A
CONV_W = 4
NH_B, DH_B = 8, 128
D_B = NH_B * DH_B
N_GROUPS, E_PER_GROUP = 4, 8
N_EXPERTS = N_GROUPS * E_PER_GROUP
LANES = 128
EXPERT_TILE = 256
VMEM_LIMIT = 56 * 1024 * 1024


def _pick(n, cands):
    for c in cands:
        if n % c == 0:
            return c
    raise ValueError(f"no tile for {n} in {cands}")


def _params(sem):
    return pltpu.CompilerParams(dimension_semantics=sem, vmem_limit_bytes=VMEM_LIMIT)


def _rms(x, g):
    return x * lax.rsqrt(jnp.mean(x * x, axis=-1, keepdims=True) + EPS) * g


ROW_CHUNKS = 2048 // LANES
ROW_PITCH = 20


def _store_row_major(ref, x):
    rows = x.shape[0]
    for j in range(ROW_PITCH):
        piece = x[:, j * LANES:(j + 1) * LANES] if j < ROW_CHUNKS else jnp.zeros((rows, LANES), x.dtype)
        ref[pl.ds(j, rows, stride=ROW_PITCH), :] = piece


def _load_row_major(ref, rows):
    return jnp.concatenate([ref[pl.ds(j, rows, stride=ROW_PITCH), :] for j in range(ROW_CHUNKS)], axis=1)


def _log_sigmoid(x):
    return jnp.minimum(x, 0.0) - jnp.log(1.0 + jnp.exp(-jnp.abs(x)))


def _two_stream_specs(tm, D, n_first):
    return [pl.BlockSpec((tm, D), lambda i: (jnp.minimum(i, n_first - 1), 0)),
            pl.BlockSpec((tm, D), lambda i: (jnp.maximum(i - n_first, 0), 0))]


def _rms_kernel(xp_ref, xs_ref, g_ref, o_ref, *, n_first):
    x = jnp.where(pl.program_id(0) < n_first, xp_ref[...], xs_ref[...])
    o_ref[...] = _rms(x, g_ref[...]).astype(o_ref.dtype)


def rms_cast(xp, xs, g, tm):
    (Tp, D), Ts = xp.shape, xs.shape[0]
    T = Tp + Ts
    return pl.pallas_call(
        functools.partial(_rms_kernel, n_first=Tp // tm),
        out_shape=jax.ShapeDtypeStruct((T, D), BF16),
        grid=(T // tm,),
        in_specs=_two_stream_specs(tm, D, Tp // tm) + [pl.BlockSpec((1, D), lambda i: (0, 0))],
        out_specs=pl.BlockSpec((tm, D), lambda i: (i, 0)),
        compiler_params=_params(("arbitrary",)),
        name="rms_cast",
    )(xp, xs, g.reshape(1, D))


def _mm_kernel(a_ref, w_ref, o_ref):
    o_ref[...] = jnp.dot(a_ref[...], w_ref[...], preferred_element_type=F32)


def matmul(a, w, name):
    T, K = a.shape
    N = w.shape[1]
    tm = _pick(T, (1280, 640, 512, 256, 128, 64, 32, 16))
    tn = _pick(N, (512, 256, 128))
    return pl.pallas_call(
        _mm_kernel,
        out_shape=jax.ShapeDtypeStruct((T, N), F32),
        grid=(T // tm, N // tn),
        in_specs=[pl.BlockSpec((tm, K), lambda i, j: (i, 0)),
                  pl.BlockSpec((K, tn), lambda i, j: (0, j))],
        out_specs=pl.BlockSpec((tm, tn), lambda i, j: (i, j)),
        compiler_params=_params(("parallel", "arbitrary")),
        name=name,
    )(a, w)


def _proj_main_kernel(a_ref, wa_ref, wb_ref, o_ref, *, n_a):
    j = pl.program_id(1)

    @pl.when(j < n_a)
    def _():
        o_ref[...] = jnp.dot(a_ref[...], wa_ref[...].astype(BF16), preferred_element_type=F32)

    @pl.when(j >= n_a)
    def _():
        o_ref[...] = jnp.dot(a_ref[...], wb_ref[...], preferred_element_type=F32)


def proj_main(a, w_f32, n_cols_a, w_rest, n_cols_b):
    T, K = a.shape
    tm = _pick(T, (1280, 640, 512, 256, 128, 64, 32, 16))
    tn = _pick(math.gcd(n_cols_a, n_cols_b), (512, 256, 128))
    n_a, n_b = n_cols_a // tn, n_cols_b // tn
    return pl.pallas_call(
        functools.partial(_proj_main_kernel, n_a=n_a),
        out_shape=jax.ShapeDtypeStruct((T, n_cols_a + n_cols_b), F32),
        grid=(T // tm, n_a + n_b),
        in_specs=[pl.BlockSpec((tm, K), lambda i, j: (i, 0)),
                  pl.BlockSpec((K, tn), lambda i, j: (0, jnp.minimum(j, n_a - 1))),
                  pl.BlockSpec((K, tn), lambda i, j: (0, jnp.maximum(j - n_a, 0)))],
        out_specs=pl.BlockSpec((tm, tn), lambda i, j: (i, j)),
        compiler_params=_params(("parallel", "arbitrary")),
        name="proj_main",
    )(a, w_f32, w_rest)


def _mlstm_kernel(*refs, c, nc, nb):
    qk_refs, v_refs, o_refs, g_refs = (refs[k * nb:(k + 1) * nb] for k in range(4))
    (bif_ref, cw_ref, cb_ref, gh_ref, cbuf_ref, c0_ref, n0_ref, m0_ref,
     h_ref, cout_ref, nout_ref, mout_ref, convout_ref, xbuf, c_s, n_s, m_s) = refs[4 * nb:]
    j = pl.program_id(1)

    @pl.when(j == 0)
    def _():
        c_s[...] = c0_ref[...]
        n_s[...] = n0_ref[...]
        m_s[...] = m0_ref[...]
        xbuf[:, 5:8, :] = cbuf_ref[...]

    row = lax.broadcasted_iota(jnp.int32, (c, c), 0)
    col = lax.broadcasted_iota(jnp.int32, (c, c), 1)
    consts = dict(lane=lax.broadcasted_iota(jnp.int32, (c, LANES), 1), tril=(col <= row).astype(F32),
                  eye=(col == row).astype(F32), triu=(row <= col).astype(F32), causal=col <= row)
    last_rows = [
        _mlstm_chunk(qk_refs[b], v_refs[b], o_refs[b], g_refs[b], bif_ref, cw_ref, cb_ref, gh_ref,
                     h_ref.at[b], xbuf.at[b], c_s.at[b], n_s.at[b], m_s.at[b], consts, c)
        for b in range(nb)]

    @pl.when(j == nc - 1)
    def _():
        cout_ref[...] = c_s[...]
        nout_ref[...] = n_s[...]
        mout_ref[...] = m_s[...]
        for b in range(nb):
            convout_ref[b] = last_rows[b]


def _mlstm_chunk(qk_ref, v_ref, o_ref, g_ref, bif_ref, cw_ref, cb_ref, gh_ref, h_ref, xbuf, c_s, n_s, m_s,
                 consts, c):
    lane, tril, eye, triu, causal = (consts[k] for k in ("lane", "tril", "eye", "triu", "causal"))
    xbuf[8:8 + c, :] = qk_ref[...]
    y = cb_ref[...] + cw_ref[0:1, :] * xbuf[5:5 + c, :]
    for t in range(1, CONV_W):
        y = y + cw_ref[t:t + 1, :] * xbuf[5 + t:5 + t + c, :]
    last_rows = xbuf[c + 5:c + 8, :]
    xbuf[5:8, :] = last_rows
    qk = y * jax.nn.sigmoid(y)

    graw = g_ref[...] + bif_ref[...]
    xg = jnp.where(lane < NH_A, graw, jnp.where(lane < 2 * NH_A, _log_sigmoid(graw), 0.0))
    hp = lax.Precision.HIGHEST
    bc = jnp.dot(tril, xg, precision=hp, preferred_element_type=F32)
    dn0 = (((0,), (0,)), ((), ()))
    xt = lax.dot_general(xg, eye, dn0, precision=hp, preferred_element_type=F32)
    bt = lax.dot_general(xg, triu, dn0, precision=hp, preferred_element_type=F32)

    for h in range(NH_A):
        q = qk[:, h * DQK_A:(h + 1) * DQK_A]
        k = qk[:, QK_A + h * DQK_A:QK_A + (h + 1) * DQK_A] * (DQK_A ** -0.5)
        v = v_ref[:, h * DV_A:(h + 1) * DV_A]
        qb, kb, vb = q.astype(BF16), k.astype(BF16), v.astype(BF16)
        i_col = xg[:, h:h + 1]
        b_col = bc[:, NH_A + h:NH_A + h + 1]
        i_row = xt[h:h + 1, :]
        b_row = bt[NH_A + h:NH_A + h + 1, :]
        m_prev = m_s[0:1, h:h + 1]
        C = c_s[h]
        nvec = n_s[h:h + 1, :]

        dmat = jnp.where(causal, b_col - b_row + i_row, -jnp.inf)
        inter = b_col + m_prev
        m_t = jnp.maximum(inter, jnp.max(dmat, axis=1, keepdims=True))
        e = jnp.exp(dmat - m_t)
        s = lax.dot_general(qb, kb, (((1,), (1,)), ((), ())), preferred_element_type=F32) * e
        w_inter = jnp.exp(inter - m_t)
        num = (jnp.dot(s.astype(BF16), vb, preferred_element_type=F32)
               + w_inter * jnp.dot(qb, C.astype(BF16), preferred_element_type=F32))
        den = jnp.sum(s, axis=1, keepdims=True) + w_inter * jnp.sum(q * nvec, axis=1, keepdims=True)
        hh = num / jnp.maximum(jnp.abs(den), jnp.exp(-m_t))
        hh = hh * lax.rsqrt(jnp.mean(hh * hh, axis=1, keepdims=True) + EPS)
        hh = hh * gh_ref[:, h * DV_A:(h + 1) * DV_A]
        hh = jax.nn.sigmoid(o_ref[:, h * DV_A:(h + 1) * DV_A]) * hh
        h_ref[:, h * DV_A:(h + 1) * DV_A] = hh.astype(h_ref.dtype)

        b_end = b_col[c - 1:c, :]
        dec = b_end - b_col + i_col
        m_new = jnp.maximum(b_end + m_prev, jnp.max(dec, axis=0, keepdims=True))
        wk = jnp.exp(dec - m_new)
        w_old = jnp.exp(b_end + m_prev - m_new)
        kw = k * wk
        c_s[h] = w_old * C + lax.dot_general(kw.astype(BF16), vb, dn0, preferred_element_type=F32)
        n_s[h:h + 1, :] = w_old * nvec + jnp.sum(kw, axis=0, keepdims=True)
        m_s[0:1, h:h + 1] = m_new
    return last_rows


MLSTM_SEQS = 2


def mlstm(P, G, row0, N, L, b_if, conv_w, conv_b, g_head, conv_buf, C0, n0, m0):
    c = min(L, CHUNK)
    nc = L // c
    nb = MLSTM_SEQS if N % MLSTM_SEQS == 0 else 1
    base = row0 // c
    m0p = jnp.zeros((N, 1, LANES), F32).at[:, 0, :NH_A].set(m0)
    bif = jnp.zeros((1, LANES), F32).at[0, :2 * NH_A].set(b_if)

    def rows(width, col):
        return [pl.BlockSpec((c, width), lambda n, j, b=b: (base + (n * nb + b) * nc + j, col))
                for b in range(nb)]
    fix = lambda n, j: (0, 0)
    st4 = lambda n, j: (n, 0, 0, 0)
    st3 = lambda n, j: (n, 0, 0)
    outs = pl.pallas_call(
        functools.partial(_mlstm_kernel, c=c, nc=nc, nb=nb),
        out_shape=(jax.ShapeDtypeStruct((N, L, D_A), BF16),
                   jax.ShapeDtypeStruct((N, NH_A, DQK_A, DV_A), F32),
                   jax.ShapeDtypeStruct((N, NH_A, DQK_A), F32),
                   jax.ShapeDtypeStruct((N, 1, LANES), F32),
                   jax.ShapeDtypeStruct((N, CONV_W - 1, 2 * QK_A), F32)),
        grid=(N // nb, nc),
        in_specs=rows(2 * QK_A, 0) + rows(D_A, 1) + rows(D_A, 2) + rows(LANES, 0)
                 + [pl.BlockSpec((1, LANES), fix),
                    pl.BlockSpec((CONV_W, 2 * QK_A), fix),
                    pl.BlockSpec((1, 2 * QK_A), fix),
                    pl.BlockSpec((1, D_A), fix),
                    pl.BlockSpec((nb, CONV_W - 1, 2 * QK_A), st3),
                    pl.BlockSpec((nb, NH_A, DQK_A, DV_A), st4),
                    pl.BlockSpec((nb, NH_A, DQK_A), st3),
                    pl.BlockSpec((nb, 1, LANES), st3)],
        out_specs=(pl.BlockSpec((nb, c, D_A), lambda n, j: (n, j, 0)),
                   pl.BlockSpec((nb, NH_A, DQK_A, DV_A), st4),
                   pl.BlockSpec((nb, NH_A, DQK_A), st3),
                   pl.BlockSpec((nb, 1, LANES), st3),
                   pl.BlockSpec((nb, CONV_W - 1, 2 * QK_A), st3)),
        scratch_shapes=[pltpu.VMEM((nb, c + 8, 2 * QK_A), F32),
                        pltpu.VMEM((nb, NH_A, DQK_A, DV_A), F32),
                        pltpu.VMEM((nb, NH_A, DQK_A), F32),
                        pltpu.VMEM((nb, 1, LANES), F32)],
        compiler_params=_params(("parallel", "arbitrary")),
        name="mlstm",
    )(*([P] * (3 * nb) + [G] * nb), bif, conv_w, conv_b.reshape(1, -1), g_head.reshape(1, -1),
      conv_buf, C0, n0, m0p)
    h, C1, n1, m1, cv = outs
    return h.reshape(N * L, D_A), C1, n1, m1[:, 0, :NH_A], cv


STICK_UNDERFLOW = -110.0


def _later_matrix(tk):
    jr = lax.broadcasted_iota(jnp.int32, (tk, tk), 0)
    sc = lax.broadcasted_iota(jnp.int32, (tk, tk), 1)
    return (jr > sc).astype(BF16)


def _sb_blocks(q_bfs, k_blks, v_blks, Rs, later, mask):
    tq = q_bfs[0].shape[0]
    zs, l1mbs, parts = [], [], []
    for q_bf, k_blk in zip(q_bfs, k_blks):
        z = lax.dot_general(q_bf, k_blk.astype(BF16), (((1,), (1,)), ((), ())),
                            preferred_element_type=F32) * (DH_B ** -0.5)
        l1mb = -(jnp.maximum(z, 0.0) + jnp.log(1.0 + jnp.exp(-jnp.abs(z))))
        if mask is not None:
            l1mb = jnp.where(mask, l1mb, 0.0)
        hi = l1mb.astype(BF16)
        parts += [hi, (l1mb - hi.astype(F32)).astype(BF16)]
        zs.append(z)
        l1mbs.append(l1mb)
    suffix = jnp.dot(jnp.concatenate(parts, axis=0), later, preferred_element_type=F32)
    pvs, new_Rs = [], []
    for h, (z, l1mb) in enumerate(zip(zs, l1mbs)):
        rest = suffix[2 * h * tq:(2 * h + 1) * tq] + suffix[(2 * h + 1) * tq:(2 * h + 2) * tq] + Rs[h]
        a = jnp.exp(z + l1mb + rest)
        if mask is not None:
            a = jnp.where(mask, a, 0.0)
        pvs.append(jnp.dot(a.astype(BF16), v_blks[h].astype(BF16), preferred_element_type=F32))
        new_Rs.append(Rs[h] + jnp.sum(l1mb, axis=1, keepdims=True))
    return pvs, new_Rs


SB_HEADS = 4


def _sb_kernel(q_ref, k_ref, v_ref, o_ref, *, tq):
    qi = pl.program_id(2)
    rr = lax.broadcasted_iota(jnp.int32, (tq, tq), 0)
    cc = lax.broadcasted_iota(jnp.int32, (tq, tq), 1)
    later = _later_matrix(tq)
    cols = [slice(h * DH_B, (h + 1) * DH_B) for h in range(SB_HEADS)]
    q_bf = [q_ref[:, c].astype(BF16) for c in cols]
    def blocks(st, Rs, mask):
        return _sb_blocks(q_bf, [k_ref[pl.ds(st, tq), c] for c in cols], [v_ref[pl.ds(st, tq), c] for c in cols],
                          Rs, later, mask)

    accs, Rs = blocks(pl.multiple_of(qi * tq, tq), [jnp.zeros((tq, 1), F32)] * SB_HEADS, cc < rr)

    def rmax(Rs):
        return jnp.max(jnp.concatenate(Rs, axis=1))

    def cond(c):
        return jnp.logical_and(c[0] < qi, c[3] > STICK_UNDERFLOW)

    def body(carry):
        t, accs, Rs, _ = carry
        pvs, new_Rs = blocks(pl.multiple_of((qi - 1 - t) * tq, tq), list(Rs), None)
        new_accs = [acc + pv for acc, pv in zip(accs, pvs)]
        return t + 1, tuple(new_accs), tuple(new_Rs), rmax(new_Rs)

    _, accs, _, _ = lax.while_loop(cond, body, (jnp.int32(0), tuple(accs), tuple(Rs), rmax(Rs)))
    for h, c in enumerate(cols):
        o_ref[:, c] = accs[h].astype(o_ref.dtype)


QB_BLK, KB_BLK, VB_BLK = 3 * D_B // DH_B, 4 * D_B // DH_B, 5 * D_B // DH_B


def sb_attention(P, row0, N, L):
    tq = min(L, 256)
    nq = L // tq
    W = SB_HEADS * DH_B
    qb, kb, vb = QB_BLK // SB_HEADS, KB_BLK // SB_HEADS, VB_BLK // SB_HEADS
    return pl.pallas_call(
        functools.partial(_sb_kernel, tq=tq),
        out_shape=jax.ShapeDtypeStruct((N * L, D_B), BF16),
        grid=(N, NH_B // SB_HEADS, nq),
        in_specs=[pl.BlockSpec((tq, W), lambda n, h, i: (row0 // tq + n * nq + i, qb + h)),
                  pl.BlockSpec((L, W), lambda n, h, i: (row0 // L + n, kb + h), pipeline_mode=pl.Buffered(1)),
                  pl.BlockSpec((L, W), lambda n, h, i: (row0 // L + n, vb + h), pipeline_mode=pl.Buffered(1))],
        out_specs=pl.BlockSpec((tq, W), lambda n, h, i: (n * nq + i, h)),
        compiler_params=_params(("parallel", "parallel", "arbitrary")),
        name="sb_attention",
    )(P, P, P)


def _sb_past_kernel(q_ref, k_ref, v_ref, pk_hbm, pv_hbm, o_ref, kbuf, vbuf, ksem, vsem, *, L, tk, n_blk):
    n = pl.program_id(0)

    def copies(blk):
        st = blk * tk
        return ([pltpu.make_async_copy(pk_hbm.at[0, n, pl.ds(st, tk), h, :], kbuf.at[h], ksem)
                 for h in range(NH_B)]
                + [pltpu.make_async_copy(pv_hbm.at[0, n, pl.ds(st, tk), h, :], vbuf.at[h], vsem)
                   for h in range(NH_B)])

    rr = lax.broadcasted_iota(jnp.int32, (L, L), 0)
    cc = lax.broadcasted_iota(jnp.int32, (L, L), 1)
    later_new = _later_matrix(L)
    later = _later_matrix(tk)
    sls = [slice(h * DH_B, (h + 1) * DH_B) for h in range(NH_B)]
    q_bf = [q_ref[:, sl].astype(BF16) for sl in sls]
    accs, Rs = _sb_blocks(q_bf, [k_ref[:, sl] for sl in sls], [v_ref[:, sl] for sl in sls],
                          [jnp.zeros((L, 1), F32)] * NH_B, later_new, cc < rr)

    def rmax(Rs):
        return jnp.max(jnp.concatenate(Rs, axis=1))

    def cond(c):
        return jnp.logical_and(c[0] < n_blk, c[3] > STICK_UNDERFLOW)

    def body(c):
        t, accs, Rs, _ = c
        cps = copies(n_blk - 1 - t)
        for cp in cps:
            cp.start()
        for cp in cps:
            cp.wait()
        pvs, new_Rs = _sb_blocks(q_bf, [kbuf[h] for h in range(NH_B)], [vbuf[h] for h in range(NH_B)],
                                 list(Rs), later, None)
        new_accs = [acc + pv for acc, pv in zip(accs, pvs)]
        return t + 1, tuple(new_accs), tuple(new_Rs), rmax(new_Rs)

    _, accs, _, _ = lax.while_loop(cond, body, (jnp.int32(0), tuple(accs), tuple(Rs), rmax(Rs)))
    for p, sl in enumerate(sls):
        o_ref[:, sl] = accs[p].astype(o_ref.dtype)


def sb_attention_past(P, row0, N, L, past_k, past_v):
    Lp = past_k.shape[2]
    tk = min(Lp, 256)
    blk = lambda c: pl.BlockSpec((L, D_B), lambda n: (row0 // L + n, c))
    return pl.pallas_call(
        functools.partial(_sb_past_kernel, L=L, tk=tk, n_blk=Lp // tk),
        out_shape=jax.ShapeDtypeStruct((N * L, D_B), BF16),
        grid=(N,),
        in_specs=[blk(3), blk(4), blk(5),
                  pl.BlockSpec(memory_space=pl.ANY), pl.BlockSpec(memory_space=pl.ANY)],
        out_specs=pl.BlockSpec((L, D_B), lambda n: (n, 0)),
        scratch_shapes=[pltpu.VMEM((NH_B, tk, DH_B), F32), pltpu.VMEM((NH_B, tk, DH_B), F32),
                        pltpu.SemaphoreType.DMA(()), pltpu.SemaphoreType.DMA(())],
        compiler_params=_params(("arbitrary",)),
        name="sb_attention_past",
    )(P, P, P, past_k, past_v)


def _kv_layout_kernel(kv_ref, kp_ref, ks_ref, vp_ref, vs_ref, *, n_first):
    tm = kv_ref.shape[0]
    first = pl.program_id(0) < n_first

    def relayout(k_out, v_out):
        for which, out in enumerate((k_out, v_out)):
            for h in range(NH_B):
                out[pl.ds(h, tm, stride=NH_B), :] = kv_ref[:, which * D_B + h * DH_B:which * D_B + (h + 1) * DH_B]

    pl.when(first)(functools.partial(relayout, kp_ref, vp_ref))
    pl.when(jnp.logical_not(first))(functools.partial(relayout, ks_ref, vs_ref))


def kv_outputs(P, Tp, tm):
    T = P.shape[0]
    n_first = Tp // tm
    outs = _two_stream_specs(tm * NH_B, DH_B, n_first)
    shapes = [jax.ShapeDtypeStruct((Tp * NH_B, DH_B), F32), jax.ShapeDtypeStruct(((T - Tp) * NH_B, DH_B), F32)]
    return pl.pallas_call(
        functools.partial(_kv_layout_kernel, n_first=n_first),
        out_shape=tuple(shapes + shapes),
        grid=(T // tm,),
        in_specs=[pl.BlockSpec((tm, 2 * D_B), lambda i: (i, 2))],
        out_specs=tuple(outs + outs),
        compiler_params=_params(("arbitrary",)),
        name="kv_layout",
    )(P)


def _merge_kernel(h_ref, ha_ref, hb_ref, wga_ref, wgb_ref, wpa_ref, wpb_ref, o_ref):
    h = h_ref[...]
    ga = jnp.dot(h, wga_ref[...], preferred_element_type=F32)
    gb = jnp.dot(h, wgb_ref[...], preferred_element_type=F32)
    pa = jnp.dot(ha_ref[...], wpa_ref[...], preferred_element_type=F32)
    pb = jnp.dot(hb_ref[...], wpb_ref[...], preferred_element_type=F32)
    o_ref[...] = (jax.nn.sigmoid(ga) * pa + jax.nn.sigmoid(gb) * pb).astype(o_ref.dtype)


def merge(h, row0, ha, hb, w_rest, col_ga, col_gb, wpa, wpb):
    Tn = ha.shape[0]
    D = h.shape[1]
    tm = _pick(math.gcd(Tn, row0) if row0 else Tn, (1024, 512, 256, 128, 64, 32, 16))
    tn = _pick(D, (512, 256, 128))
    row = lambda i, j: (i, 0)
    colw = lambda i, j: (0, j)
    col_at = lambda c0: (lambda i, j: (0, c0 // tn + j))
    return pl.pallas_call(
        _merge_kernel,
        out_shape=jax.ShapeDtypeStruct((Tn, D), BF16),
        grid=(Tn // tm, D // tn),
        in_specs=[pl.BlockSpec((tm, D), lambda i, j: (row0 // tm + i, 0)),
                  pl.BlockSpec((tm, D_A), row), pl.BlockSpec((tm, D_B), row),
                  pl.BlockSpec((D, tn), col_at(col_ga)), pl.BlockSpec((D, tn), col_at(col_gb)),
                  pl.BlockSpec((D_A, tn), colw), pl.BlockSpec((D_B, tn), colw)],
        out_specs=pl.BlockSpec((tm, tn), lambda i, j: (i, j)),
        compiler_params=_params(("parallel", "arbitrary")),
        name="merge",
    )(h, ha, hb, w_rest, w_rest, wpa, wpb)


def _wout_route_kernel(mgp_ref, mgs_ref, xp_ref, xs_ref, wo_ref, g_ref, wr_ref, br_ref, x1_ref, t_ref, r_ref,
                       cnt_ref, run_s, *, n_first):
    i = pl.program_id(0)

    @pl.when(i == 0)
    def _():
        run_s[...] = jnp.zeros_like(run_s)

    x = jnp.where(i < n_first, xp_ref[...], xs_ref[...])
    mg = jnp.where(i < n_first, mgp_ref[...], mgs_ref[...])
    x1 = x + jnp.dot(mg, wo_ref[...], preferred_element_type=F32)
    x1_ref[...] = x1
    t = _rms(x1, g_ref[...])
    _store_row_major(t_ref, t)
    lg = jnp.dot(t.astype(BF16), wr_ref[...], preferred_element_type=F32) + br_ref[...]
    tm = lg.shape[0]
    lane = lax.broadcasted_iota(jnp.int32, (tm, LANES), 1)
    ninf = -jnp.inf
    glm = jnp.where(lane < N_GROUPS, lg, ninf)
    gmax = jnp.max(glm, axis=1, keepdims=True)
    gsel = jnp.min(jnp.where(glm == gmax, lane, LANES), axis=1, keepdims=True)
    p_sel = 1.0 / jnp.sum(jnp.exp(glm - gmax), axis=1, keepdims=True)
    lo = N_GROUPS + E_PER_GROUP * gsel
    elm = jnp.where((lane >= lo) & (lane < lo + E_PER_GROUP), lg, ninf)
    m1 = jnp.max(elm, axis=1, keepdims=True)
    i1 = jnp.min(jnp.where(elm == m1, lane, LANES), axis=1, keepdims=True)
    elm2 = jnp.where(lane == i1, ninf, elm)
    m2 = jnp.max(elm2, axis=1, keepdims=True)
    i2 = jnp.min(jnp.where(elm2 == m2, lane, LANES), axis=1, keepdims=True)
    e2 = jnp.exp(m2 - m1)
    w1 = p_sel / (1.0 + e2)
    w2 = p_sel * e2 / (1.0 + e2)
    e1 = (lane == i1 - N_GROUPS).astype(F32)
    e2h = (lane == i2 - N_GROUPS).astype(F32)
    both = e1 + e2h
    rr = lax.broadcasted_iota(jnp.int32, (tm, tm), 0)
    cc = lax.broadcasted_iota(jnp.int32, (tm, tm), 1)
    before = jnp.dot((cc < rr).astype(BF16), both.astype(BF16), preferred_element_type=F32) + run_s[...]
    rank1 = jnp.sum(e1 * before, axis=1, keepdims=True)
    rank2 = jnp.sum(e2h * before, axis=1, keepdims=True)
    run_s[...] = run_s[...] + jnp.sum(both, axis=0, keepdims=True)
    cnt_ref[...] = run_s[...]
    r = jnp.where(lane == 0, (i1 - N_GROUPS).astype(F32),
        jnp.where(lane == 1, (i2 - N_GROUPS).astype(F32),
        jnp.where(lane == 2, w1, jnp.where(lane == 3, w2,
        jnp.where(lane == 4, rank1, jnp.where(lane == 5, rank2, 0.0))))))
    r_ref[...] = r


def wout_route(mg_p, mg_s, xp, xs, wo, g_ffn, wr, br, tm):
    (Tp, D), Ts = xp.shape, xs.shape[0]
    T = Tp + Ts
    row = lambda i: (i, 0)
    fix = lambda i: (0, 0)
    return pl.pallas_call(
        functools.partial(_wout_route_kernel, n_first=Tp // tm),
        out_shape=(jax.ShapeDtypeStruct((T, D), F32), jax.ShapeDtypeStruct((T * ROW_PITCH, LANES), F32),
                   jax.ShapeDtypeStruct((T, LANES), F32), jax.ShapeDtypeStruct((1, LANES), F32)),
        grid=(T // tm,),
        in_specs=_two_stream_specs(tm, D, Tp // tm) + _two_stream_specs(tm, D, Tp // tm)
                 + [pl.BlockSpec((D, D), fix), pl.BlockSpec((1, D), fix), pl.BlockSpec((D, LANES), fix),
                    pl.BlockSpec((1, LANES), fix)],
        out_specs=(pl.BlockSpec((tm, D), row), pl.BlockSpec((tm * ROW_PITCH, LANES), row),
                   pl.BlockSpec((tm, LANES), row),
                   pl.BlockSpec((1, LANES), fix)),
        scratch_shapes=[pltpu.VMEM((1, LANES), F32)],
        compiler_params=_params(("arbitrary",)),
        name="wout_route",
    )(mg_p, mg_s, xp, xs, wo, g_ffn.reshape(1, D), wr, br)


IDX_RING = 8


def _expert_kernel(te_ref, nu_ref, src_hbm, dst_hbm, t_hbm, wg_ref, wu_ref, wd_ref, y_hbm,
                   sidx, didx, xb0, xb1, yb0, yb1, wg_s, wu_s, wd_s, gsem, ssem, isem, *, dump0):
    i = pl.program_id(0)
    nu = nu_ref[0]
    TE = EXPERT_TILE
    xb, yb = (xb0, xb1), (yb0, yb1)

    def table_copies(tile, seq):
        base = (seq & (IDX_RING - 1)) * TE
        return (pltpu.make_async_copy(src_hbm.at[pl.ds(tile * TE, TE)], sidx.at[pl.ds(base, TE)], isem.at[0]),
                pltpu.make_async_copy(dst_hbm.at[pl.ds(tile * TE, TE)], didx.at[pl.ds(base, TE)], isem.at[1]))

    def gather(seq, par):
        base = (seq & (IDX_RING - 1)) * TE
        return [pltpu.make_async_copy(t_hbm.at[pl.ds(sidx[base + r], ROW_CHUNKS), :],
                                      xb[par].at[pl.ds(r * ROW_PITCH, ROW_CHUNKS), :],
                                      gsem.at[par]) for r in range(TE)]

    def scatter(seq, par):
        base = (seq & (IDX_RING - 1)) * TE
        return [pltpu.make_async_copy(yb[par].at[pl.ds(r * ROW_PITCH, ROW_PITCH), :],
                                      y_hbm.at[pl.ds(didx[base + r], ROW_PITCH), :],
                                      ssem.at[par]) for r in range(TE)]

    def start(cps):
        rows = len(cps) == TE
        for n, cp in enumerate(cps):
            cp.start(priority=n % 2 if rows else 0)

    def wait(cps):
        for cp in cps:
            cp.wait()

    last = nu - 1

    @pl.when(i == 0)
    def _():
        first = table_copies(0, 0)
        start(first)
        wait(first)
        start(table_copies(jnp.minimum(1, last), 1))
        start(gather(0, 0))
        yb0[...] = jnp.zeros_like(yb0)
        yb1[...] = jnp.zeros_like(yb1)
        for r in range(TE):
            didx[(IDX_RING - 1) * TE + r] = (dump0 + TE + r) * ROW_PITCH
        fill = pltpu.make_async_copy(yb1, y_hbm.at[pl.ds(dump0 * ROW_PITCH, TE * ROW_PITCH), :], ssem.at[0])
        fill.start()
        fill.wait()

    @pl.when((i < nu) & ((i == 0) | (te_ref[i] != te_ref[jnp.maximum(i - 1, 0)])))
    def _():
        wg_s[...] = wg_ref[...].astype(BF16)
        wu_s[...] = wu_ref[...].astype(BF16)
        wd_s[...] = wd_ref[...].astype(BF16)

    def step(par):
        wait(table_copies(jnp.minimum(i + 1, last), i + 1))
        start(table_copies(jnp.minimum(i + 2, last), i + 2))

        @pl.when(i >= 1)
        def _():
            wait(scatter(i - 2, par))

        wait(gather(i, par))
        start(scatter(i - 1, 1 - par))
        start(gather(i + 1, 1 - par))
        x = _load_row_major(xb[par], TE).astype(BF16)
        a = jnp.dot(x, wg_s[...], preferred_element_type=F32)
        u = jnp.dot(x, wu_s[...], preferred_element_type=F32)
        hid = (a * jax.nn.sigmoid(a) * u).astype(BF16)
        y = jnp.dot(hid, wd_s[...], preferred_element_type=F32)
        for j in range(ROW_CHUNKS):
            yb[par][pl.ds(j, TE, stride=ROW_PITCH), :] = y[:, j * LANES:(j + 1) * LANES]

        @pl.when(i == last)
        def _():
            start(scatter(i, par))
            wait(scatter(i - 1, 1 - par))
            wait(scatter(i, par))
            wait(gather(i + 1, 1 - par))
            wait(table_copies(last, i + 2))

    for par in (0, 1):
        pl.when((i < nu) & (i % 2 == par))(functools.partial(step, par))


def experts(t, slot_src, slot_dst, tile_expert, n_used, w_gate, w_up, w_down):
    T = t.shape[0] // ROW_PITCH
    D, F = w_gate.shape[-2:]
    nt = slot_src.shape[0] // EXPERT_TILE
    wmap = lambda i, te, nu: (te[i], 0, 0)
    any_spec = pl.BlockSpec(memory_space=pl.ANY)
    tile_buf = pltpu.VMEM((EXPERT_TILE * ROW_PITCH, LANES), F32)
    grid_spec = pltpu.PrefetchScalarGridSpec(
        num_scalar_prefetch=2,
        grid=(nt,),
        in_specs=[any_spec, any_spec, any_spec,
                  pl.BlockSpec((None, D, F), wmap), pl.BlockSpec((None, D, F), wmap),
                  pl.BlockSpec((None, F, D), wmap)],
        out_specs=any_spec,
        scratch_shapes=[pltpu.SMEM((IDX_RING * EXPERT_TILE,), jnp.int32),
                        pltpu.SMEM((IDX_RING * EXPERT_TILE,), jnp.int32),
                        tile_buf, tile_buf, tile_buf, tile_buf,
                        pltpu.VMEM((D, F), BF16), pltpu.VMEM((D, F), BF16), pltpu.VMEM((F, D), BF16),
                        pltpu.SemaphoreType.DMA((2,)), pltpu.SemaphoreType.DMA((2,)),
                        pltpu.SemaphoreType.DMA((2,))],
    )
    return pl.pallas_call(
        functools.partial(_expert_kernel, dump0=2 * T),
        out_shape=jax.ShapeDtypeStruct(((2 * T + 2 * EXPERT_TILE) * ROW_PITCH, LANES), F32),
        grid_spec=grid_spec,
        compiler_params=_params(("arbitrary",)),
        name="moe_experts",
    )(tile_expert, n_used, slot_src, slot_dst, t, w_gate.reshape(N_EXPERTS, D, F),
      w_up.reshape(N_EXPERTS, D, F), w_down.reshape(N_EXPERTS, F, D))


def _combine_kernel(x1_ref, y0_ref, y1_ref, r_ref, pp_ref, ps_ref, wple_ref, wpg_ref, gple_ref, gfin_ref,
                    yp_ref, ys_ref, *, n_first):
    i = pl.program_id(0)
    p = jnp.where(i < n_first, pp_ref[...], ps_ref[...])
    r = r_ref[...]
    tm = r.shape[0]
    x2 = x1_ref[...] + r[:, 2:3] * _load_row_major(y0_ref, tm) + r[:, 3:4] * _load_row_major(y1_ref, tm)
    hp = _rms(x2, gple_ref[...]).astype(BF16)
    gate = jax.nn.sigmoid(jnp.dot(hp, wpg_ref[...], preferred_element_type=F32))
    emb = jnp.dot(p.astype(BF16), wple_ref[...], preferred_element_type=F32)
    y = _rms(x2 + emb * gate, gfin_ref[...])

    @pl.when(i < n_first)
    def _():
        yp_ref[...] = y

    @pl.when(i >= n_first)
    def _():
        ys_ref[...] = y


def combine(y_flat, x1, route, p_p, p_s, w_ple, w_ple_gate, g_ple, g_final, tm):
    T, D = x1.shape
    row = lambda i: (i, 0)
    row1 = lambda i: (i + T // tm, 0)
    fix = lambda i: (0, 0)
    Tp = p_p.shape[0]
    n_first = Tp // tm
    return pl.pallas_call(
        functools.partial(_combine_kernel, n_first=n_first),
        out_shape=(jax.ShapeDtypeStruct((Tp, D), F32), jax.ShapeDtypeStruct((T - Tp, D), F32)),
        grid=(T // tm,),
        in_specs=[pl.BlockSpec((tm, D), row), pl.BlockSpec((tm * ROW_PITCH, LANES), row),
                  pl.BlockSpec((tm * ROW_PITCH, LANES), row1),
                  pl.BlockSpec((tm, LANES), row)] + _two_stream_specs(tm, p_p.shape[1], n_first)
                 + [pl.BlockSpec(w_ple.shape, fix), pl.BlockSpec((D, D), fix),
                  pl.BlockSpec((1, D), fix), pl.BlockSpec((1, D), fix)],
        out_specs=tuple(_two_stream_specs(tm, D, n_first)),
        compiler_params=_params(("arbitrary",)),
        name="moe_combine_ple",
    )(x1, y_flat, y_flat, route, p_p, p_s, w_ple, w_ple_gate, g_ple.reshape(1, D), g_final.reshape(1, D))


def _routing_tables(route, counts, n_tiles):
    ids = route[:, 0:2].astype(jnp.int32).reshape(-1)
    rank = route[:, 4:6].astype(jnp.int32).reshape(-1)
    counts = counts.astype(jnp.int32)
    tiles = (counts + EXPERT_TILE - 1) // EXPERT_TILE
    tile_end = jnp.cumsum(tiles)
    tile_start = tile_end - tiles
    onehot = ids[:, None] == jnp.arange(N_EXPERTS, dtype=jnp.int32)[None, :]
    pos = jnp.sum(jnp.where(onehot, tile_start[None, :], 0), axis=1) * EXPERT_TILE + rank
    slot_assign = jnp.full((n_tiles * EXPERT_TILE,), -1, jnp.int32).at[pos].set(
        jnp.arange(ids.shape[0], dtype=jnp.int32), unique_indices=True)
    n_assign = ids.shape[0]
    slot = jnp.arange(n_tiles * EXPERT_TILE, dtype=jnp.int32)
    token = jnp.maximum(slot_assign, 0) // 2
    slot_src = token * ROW_PITCH
    dump = n_assign + ((slot // EXPERT_TILE) % 2) * EXPERT_TILE + slot % EXPERT_TILE
    slot_dst = jnp.where(slot_assign >= 0, (slot_assign % 2) * (n_assign // 2) + token, dump) * ROW_PITCH
    n_used = tile_end[-1]
    tidx = jnp.minimum(jnp.arange(n_tiles, dtype=jnp.int32), n_used - 1)
    tile_expert = jnp.sum((tidx[:, None] >= tile_end[None, :]).astype(jnp.int32), axis=1)
    return slot_src, slot_dst, tile_expert.astype(jnp.int32), n_used.reshape(1).astype(jnp.int32)


def kernel(x_prompt, x_sample, cache_k, cache_v, state_conv, state_C, state_n, state_m, p_prompt, p_sample,
           g_mix, w_in, b_if, conv_w, conv_b, g_head_a, w_proj_a, w_proj_b, w_out, g_ffn, w_route_g,
           b_route_g, w_route_e, b_route_e, w_exp_gate, w_exp_up, w_exp_down, g_ple, w_ple, w_ple_gate,
           g_final):
    assert w_in.shape[0] == 1, "single layer"
    B, S, D = x_prompt.shape
    DB, DS, _ = x_sample.shape
    Tp, Ts = B * S, DB * DS
    T = Tp + Ts
    xp, xs = x_prompt.reshape(Tp, D), x_sample.reshape(Ts, D)
    tm = _pick(math.gcd(Tp, Ts), (256, 128, 64, 32, 16, 8))

    wi = w_in[0]
    g0 = 2 * QK_A + D_A
    m0 = g0 + 2 * NH_A
    e0 = m0 + D_A + 3 * D_B
    w_rest = wi[:, m0:].astype(BF16)
    w_gates = jnp.zeros((D, LANES), F32).at[:, :2 * NH_A].set(wi[:, g0:m0]).astype(BF16)

    h = rms_cast(xp, xs, g_mix[0], tm)
    P = proj_main(h, wi, g0, w_rest, e0 - m0)
    G = matmul(h, w_gates, "proj_gates")

    zeros = functools.partial(jnp.zeros, dtype=F32)
    ha_p, C_p, n_p, m_p, cv_p = mlstm(P, G, 0, B, S, b_if[0], conv_w[0], conv_b[0], g_head_a[0],
                                      zeros((B, CONV_W - 1, 2 * QK_A)), zeros((B, NH_A, DQK_A, DV_A)),
                                      zeros((B, NH_A, DQK_A)), zeros((B, NH_A)))
    ha_s, C_s, n_s, m_s, cv_s = mlstm(P, G, Tp, DB, DS, b_if[0], conv_w[0], conv_b[0], g_head_a[0],
                                      state_conv[0], state_C[0], state_n[0], state_m[0])
    hb_p = sb_attention(P, 0, B, S)
    hb_s = sb_attention_past(P, Tp, DB, DS, cache_k, cache_v)
    wpa, wpb = w_proj_a[0].astype(BF16), w_proj_b[0].astype(BF16)
    mg_p = merge(h, 0, ha_p, hb_p, w_rest, e0 - m0, e0 - m0 + D, wpa, wpb)
    mg_s = merge(h, Tp, ha_s, hb_s, w_rest, e0 - m0, e0 - m0 + D, wpa, wpb)

    wr = jnp.zeros((D, LANES), F32)
    wr = wr.at[:, :N_GROUPS].set(w_route_g[0])
    wr = wr.at[:, N_GROUPS:N_GROUPS + N_EXPERTS].set(
        jnp.transpose(w_route_e[0], (1, 0, 2)).reshape(D, N_EXPERTS))
    br = jnp.zeros((1, LANES), F32)
    br = br.at[0, :N_GROUPS].set(b_route_g[0])
    br = br.at[0, N_GROUPS:N_GROUPS + N_EXPERTS].set(b_route_e[0].reshape(-1))
    x1, t, route, counts = wout_route(mg_p, mg_s, xp, xs, w_out[0].astype(BF16), g_ffn[0], wr.astype(BF16), br, tm)

    n_tiles = (2 * T + N_EXPERTS * (EXPERT_TILE - 1)) // EXPERT_TILE + 1
    slot_src, slot_dst, tile_expert, n_used = _routing_tables(route, counts[0, :N_EXPERTS], n_tiles)
    y_flat = experts(t, slot_src, slot_dst, tile_expert, n_used, w_exp_gate[0], w_exp_up[0], w_exp_down[0])
    y_p, y_s = combine(y_flat, x1, route, p_prompt[0].reshape(Tp, -1), p_sample[0].reshape(Ts, -1),
                       w_ple[0].astype(BF16), w_ple_gate[0].astype(BF16), g_ple[0], g_final, tm)

    k_p, k_s, v_p, v_s = kv_outputs(P, Tp, tm)
    return (y_p.reshape(B, S, D), y_s.reshape(DB, DS, D),
            k_p.reshape(1, B, S, NH_B, DH_B), v_p.reshape(1, B, S, NH_B, DH_B),
            cv_p[None], C_p[None], n_p[None], m_p[None],
            k_s.reshape(1, DB, DS, NH_B, DH_B), v_s.reshape(1, DB, DS, NH_B, DH_B),
            cv_s[None], C_s[None], n_s[None], m_s[None])
```

```python
import functools
import math

import jax
import jax.numpy as jnp
from jax import lax
from jax.experimental import pallas as pl
from jax.experimental.pallas import tpu as pltpu

F32 = jnp.float32
BF16 = jnp.bfloat16
EPS = 1e-6

CHUNK = 128
NH_A, DQK_A, DV_A = 4, 128, 256
QK_A, D_A = NH_A * DQK_A, NH_A * DV_A
CONV_W = 4
NH_B, DH_B = 8, 128
D_B = NH_B * DH_B
N_GROUPS, E_PER_GROUP = 4, 8
N_EXPERTS = N_GROUPS * E_PER_GROUP
LANES = 128
EXPERT_TILE = 256
VMEM_LIMIT = 56 * 1024 * 1024


def _pick(n, cands):
    for c in cands:
        if n % c == 0:
            return c
    raise ValueError(f"no tile for {n} in {cands}")


def _params(sem):
    return pltpu.CompilerParams(dimension_semantics=sem, vmem_limit_bytes=VMEM_LIMIT)


def _rms(x, g):
    return x * lax.rsqrt(jnp.mean(x * x, axis=-1, keepdims=True) + EPS) * g


ROW_CHUNKS = 2048 // LANES
ROW_PITCH = 17


def _store_row_major(ref, x):
    rows = x.shape[0]
    for j in range(ROW_PITCH):
        piece = x[:, j * LANES:(j + 1) * LANES] if j < ROW_CHUNKS else jnp.zeros((rows, LANES), x.dtype)
        ref[pl.ds(j, rows, stride=ROW_PITCH), :] = piece


def _load_row_major(ref, rows):
    return jnp.concatenate([ref[pl.ds(j, rows, stride=ROW_PITCH), :] for j in range(ROW_CHUNKS)], axis=1)


def _log_sigmoid(x):
    return jnp.minimum(x, 0.0) - jnp.log(1.0 + jnp.exp(-jnp.abs(x)))


def _two_stream_specs(tm, D, n_first):
    return [pl.BlockSpec((tm, D), lambda i: (jnp.minimum(i, n_first - 1), 0)),
            pl.BlockSpec((tm, D), lambda i: (jnp.maximum(i - n_first, 0), 0))]


def _rms_kernel(xp_ref, xs_ref, g_ref, o_ref, *, n_first):
    x = jnp.where(pl.program_id(0) < n_first, xp_ref[...], xs_ref[...])
    o_ref[...] = _rms(x, g_ref[...]).astype(o_ref.dtype)


def rms_cast(xp, xs, g, tm):
    (Tp, D), Ts = xp.shape, xs.shape[0]
    T = Tp + Ts
    return pl.pallas_call(
        functools.partial(_rms_kernel, n_first=Tp // tm),
        out_shape=jax.ShapeDtypeStruct((T, D), BF16),
        grid=(T // tm,),
        in_specs=_two_stream_specs(tm, D, Tp // tm) + [pl.BlockSpec((1, D), lambda i: (0, 0))],
        out_specs=pl.BlockSpec((tm, D), lambda i: (i, 0)),
        compiler_params=_params(("arbitrary",)),
        name="rms_cast",
    )(xp, xs, g.reshape(1, D))


def _mm_kernel(a_ref, w_ref, o_ref):
    o_ref[...] = jnp.dot(a_ref[...], w_ref[...], preferred_element_type=F32)


def matmul(a, w, name):
    T, K = a.shape
    N = w.shape[1]
    tm = _pick(T, (1280, 640, 512, 256, 128, 64, 32, 16))
    tn = _pick(N, (1024, 512, 256, 128))
    return pl.pallas_call(
        _mm_kernel,
        out_shape=jax.ShapeDtypeStruct((T, N), F32),
        grid=(T // tm, N // tn),
        in_specs=[pl.BlockSpec((tm, K), lambda i, j: (i, 0)),
                  pl.BlockSpec((K, tn), lambda i, j: (0, j))],
        out_specs=pl.BlockSpec((tm, tn), lambda i, j: (i, j)),
        compiler_params=_params(("parallel", "arbitrary")),
        name=name,
    )(a, w)


def _mlstm_kernel(*refs, c, nc, nb):
    qk_refs, v_refs, o_refs, g_refs = (refs[k * nb:(k + 1) * nb] for k in range(4))
    (bif_ref, cw_ref, cb_ref, gh_ref, cbuf_ref, c0_ref, n0_ref, m0_ref,
     h_ref, cout_ref, nout_ref, mout_ref, convout_ref, xbuf, c_s, n_s, m_s) = refs[4 * nb:]
    j = pl.program_id(1)

    @pl.when(j == 0)
    def _():
        c_s[...] = c0_ref[...]
        n_s[...] = n0_ref[...]
        m_s[...] = m0_ref[...]
        xbuf[:, 5:8, :] = cbuf_ref[...]

    row = lax.broadcasted_iota(jnp.int32, (c, c), 0)
    col = lax.broadcasted_iota(jnp.int32, (c, c), 1)
    consts = dict(lane=lax.broadcasted_iota(jnp.int32, (c, LANES), 1), tril=(col <= row).astype(F32),
                  eye=(col == row).astype(F32), triu=(row <= col).astype(F32), causal=col <= row)
    last_rows = [
        _mlstm_chunk(qk_refs[b], v_refs[b], o_refs[b], g_refs[b], bif_ref, cw_ref, cb_ref, gh_ref,
                     h_ref.at[b], xbuf.at[b], c_s.at[b], n_s.at[b], m_s.at[b], consts, c)
        for b in range(nb)]

    @pl.when(j == nc - 1)
    def _():
        cout_ref[...] = c_s[...]
        nout_ref[...] = n_s[...]
        mout_ref[...] = m_s[...]
        for b in range(nb):
            convout_ref[b] = last_rows[b]


def _mlstm_chunk(qk_ref, v_ref, o_ref, g_ref, bif_ref, cw_ref, cb_ref, gh_ref, h_ref, xbuf, c_s, n_s, m_s,
                 consts, c):
    lane, tril, eye, triu, causal = (consts[k] for k in ("lane", "tril", "eye", "triu", "causal"))
    xbuf[8:8 + c, :] = qk_ref[...]
    y = cb_ref[...] + cw_ref[0:1, :] * xbuf[5:5 + c, :]
    for t in range(1, CONV_W):
        y = y + cw_ref[t:t + 1, :] * xbuf[5 + t:5 + t + c, :]
    last_rows = xbuf[c + 5:c + 8, :]
    xbuf[5:8, :] = last_rows
    qk = y * jax.nn.sigmoid(y)

    graw = g_ref[...] + bif_ref[...]
    xg = jnp.where(lane < NH_A, graw, jnp.where(lane < 2 * NH_A, _log_sigmoid(graw), 0.0))
    hp = lax.Precision.HIGHEST
    bc = jnp.dot(tril, xg, precision=hp, preferred_element_type=F32)
    dn0 = (((0,), (0,)), ((), ()))
    xt = lax.dot_general(xg, eye, dn0, precision=hp, preferred_element_type=F32)
    bt = lax.dot_general(xg, triu, dn0, precision=hp, preferred_element_type=F32)

    for h in range(NH_A):
        q = qk[:, h * DQK_A:(h + 1) * DQK_A]
        k = qk[:, QK_A + h * DQK_A:QK_A + (h + 1) * DQK_A] * (DQK_A ** -0.5)
        v = v_ref[:, h * DV_A:(h + 1) * DV_A]
        qb, kb, vb = q.astype(BF16), k.astype(BF16), v.astype(BF16)
        i_col = xg[:, h:h + 1]
        b_col = bc[:, NH_A + h:NH_A + h + 1]
        i_row = xt[h:h + 1, :]
        b_row = bt[NH_A + h:NH_A + h + 1, :]
        m_prev = m_s[0:1, h:h + 1]
        C = c_s[h]
        nvec = n_s[h:h + 1, :]

        dmat = jnp.where(causal, b_col - b_row + i_row, -jnp.inf)
        inter = b_col + m_prev
        m_t = jnp.maximum(inter, jnp.max(dmat, axis=1, keepdims=True))
        e = jnp.exp(dmat - m_t)
        s = lax.dot_general(qb, kb, (((1,), (1,)), ((), ())), preferred_element_type=F32) * e
        w_inter = jnp.exp(inter - m_t)
        num = (jnp.dot(s.astype(BF16), vb, preferred_element_type=F32)
               + w_inter * jnp.dot(qb, C.astype(BF16), preferred_element_type=F32))
        den = jnp.sum(s, axis=1, keepdims=True) + w_inter * jnp.sum(q * nvec, axis=1, keepdims=True)
        hh = num / jnp.maximum(jnp.abs(den), jnp.exp(-m_t))
        hh = hh * lax.rsqrt(jnp.mean(hh * hh, axis=1, keepdims=True) + EPS)
        hh = hh * gh_ref[:, h * DV_A:(h + 1) * DV_A]
        hh = jax.nn.sigmoid(o_ref[:, h * DV_A:(h + 1) * DV_A]) * hh
        h_ref[:, h * DV_A:(h + 1) * DV_A] = hh.astype(h_ref.dtype)

        b_end = b_col[c - 1:c, :]
        dec = b_end - b_col + i_col
        m_new = jnp.maximum(b_end + m_prev, jnp.max(dec, axis=0, keepdims=True))
        wk = jnp.exp(dec - m_new)
        w_old = jnp.exp(b_end + m_prev - m_new)
        kw = k * wk
        c_s[h] = w_old * C + lax.dot_general(kw.astype(BF16), vb, dn0, preferred_element_type=F32)
        n_s[h:h + 1, :] = w_old * nvec + jnp.sum(kw, axis=0, keepdims=True)
        m_s[0:1, h:h + 1] = m_new
    return last_rows


MLSTM_SEQS = 2


def mlstm(P, G, row0, N, L, b_if, conv_w, conv_b, g_head, conv_buf, C0, n0, m0):
    c = min(L, CHUNK)
    nc = L // c
    nb = MLSTM_SEQS if N % MLSTM_SEQS == 0 else 1
    base = row0 // c
    m0p = jnp.zeros((N, 1, LANES), F32).at[:, 0, :NH_A].set(m0)
    bif = jnp.zeros((1, LANES), F32).at[0, :2 * NH_A].set(b_if)

    def rows(width, col):
        return [pl.BlockSpec((c, width), lambda n, j, b=b: (base + (n * nb + b) * nc + j, col))
                for b in range(nb)]
    fix = lambda n, j: (0, 0)
    st4 = lambda n, j: (n, 0, 0, 0)
    st3 = lambda n, j: (n, 0, 0)
    outs = pl.pallas_call(
        functools.partial(_mlstm_kernel, c=c, nc=nc, nb=nb),
        out_shape=(jax.ShapeDtypeStruct((N, L, D_A), BF16),
                   jax.ShapeDtypeStruct((N, NH_A, DQK_A, DV_A), F32),
                   jax.ShapeDtypeStruct((N, NH_A, DQK_A), F32),
                   jax.ShapeDtypeStruct((N, 1, LANES), F32),
                   jax.ShapeDtypeStruct((N, CONV_W - 1, 2 * QK_A), F32)),
        grid=(N // nb, nc),
        in_specs=rows(2 * QK_A, 0) + rows(D_A, 1) + rows(D_A, 2) + rows(LANES, 0)
                 + [pl.BlockSpec((1, LANES), fix),
                    pl.BlockSpec((CONV_W, 2 * QK_A), fix),
                    pl.BlockSpec((1, 2 * QK_A), fix),
                    pl.BlockSpec((1, D_A), fix),
                    pl.BlockSpec((nb, CONV_W - 1, 2 * QK_A), st3),
                    pl.BlockSpec((nb, NH_A, DQK_A, DV_A), st4),
                    pl.BlockSpec((nb, NH_A, DQK_A), st3),
                    pl.BlockSpec((nb, 1, LANES), st3)],
        out_specs=(pl.BlockSpec((nb, c, D_A), lambda n, j: (n, j, 0)),
                   pl.BlockSpec((nb, NH_A, DQK_A, DV_A), st4),
                   pl.BlockSpec((nb, NH_A, DQK_A), st3),
                   pl.BlockSpec((nb, 1, LANES), st3),
                   pl.BlockSpec((nb, CONV_W - 1, 2 * QK_A), st3)),
        scratch_shapes=[pltpu.VMEM((nb, c + 8, 2 * QK_A), F32),
                        pltpu.VMEM((nb, NH_A, DQK_A, DV_A), F32),
                        pltpu.VMEM((nb, NH_A, DQK_A), F32),
                        pltpu.VMEM((nb, 1, LANES), F32)],
        compiler_params=_params(("parallel", "arbitrary")),
        name="mlstm",
    )(*([P] * (3 * nb) + [G] * nb), bif, conv_w, conv_b.reshape(1, -1), g_head.reshape(1, -1),
      conv_buf, C0, n0, m0p)
    h, C1, n1, m1, cv = outs
    return h.reshape(N * L, D_A), C1, n1, m1[:, 0, :NH_A], cv


STICK_UNDERFLOW = -110.0


def _later_matrix(tk):
    jr = lax.broadcasted_iota(jnp.int32, (tk, tk), 0)
    sc = lax.broadcasted_iota(jnp.int32, (tk, tk), 1)
    return (jr > sc).astype(BF16)


def _sb_blocks(q_bfs, k_blks, v_blks, Rs, later, mask):
    tq = q_bfs[0].shape[0]
    zs, l1mbs, parts = [], [], []
    for q_bf, k_blk in zip(q_bfs, k_blks):
        z = lax.dot_general(q_bf, k_blk.astype(BF16), (((1,), (1,)), ((), ())),
                            preferred_element_type=F32) * (DH_B ** -0.5)
        l1mb = -(jnp.maximum(z, 0.0) + jnp.log(1.0 + jnp.exp(-jnp.abs(z))))
        if mask is not None:
            l1mb = jnp.where(mask, l1mb, 0.0)
        hi = l1mb.astype(BF16)
        parts += [hi, (l1mb - hi.astype(F32)).astype(BF16)]
        zs.append(z)
        l1mbs.append(l1mb)
    suffix = jnp.dot(jnp.concatenate(parts, axis=0), later, preferred_element_type=F32)
    pvs, new_Rs = [], []
    for h, (z, l1mb) in enumerate(zip(zs, l1mbs)):
        rest = suffix[2 * h * tq:(2 * h + 1) * tq] + suffix[(2 * h + 1) * tq:(2 * h + 2) * tq] + Rs[h]
        a = jnp.exp(z + l1mb + rest)
        if mask is not None:
            a = jnp.where(mask, a, 0.0)
        pvs.append(jnp.dot(a.astype(BF16), v_blks[h].astype(BF16), preferred_element_type=F32))
        new_Rs.append(Rs[h] + jnp.sum(l1mb, axis=1, keepdims=True))
    return pvs, new_Rs


SB_HEADS = 4


def _sb_kernel(q_ref, k_ref, v_ref, o_ref, *, tq):
    qi = pl.program_id(2)
    rr = lax.broadcasted_iota(jnp.int32, (tq, tq), 0)
    cc = lax.broadcasted_iota(jnp.int32, (tq, tq), 1)
    later = _later_matrix(tq)
    cols = [slice(h * DH_B, (h + 1) * DH_B) for h in range(SB_HEADS)]
    q_bf = [q_ref[:, c].astype(BF16) for c in cols]
    def blocks(st, Rs, mask):
        return _sb_blocks(q_bf, [k_ref[pl.ds(st, tq), c] for c in cols], [v_ref[pl.ds(st, tq), c] for c in cols],
                          Rs, later, mask)

    accs, Rs = blocks(pl.multiple_of(qi * tq, tq), [jnp.zeros((tq, 1), F32)] * SB_HEADS, cc < rr)

    def rmax(Rs):
        return jnp.max(jnp.concatenate(Rs, axis=1))

    def cond(c):
        return jnp.logical_and(c[0] < qi, c[3] > STICK_UNDERFLOW)

    def body(carry):
        t, accs, Rs, _ = carry
        pvs, new_Rs = blocks(pl.multiple_of((qi - 1 - t) * tq, tq), list(Rs), None)
        new_accs = [acc + pv for acc, pv in zip(accs, pvs)]
        return t + 1, tuple(new_accs), tuple(new_Rs), rmax(new_Rs)

    _, accs, _, _ = lax.while_loop(cond, body, (jnp.int32(0), tuple(accs), tuple(Rs), rmax(Rs)))
    for h, c in enumerate(cols):
        o_ref[:, c] = accs[h].astype(o_ref.dtype)


QB_BLK, KB_BLK, VB_BLK = 3 * D_B // DH_B, 4 * D_B // DH_B, 5 * D_B // DH_B


def sb_attention(P, row0, N, L):
    tq = min(L, 256)
    nq = L // tq
    W = SB_HEADS * DH_B
    qb, kb, vb = QB_BLK // SB_HEADS, KB_BLK // SB_HEADS, VB_BLK // SB_HEADS
    return pl.pallas_call(
        functools.partial(_sb_kernel, tq=tq),
        out_shape=jax.ShapeDtypeStruct((N * L, D_B), BF16),
        grid=(N, NH_B // SB_HEADS, nq),
        in_specs=[pl.BlockSpec((tq, W), lambda n, h, i: (row0 // tq + n * nq + i, qb + h)),
                  pl.BlockSpec((L, W), lambda n, h, i: (row0 // L + n, kb + h), pipeline_mode=pl.Buffered(1)),
                  pl.BlockSpec((L, W), lambda n, h, i: (row0 // L + n, vb + h), pipeline_mode=pl.Buffered(1))],
        out_specs=pl.BlockSpec((tq, W), lambda n, h, i: (n * nq + i, h)),
        compiler_params=_params(("parallel", "parallel", "arbitrary")),
        name="sb_attention",
    )(P, P, P)


def _sb_past_kernel(q_ref, k_ref, v_ref, pk_hbm, pv_hbm, o_ref, kbuf, vbuf, ksem, vsem, *, L, tk, n_blk):
    n = pl.program_id(0)

    def copies(blk):
        st = blk * tk
        return ([pltpu.make_async_copy(pk_hbm.at[0, n, pl.ds(st, tk), h, :], kbuf.at[h], ksem)
                 for h in range(NH_B)]
                + [pltpu.make_async_copy(pv_hbm.at[0, n, pl.ds(st, tk), h, :], vbuf.at[h], vsem)
                   for h in range(NH_B)])

    rr = lax.broadcasted_iota(jnp.int32, (L, L), 0)
    cc = lax.broadcasted_iota(jnp.int32, (L, L), 1)
    later_new = _later_matrix(L)
    later = _later_matrix(tk)
    sls = [slice(h * DH_B, (h + 1) * DH_B) for h in range(NH_B)]
    q_bf = [q_ref[:, sl].astype(BF16) for sl in sls]
    accs, Rs = _sb_blocks(q_bf, [k_ref[:, sl] for sl in sls], [v_ref[:, sl] for sl in sls],
                          [jnp.zeros((L, 1), F32)] * NH_B, later_new, cc < rr)

    def rmax(Rs):
        return jnp.max(jnp.concatenate(Rs, axis=1))

    def cond(c):
        return jnp.logical_and(c[0] < n_blk, c[3] > STICK_UNDERFLOW)

    def body(c):
        t, accs, Rs, _ = c
        cps = copies(n_blk - 1 - t)
        for cp in cps:
            cp.start()
        for cp in cps:
            cp.wait()
        pvs, new_Rs = _sb_blocks(q_bf, [kbuf[h] for h in range(NH_B)], [vbuf[h] for h in range(NH_B)],
                                 list(Rs), later, None)
        new_accs = [acc + pv for acc, pv in zip(accs, pvs)]
        return t + 1, tuple(new_accs), tuple(new_Rs), rmax(new_Rs)

    _, accs, _, _ = lax.while_loop(cond, body, (jnp.int32(0), tuple(accs), tuple(Rs), rmax(Rs)))
    for p, sl in enumerate(sls):
        o_ref[:, sl] = accs[p].astype(o_ref.dtype)


def sb_attention_past(P, row0, N, L, past_k, past_v):
    Lp = past_k.shape[2]
    tk = min(Lp, 256)
    blk = lambda c: pl.BlockSpec((L, D_B), lambda n: (row0 // L + n, c))
    return pl.pallas_call(
        functools.partial(_sb_past_kernel, L=L, tk=tk, n_blk=Lp // tk),
        out_shape=jax.ShapeDtypeStruct((N * L, D_B), BF16),
        grid=(N,),
        in_specs=[blk(3), blk(4), blk(5),
                  pl.BlockSpec(memory_space=pl.ANY), pl.BlockSpec(memory_space=pl.ANY)],
        out_specs=pl.BlockSpec((L, D_B), lambda n: (n, 0)),
        scratch_shapes=[pltpu.VMEM((NH_B, tk, DH_B), F32), pltpu.VMEM((NH_B, tk, DH_B), F32),
                        pltpu.SemaphoreType.DMA(()), pltpu.SemaphoreType.DMA(())],
        compiler_params=_params(("arbitrary",)),
        name="sb_attention_past",
    )(P, P, P, past_k, past_v)


def _kv_layout_kernel(kv_ref, kp_ref, ks_ref, vp_ref, vs_ref, *, n_first):
    tm = kv_ref.shape[0]
    first = pl.program_id(0) < n_first

    def relayout(k_out, v_out):
        for which, out in enumerate((k_out, v_out)):
            for h in range(NH_B):
                out[pl.ds(h, tm, stride=NH_B), :] = kv_ref[:, which * D_B + h * DH_B:which * D_B + (h + 1) * DH_B]

    pl.when(first)(functools.partial(relayout, kp_ref, vp_ref))
    pl.when(jnp.logical_not(first))(functools.partial(relayout, ks_ref, vs_ref))


def kv_outputs(P, Tp, tm):
    T = P.shape[0]
    n_first = Tp // tm
    outs = _two_stream_specs(tm * NH_B, DH_B, n_first)
    shapes = [jax.ShapeDtypeStruct((Tp * NH_B, DH_B), F32), jax.ShapeDtypeStruct(((T - Tp) * NH_B, DH_B), F32)]
    return pl.pallas_call(
        functools.partial(_kv_layout_kernel, n_first=n_first),
        out_shape=tuple(shapes + shapes),
        grid=(T // tm,),
        in_specs=[pl.BlockSpec((tm, 2 * D_B), lambda i: (i, 2))],
        out_specs=tuple(outs + outs),
        compiler_params=_params(("arbitrary",)),
        name="kv_layout",
    )(P)


def _merge_kernel(h_ref, ha_ref, hb_ref, wga_ref, wgb_ref, wpa_ref, wpb_ref, o_ref):
    h = h_ref[...]
    ga = jnp.dot(h, wga_ref[...], preferred_element_type=F32)
    gb = jnp.dot(h, wgb_ref[...], preferred_element_type=F32)
    pa = jnp.dot(ha_ref[...], wpa_ref[...], preferred_element_type=F32)
    pb = jnp.dot(hb_ref[...], wpb_ref[...], preferred_element_type=F32)
    o_ref[...] = (jax.nn.sigmoid(ga) * pa + jax.nn.sigmoid(gb) * pb).astype(o_ref.dtype)


def merge(h, row0, ha, hb, wga, wgb, wpa, wpb):
    Tn = ha.shape[0]
    D = h.shape[1]
    tm = _pick(math.gcd(Tn, row0) if row0 else Tn, (1024, 512, 256, 128, 64, 32, 16))
    tn = _pick(D, (512, 256, 128))
    row = lambda i, j: (i, 0)
    colw = lambda i, j: (0, j)
    return pl.pallas_call(
        _merge_kernel,
        out_shape=jax.ShapeDtypeStruct((Tn, D), BF16),
        grid=(Tn // tm, D // tn),
        in_specs=[pl.BlockSpec((tm, D), lambda i, j: (row0 // tm + i, 0)),
                  pl.BlockSpec((tm, D_A), row), pl.BlockSpec((tm, D_B), row),
                  pl.BlockSpec((D, tn), colw), pl.BlockSpec((D, tn), colw),
                  pl.BlockSpec((D_A, tn), colw), pl.BlockSpec((D_B, tn), colw)],
        out_specs=pl.BlockSpec((tm, tn), lambda i, j: (i, j)),
        compiler_params=_params(("parallel", "arbitrary")),
        name="merge",
    )(h, ha, hb, wga, wgb, wpa, wpb)


def _wout_route_kernel(mgp_ref, mgs_ref, xp_ref, xs_ref, wo_ref, g_ref, wr_ref, br_ref, x1_ref, t_ref, r_ref,
                       cnt_ref, run_s, *, n_first):
    i = pl.program_id(0)

    @pl.when(i == 0)
    def _():
        run_s[...] = jnp.zeros_like(run_s)

    x = jnp.where(i < n_first, xp_ref[...], xs_ref[...])
    mg = jnp.where(i < n_first, mgp_ref[...], mgs_ref[...])
    x1 = x + jnp.dot(mg, wo_ref[...], preferred_element_type=F32)
    x1_ref[...] = x1
    t = _rms(x1, g_ref[...])
    _store_row_major(t_ref, t)
    lg = jnp.dot(t.astype(BF16), wr_ref[...], preferred_element_type=F32) + br_ref[...]
    tm = lg.shape[0]
    lane = lax.broadcasted_iota(jnp.int32, (tm, LANES), 1)
    ninf = -jnp.inf
    glm = jnp.where(lane < N_GROUPS, lg, ninf)
    gmax = jnp.max(glm, axis=1, keepdims=True)
    gsel = jnp.min(jnp.where(glm == gmax, lane, LANES), axis=1, keepdims=True)
    p_sel = 1.0 / jnp.sum(jnp.exp(glm - gmax), axis=1, keepdims=True)
    lo = N_GROUPS + E_PER_GROUP * gsel
    elm = jnp.where((lane >= lo) & (lane < lo + E_PER_GROUP), lg, ninf)
    m1 = jnp.max(elm, axis=1, keepdims=True)
    i1 = jnp.min(jnp.where(elm == m1, lane, LANES), axis=1, keepdims=True)
    elm2 = jnp.where(lane == i1, ninf, elm)
    m2 = jnp.max(elm2, axis=1, keepdims=True)
    i2 = jnp.min(jnp.where(elm2 == m2, lane, LANES), axis=1, keepdims=True)
    e2 = jnp.exp(m2 - m1)
    w1 = p_sel / (1.0 + e2)
    w2 = p_sel * e2 / (1.0 + e2)
    e1 = (lane == i1 - N_GROUPS).astype(F32)
    e2h = (lane == i2 - N_GROUPS).astype(F32)
    both = e1 + e2h
    rr = lax.broadcasted_iota(jnp.int32, (tm, tm), 0)
    cc = lax.broadcasted_iota(jnp.int32, (tm, tm), 1)
    before = jnp.dot((cc < rr).astype(BF16), both.astype(BF16), preferred_element_type=F32) + run_s[...]
    rank1 = jnp.sum(e1 * before, axis=1, keepdims=True)
    rank2 = jnp.sum(e2h * before, axis=1, keepdims=True)
    run_s[...] = run_s[...] + jnp.sum(both, axis=0, keepdims=True)
    cnt_ref[...] = run_s[...]
    r = jnp.where(lane == 0, (i1 - N_GROUPS).astype(F32),
        jnp.where(lane == 1, (i2 - N_GROUPS).astype(F32),
        jnp.where(lane == 2, w1, jnp.where(lane == 3, w2,
        jnp.where(lane == 4, rank1, jnp.where(lane == 5, rank2, 0.0))))))
    r_ref[...] = r


def wout_route(mg_p, mg_s, xp, xs, wo, g_ffn, wr, br, tm):
    (Tp, D), Ts = xp.shape, xs.shape[0]
    T = Tp + Ts
    row = lambda i: (i, 0)
    fix = lambda i: (0, 0)
    return pl.pallas_call(
        functools.partial(_wout_route_kernel, n_first=Tp // tm),
        out_shape=(jax.ShapeDtypeStruct((T, D), F32), jax.ShapeDtypeStruct((T * ROW_PITCH, LANES), F32),
                   jax.ShapeDtypeStruct((T, LANES), F32), jax.ShapeDtypeStruct((1, LANES), F32)),
        grid=(T // tm,),
        in_specs=_two_stream_specs(tm, D, Tp // tm) + _two_stream_specs(tm, D, Tp // tm)
                 + [pl.BlockSpec((D, D), fix), pl.BlockSpec((1, D), fix), pl.BlockSpec((D, LANES), fix),
                    pl.BlockSpec((1, LANES), fix)],
        out_specs=(pl.BlockSpec((tm, D), row), pl.BlockSpec((tm * ROW_PITCH, LANES), row),
                   pl.BlockSpec((tm, LANES), row),
                   pl.BlockSpec((1, LANES), fix)),
        scratch_shapes=[pltpu.VMEM((1, LANES), F32)],
        compiler_params=_params(("arbitrary",)),
        name="wout_route",
    )(mg_p, mg_s, xp, xs, wo, g_ffn.reshape(1, D), wr, br)


IDX_RING = 8


def _expert_kernel(te_ref, nu_ref, src_hbm, dst_hbm, t_hbm, wg_ref, wu_ref, wd_ref, y_hbm,
                   sidx, didx, xb0, xb1, yb0, yb1, wg_s, wu_s, wd_s, gsem, ssem, isem, *, dump0):
    i = pl.program_id(0)
    nu = nu_ref[0]
    TE = EXPERT_TILE
    xb, yb = (xb0, xb1), (yb0, yb1)

    def table_copies(tile, seq):
        base = (seq & (IDX_RING - 1)) * TE
        return (pltpu.make_async_copy(src_hbm.at[pl.ds(tile * TE, TE)], sidx.at[pl.ds(base, TE)], isem.at[0]),
                pltpu.make_async_copy(dst_hbm.at[pl.ds(tile * TE, TE)], didx.at[pl.ds(base, TE)], isem.at[1]))

    def gather(seq, par):
        base = (seq & (IDX_RING - 1)) * TE
        return [pltpu.make_async_copy(t_hbm.at[pl.ds(sidx[base + r], ROW_CHUNKS), :],
                                      xb[par].at[pl.ds(r * ROW_PITCH, ROW_CHUNKS), :],
                                      gsem.at[par]) for r in range(TE)]

    def scatter(seq, par):
        base = (seq & (IDX_RING - 1)) * TE
        return [pltpu.make_async_copy(yb[par].at[pl.ds(r * ROW_PITCH, ROW_PITCH), :],
                                      y_hbm.at[pl.ds(didx[base + r], ROW_PITCH), :],
                                      ssem.at[par]) for r in range(TE)]

    def start(cps):
        rows = len(cps) == TE
        for n, cp in enumerate(cps):
            cp.start(priority=n % 2 if rows else 0)

    def wait(cps):
        for cp in cps:
            cp.wait()

    last = nu - 1

    @pl.when(i == 0)
    def _():
        first = table_copies(0, 0)
        start(first)
        wait(first)
        start(table_copies(jnp.minimum(1, last), 1))
        start(gather(0, 0))
        yb0[...] = jnp.zeros_like(yb0)
        yb1[...] = jnp.zeros_like(yb1)
        for r in range(TE):
            didx[(IDX_RING - 1) * TE + r] = (dump0 + TE + r) * ROW_PITCH
        fill = pltpu.make_async_copy(yb1, y_hbm.at[pl.ds(dump0 * ROW_PITCH, TE * ROW_PITCH), :], ssem.at[0])
        fill.start()
        fill.wait()

    @pl.when((i < nu) & ((i == 0) | (te_ref[i] != te_ref[jnp.maximum(i - 1, 0)])))
    def _():
        wg_s[...] = wg_ref[...].astype(BF16)
        wu_s[...] = wu_ref[...].astype(BF16)
        wd_s[...] = wd_ref[...].astype(BF16)

    def step(par):
        wait(table_copies(jnp.minimum(i + 1, last), i + 1))
        start(table_copies(jnp.minimum(i + 2, last), i + 2))

        @pl.when(i >= 1)
        def _():
            wait(scatter(i - 2, par))

        wait(gather(i, par))
        start(scatter(i - 1, 1 - par))
        start(gather(i + 1, 1 - par))
        x = _load_row_major(xb[par], TE).astype(BF16)
        a = jnp.dot(x, wg_s[...], preferred_element_type=F32)
        u = jnp.dot(x, wu_s[...], preferred_element_type=F32)
        hid = (a * jax.nn.sigmoid(a) * u).astype(BF16)
        y = jnp.dot(hid, wd_s[...], preferred_element_type=F32)
        for j in range(ROW_CHUNKS):
            yb[par][pl.ds(j, TE, stride=ROW_PITCH), :] = y[:, j * LANES:(j + 1) * LANES]

        @pl.when(i == last)
        def _():
            start(scatter(i, par))
            wait(scatter(i - 1, 1 - par))
            wait(scatter(i, par))
            wait(gather(i + 1, 1 - par))
            wait(table_copies(last, i + 2))

    for par in (0, 1):
        pl.when((i < nu) & (i % 2 == par))(functools.partial(step, par))


def experts(t, slot_src, slot_dst, tile_expert, n_used, w_gate, w_up, w_down):
    T = t.shape[0] // ROW_PITCH
    D, F = w_gate.shape[-2:]
    nt = slot_src.shape[0] // EXPERT_TILE
    wmap = lambda i, te, nu: (te[i], 0, 0)
    any_spec = pl.BlockSpec(memory_space=pl.ANY)
    tile_buf = pltpu.VMEM((EXPERT_TILE * ROW_PITCH, LANES), F32)
    grid_spec = pltpu.PrefetchScalarGridSpec(
        num_scalar_prefetch=2,
        grid=(nt,),
        in_specs=[any_spec, any_spec, any_spec,
                  pl.BlockSpec((None, D, F), wmap), pl.BlockSpec((None, D, F), wmap),
                  pl.BlockSpec((None, F, D), wmap)],
        out_specs=any_spec,
        scratch_shapes=[pltpu.SMEM((IDX_RING * EXPERT_TILE,), jnp.int32),
                        pltpu.SMEM((IDX_RING * EXPERT_TILE,), jnp.int32),
                        tile_buf, tile_buf, tile_buf, tile_buf,
                        pltpu.VMEM((D, F), BF16), pltpu.VMEM((D, F), BF16), pltpu.VMEM((F, D), BF16),
                        pltpu.SemaphoreType.DMA((2,)), pltpu.SemaphoreType.DMA((2,)),
                        pltpu.SemaphoreType.DMA((2,))],
    )
    return pl.pallas_call(
        functools.partial(_expert_kernel, dump0=2 * T),
        out_shape=jax.ShapeDtypeStruct(((2 * T + 2 * EXPERT_TILE) * ROW_PITCH, LANES), F32),
        grid_spec=grid_spec,
        compiler_params=_params(("arbitrary",)),
        name="moe_experts",
    )(tile_expert, n_used, slot_src, slot_dst, t, w_gate.reshape(N_EXPERTS, D, F),
      w_up.reshape(N_EXPERTS, D, F), w_down.reshape(N_EXPERTS, F, D))


def _combine_kernel(x1_ref, y0_ref, y1_ref, r_ref, pp_ref, ps_ref, wple_ref, wpg_ref, gple_ref, gfin_ref,
                    yp_ref, ys_ref, *, n_first):
    i = pl.program_id(0)
    p = jnp.where(i < n_first, pp_ref[...], ps_ref[...])
    r = r_ref[...]
    tm = r.shape[0]
    x2 = x1_ref[...] + r[:, 2:3] * _load_row_major(y0_ref, tm) + r[:, 3:4] * _load_row_major(y1_ref, tm)
    hp = _rms(x2, gple_ref[...]).astype(BF16)
    gate = jax.nn.sigmoid(jnp.dot(hp, wpg_ref[...], preferred_element_type=F32))
    emb = jnp.dot(p.astype(BF16), wple_ref[...], preferred_element_type=F32)
    y = _rms(x2 + emb * gate, gfin_ref[...])

    @pl.when(i < n_first)
    def _():
        yp_ref[...] = y

    @pl.when(i >= n_first)
    def _():
        ys_ref[...] = y


def combine(y_flat, x1, route, p_p, p_s, w_ple, w_ple_gate, g_ple, g_final, tm):
    T, D = x1.shape
    row = lambda i: (i, 0)
    row1 = lambda i: (i + T // tm, 0)
    fix = lambda i: (0, 0)
    Tp = p_p.shape[0]
    n_first = Tp // tm
    return pl.pallas_call(
        functools.partial(_combine_kernel, n_first=n_first),
        out_shape=(jax.ShapeDtypeStruct((Tp, D), F32), jax.ShapeDtypeStruct((T - Tp, D), F32)),
        grid=(T // tm,),
        in_specs=[pl.BlockSpec((tm, D), row), pl.BlockSpec((tm * ROW_PITCH, LANES), row),
                  pl.BlockSpec((tm * ROW_PITCH, LANES), row1),
                  pl.BlockSpec((tm, LANES), row)] + _two_stream_specs(tm, p_p.shape[1], n_first)
                 + [pl.BlockSpec(w_ple.shape, fix), pl.BlockSpec((D, D), fix),
                  pl.BlockSpec((1, D), fix), pl.BlockSpec((1, D), fix)],
        out_specs=tuple(_two_stream_specs(tm, D, n_first)),
        compiler_params=_params(("arbitrary",)),
        name="moe_combine_ple",
    )(x1, y_flat, y_flat, route, p_p, p_s, w_ple, w_ple_gate, g_ple.reshape(1, D), g_final.reshape(1, D))


def _routing_tables(route, counts, n_tiles):
    ids = route[:, 0:2].astype(jnp.int32).reshape(-1)
    rank = route[:, 4:6].astype(jnp.int32).reshape(-1)
    counts = counts.astype(jnp.int32)
    tiles = (counts + EXPERT_TILE - 1) // EXPERT_TILE
    tile_end = jnp.cumsum(tiles)
    tile_start = tile_end - tiles
    onehot = ids[:, None] == jnp.arange(N_EXPERTS, dtype=jnp.int32)[None, :]
    pos = jnp.sum(jnp.where(onehot, tile_start[None, :], 0), axis=1) * EXPERT_TILE + rank
    slot_assign = jnp.full((n_tiles * EXPERT_TILE,), -1, jnp.int32).at[pos].set(
        jnp.arange(ids.shape[0], dtype=jnp.int32), unique_indices=True)
    n_assign = ids.shape[0]
    slot = jnp.arange(n_tiles * EXPERT_TILE, dtype=jnp.int32)
    token = jnp.maximum(slot_assign, 0) // 2
    slot_src = token * ROW_PITCH
    dump = n_assign + ((slot // EXPERT_TILE) % 2) * EXPERT_TILE + slot % EXPERT_TILE
    slot_dst = jnp.where(slot_assign >= 0, (slot_assign % 2) * (n_assign // 2) + token, dump) * ROW_PITCH
    n_used = tile_end[-1]
    tidx = jnp.minimum(jnp.arange(n_tiles, dtype=jnp.int32), n_used - 1)
    tile_expert = jnp.sum((tidx[:, None] >= tile_end[None, :]).astype(jnp.int32), axis=1)
    return slot_src, slot_dst, tile_expert.astype(jnp.int32), n_used.reshape(1).astype(jnp.int32)


def kernel(x_prompt, x_sample, cache_k, cache_v, state_conv, state_C, state_n, state_m, p_prompt, p_sample,
           g_mix, w_in, b_if, conv_w, conv_b, g_head_a, w_proj_a, w_proj_b, w_out, g_ffn, w_route_g,
           b_route_g, w_route_e, b_route_e, w_exp_gate, w_exp_up, w_exp_down, g_ple, w_ple, w_ple_gate,
           g_final):
    assert w_in.shape[0] == 1, "single layer"
    B, S, D = x_prompt.shape
    DB, DS, _ = x_sample.shape
    Tp, Ts = B * S, DB * DS
    T = Tp + Ts
    xp, xs = x_prompt.reshape(Tp, D), x_sample.reshape(Ts, D)
    tm = _pick(math.gcd(Tp, Ts), (256, 128, 64, 32, 16, 8))

    wi = w_in[0]
    g0 = 2 * QK_A + D_A
    m0 = g0 + 2 * NH_A
    e0 = m0 + D_A + 3 * D_B
    w_main = jnp.concatenate([wi[:, :g0], wi[:, m0:e0]], axis=1).astype(BF16)
    w_gates = jnp.zeros((D, LANES), F32).at[:, :2 * NH_A].set(wi[:, g0:m0]).astype(BF16)
    wga = wi[:, e0:e0 + D].astype(BF16)
    wgb = wi[:, e0 + D:].astype(BF16)

    h = rms_cast(xp, xs, g_mix[0], tm)
    P = matmul(h, w_main, "proj_main")
    G = matmul(h, w_gates, "proj_gates")

    zeros = functools.partial(jnp.zeros, dtype=F32)
    ha_p, C_p, n_p, m_p, cv_p = mlstm(P, G, 0, B, S, b_if[0], conv_w[0], conv_b[0], g_head_a[0],
                                      zeros((B, CONV_W - 1, 2 * QK_A)), zeros((B, NH_A, DQK_A, DV_A)),
                                      zeros((B, NH_A, DQK_A)), zeros((B, NH_A)))
    ha_s, C_s, n_s, m_s, cv_s = mlstm(P, G, Tp, DB, DS, b_if[0], conv_w[0], conv_b[0], g_head_a[0],
                                      state_conv[0], state_C[0], state_n[0], state_m[0])
    hb_p = sb_attention(P, 0, B, S)
    hb_s = sb_attention_past(P, Tp, DB, DS, cache_k, cache_v)
    wpa, wpb = w_proj_a[0].astype(BF16), w_proj_b[0].astype(BF16)
    mg_p = merge(h, 0, ha_p, hb_p, wga, wgb, wpa, wpb)
    mg_s = merge(h, Tp, ha_s, hb_s, wga, wgb, wpa, wpb)

    wr = jnp.zeros((D, LANES), F32)
    wr = wr.at[:, :N_GROUPS].set(w_route_g[0])
    wr = wr.at[:, N_GROUPS:N_GROUPS + N_EXPERTS].set(
        jnp.transpose(w_route_e[0], (1, 0, 2)).reshape(D, N_EXPERTS))
    br = jnp.zeros((1, LANES), F32)
    br = br.at[0, :N_GROUPS].set(b_route_g[0])
    br = br.at[0, N_GROUPS:N_GROUPS + N_EXPERTS].set(b_route_e[0].reshape(-1))
    x1, t, route, counts = wout_route(mg_p, mg_s, xp, xs, w_out[0].astype(BF16), g_ffn[0], wr.astype(BF16), br, tm)

    n_tiles = (2 * T + N_EXPERTS * (EXPERT_TILE - 1)) // EXPERT_TILE + 1
    slot_src, slot_dst, tile_expert, n_used = _routing_tables(route, counts[0, :N_EXPERTS], n_tiles)
    y_flat = experts(t, slot_src, slot_dst, tile_expert, n_used, w_exp_gate[0], w_exp_up[0], w_exp_down[0])
    y_p, y_s = combine(y_flat, x1, route, p_prompt[0].reshape(Tp, -1), p_sample[0].reshape(Ts, -1),
                       w_ple[0].astype(BF16), w_ple_gate[0].astype(BF16), g_ple[0], g_final, tm)

    k_p, k_s, v_p, v_s = kv_outputs(P, Tp, tm)
    return (y_p.reshape(B, S, D), y_s.reshape(DB, DS, D),
            k_p.reshape(1, B, S, NH_B, DH_B), v_p.reshape(1, B, S, NH_B, DH_B),
            cv_p[None], C_p[None], n_p[None], m_p[None],
            k_s.reshape(1, DB, DS, NH_B, DH_B), v_s.reshape(1, DB, DS, NH_B, DH_B),
            cv_s[None], C_s[None], n_s[None], m_s[None])
```

```python
import functools
import math

import jax
import jax.numpy as jnp
from jax import lax
from jax.experimental import pallas as pl
from jax.experimental.pallas import tpu as pltpu

F32 = jnp.float32
BF16 = jnp.bfloat16
EPS = 1e-6

CHUNK = 128
NH_A, DQK_A, DV_A = 4, 128, 256
QK_A, D_A = NH_A * DQK_A, NH_A * DV_A
CONV_W = 4
NH_B, DH_B = 8, 128
D_B = NH_B * DH_B
N_GROUPS, E_PER_GROUP = 4, 8
N_EXPERTS = N_GROUPS * E_PER_GROUP
LANES = 128
EXPERT_TILE = 256
VMEM_LIMIT = 56 * 1024 * 1024


def _pick(n, cands):
    for c in cands:
        if n % c == 0:
            return c
    raise ValueError(f"no tile for {n} in {cands}")


def _params(sem):
    return pltpu.CompilerParams(dimension_semantics=sem, vmem_limit_bytes=VMEM_LIMIT)


def _rms(x, g):
    return x * lax.rsqrt(jnp.mean(x * x, axis=-1, keepdims=True) + EPS) * g


ROW_CHUNKS = 2048 // LANES
ROW_PITCH = 17


def _store_row_major(ref, x):
    rows = x.shape[0]
    for j in range(ROW_PITCH):
        piece = x[:, j * LANES:(j + 1) * LANES] if j < ROW_CHUNKS else jnp.zeros((rows, LANES), x.dtype)
        ref[pl.ds(j, rows, stride=ROW_PITCH), :] = piece


def _load_row_major(ref, rows):
    return jnp.concatenate([ref[pl.ds(j, rows, stride=ROW_PITCH), :] for j in range(ROW_CHUNKS)], axis=1)


def _log_sigmoid(x):
    return jnp.minimum(x, 0.0) - jnp.log(1.0 + jnp.exp(-jnp.abs(x)))


def _two_stream_specs(tm, D, n_first):
    return [pl.BlockSpec((tm, D), lambda i: (jnp.minimum(i, n_first - 1), 0)),
            pl.BlockSpec((tm, D), lambda i: (jnp.maximum(i - n_first, 0), 0))]


def _rms_kernel(xp_ref, xs_ref, g_ref, o_ref, *, n_first):
    x = jnp.where(pl.program_id(0) < n_first, xp_ref[...], xs_ref[...])
    o_ref[...] = _rms(x, g_ref[...]).astype(o_ref.dtype)


def rms_cast(xp, xs, g, tm):
    (Tp, D), Ts = xp.shape, xs.shape[0]
    T = Tp + Ts
    return pl.pallas_call(
        functools.partial(_rms_kernel, n_first=Tp // tm),
        out_shape=jax.ShapeDtypeStruct((T, D), BF16),
        grid=(T // tm,),
        in_specs=_two_stream_specs(tm, D, Tp // tm) + [pl.BlockSpec((1, D), lambda i: (0, 0))],
        out_specs=pl.BlockSpec((tm, D), lambda i: (i, 0)),
        compiler_params=_params(("arbitrary",)),
        name="rms_cast",
    )(xp, xs, g.reshape(1, D))


def _mm_kernel(a_ref, w_ref, o_ref):
    o_ref[...] = jnp.dot(a_ref[...], w_ref[...], preferred_element_type=F32)


def matmul(a, w, name):
    T, K = a.shape
    N = w.shape[1]
    tm = _pick(T, (1280, 640, 512, 256, 128, 64, 32, 16))
    tn = _pick(N, (2048, 1024, 512, 256, 128))
    return pl.pallas_call(
        _mm_kernel,
        out_shape=jax.ShapeDtypeStruct((T, N), F32),
        grid=(T // tm, N // tn),
        in_specs=[pl.BlockSpec((tm, K), lambda i, j: (i, 0)),
                  pl.BlockSpec((K, tn), lambda i, j: (0, j))],
        out_specs=pl.BlockSpec((tm, tn), lambda i, j: (i, j)),
        compiler_params=_params(("parallel", "arbitrary")),
        name=name,
    )(a, w)


def _mlstm_kernel(*refs, c, nc, nb):
    qk_refs, v_refs, o_refs, g_refs = (refs[k * nb:(k + 1) * nb] for k in range(4))
    (bif_ref, cw_ref, cb_ref, gh_ref, cbuf_ref, c0_ref, n0_ref, m0_ref,
     h_ref, cout_ref, nout_ref, mout_ref, convout_ref, xbuf, c_s, n_s, m_s) = refs[4 * nb:]
    j = pl.program_id(1)

    @pl.when(j == 0)
    def _():
        c_s[...] = c0_ref[...]
        n_s[...] = n0_ref[...]
        m_s[...] = m0_ref[...]
        xbuf[:, 5:8, :] = cbuf_ref[...]

    row = lax.broadcasted_iota(jnp.int32, (c, c), 0)
    col = lax.broadcasted_iota(jnp.int32, (c, c), 1)
    consts = dict(lane=lax.broadcasted_iota(jnp.int32, (c, LANES), 1), tril=(col <= row).astype(F32),
                  eye=(col == row).astype(F32), triu=(row <= col).astype(F32), causal=col <= row)
    last_rows = [
        _mlstm_chunk(qk_refs[b], v_refs[b], o_refs[b], g_refs[b], bif_ref, cw_ref, cb_ref, gh_ref,
                     h_ref.at[b], xbuf.at[b], c_s.at[b], n_s.at[b], m_s.at[b], consts, c)
        for b in range(nb)]

    @pl.when(j == nc - 1)
    def _():
        cout_ref[...] = c_s[...]
        nout_ref[...] = n_s[...]
        mout_ref[...] = m_s[...]
        for b in range(nb):
            convout_ref[b] = last_rows[b]


def _mlstm_chunk(qk_ref, v_ref, o_ref, g_ref, bif_ref, cw_ref, cb_ref, gh_ref, h_ref, xbuf, c_s, n_s, m_s,
                 consts, c):
    lane, tril, eye, triu, causal = (consts[k] for k in ("lane", "tril", "eye", "triu", "causal"))
    xbuf[8:8 + c, :] = qk_ref[...]
    y = cb_ref[...] + cw_ref[0:1, :] * xbuf[5:5 + c, :]
    for t in range(1, CONV_W):
        y = y + cw_ref[t:t + 1, :] * xbuf[5 + t:5 + t + c, :]
    last_rows = xbuf[c + 5:c + 8, :]
    xbuf[5:8, :] = last_rows
    qk = y * jax.nn.sigmoid(y)

    graw = g_ref[...] + bif_ref[...]
    xg = jnp.where(lane < NH_A, graw, jnp.where(lane < 2 * NH_A, _log_sigmoid(graw), 0.0))
    hp = lax.Precision.HIGHEST
    bc = jnp.dot(tril, xg, precision=hp, preferred_element_type=F32)
    dn0 = (((0,), (0,)), ((), ()))
    xt = lax.dot_general(xg, eye, dn0, precision=hp, preferred_element_type=F32)
    bt = lax.dot_general(xg, triu, dn0, precision=hp, preferred_element_type=F32)

    for h in range(NH_A):
        q = qk[:, h * DQK_A:(h + 1) * DQK_A]
        k = qk[:, QK_A + h * DQK_A:QK_A + (h + 1) * DQK_A] * (DQK_A ** -0.5)
        v = v_ref[:, h * DV_A:(h + 1) * DV_A]
        qb, kb, vb = q.astype(BF16), k.astype(BF16), v.astype(BF16)
        i_col = xg[:, h:h + 1]
        b_col = bc[:, NH_A + h:NH_A + h + 1]
        i_row = xt[h:h + 1, :]
        b_row = bt[NH_A + h:NH_A + h + 1, :]
        m_prev = m_s[0:1, h:h + 1]
        C = c_s[h]
        nvec = n_s[h:h + 1, :]

        dmat = jnp.where(causal, b_col - b_row + i_row, -jnp.inf)
        inter = b_col + m_prev
        m_t = jnp.maximum(inter, jnp.max(dmat, axis=1, keepdims=True))
        e = jnp.exp(dmat - m_t)
        s = lax.dot_general(qb, kb, (((1,), (1,)), ((), ())), preferred_element_type=F32) * e
        w_inter = jnp.exp(inter - m_t)
        num = (jnp.dot(s.astype(BF16), vb, preferred_element_type=F32)
               + w_inter * jnp.dot(qb, C.astype(BF16), preferred_element_type=F32))
        den = jnp.sum(s, axis=1, keepdims=True) + w_inter * jnp.sum(q * nvec, axis=1, keepdims=True)
        hh = num / jnp.maximum(jnp.abs(den), jnp.exp(-m_t))
        hh = hh * lax.rsqrt(jnp.mean(hh * hh, axis=1, keepdims=True) + EPS)
        hh = hh * gh_ref[:, h * DV_A:(h + 1) * DV_A]
        hh = jax.nn.sigmoid(o_ref[:, h * DV_A:(h + 1) * DV_A]) * hh
        h_ref[:, h * DV_A:(h + 1) * DV_A] = hh.astype(h_ref.dtype)

        b_end = b_col[c - 1:c, :]
        dec = b_end - b_col + i_col
        m_new = jnp.maximum(b_end + m_prev, jnp.max(dec, axis=0, keepdims=True))
        wk = jnp.exp(dec - m_new)
        w_old = jnp.exp(b_end + m_prev - m_new)
        kw = k * wk
        c_s[h] = w_old * C + lax.dot_general(kw.astype(BF16), vb, dn0, preferred_element_type=F32)
        n_s[h:h + 1, :] = w_old * nvec + jnp.sum(kw, axis=0, keepdims=True)
        m_s[0:1, h:h + 1] = m_new
    return last_rows


MLSTM_SEQS = 2


def mlstm(P, G, row0, N, L, b_if, conv_w, conv_b, g_head, conv_buf, C0, n0, m0):
    c = min(L, CHUNK)
    nc = L // c
    nb = MLSTM_SEQS if N % MLSTM_SEQS == 0 else 1
    base = row0 // c
    m0p = jnp.zeros((N, 1, LANES), F32).at[:, 0, :NH_A].set(m0)
    bif = jnp.zeros((1, LANES), F32).at[0, :2 * NH_A].set(b_if)

    def rows(width, col):
        return [pl.BlockSpec((c, width), lambda n, j, b=b: (base + (n * nb + b) * nc + j, col))
                for b in range(nb)]
    fix = lambda n, j: (0, 0)
    st4 = lambda n, j: (n, 0, 0, 0)
    st3 = lambda n, j: (n, 0, 0)
    outs = pl.pallas_call(
        functools.partial(_mlstm_kernel, c=c, nc=nc, nb=nb),
        out_shape=(jax.ShapeDtypeStruct((N, L, D_A), BF16),
                   jax.ShapeDtypeStruct((N, NH_A, DQK_A, DV_A), F32),
                   jax.ShapeDtypeStruct((N, NH_A, DQK_A), F32),
                   jax.ShapeDtypeStruct((N, 1, LANES), F32),
                   jax.ShapeDtypeStruct((N, CONV_W - 1, 2 * QK_A), F32)),
        grid=(N // nb, nc),
        in_specs=rows(2 * QK_A, 0) + rows(D_A, 1) + rows(D_A, 2) + rows(LANES, 0)
                 + [pl.BlockSpec((1, LANES), fix),
                    pl.BlockSpec((CONV_W, 2 * QK_A), fix),
                    pl.BlockSpec((1, 2 * QK_A), fix),
                    pl.BlockSpec((1, D_A), fix),
                    pl.BlockSpec((nb, CONV_W - 1, 2 * QK_A), st3),
                    pl.BlockSpec((nb, NH_A, DQK_A, DV_A), st4),
                    pl.BlockSpec((nb, NH_A, DQK_A), st3),
                    pl.BlockSpec((nb, 1, LANES), st3)],
        out_specs=(pl.BlockSpec((nb, c, D_A), lambda n, j: (n, j, 0)),
                   pl.BlockSpec((nb, NH_A, DQK_A, DV_A), st4),
                   pl.BlockSpec((nb, NH_A, DQK_A), st3),
                   pl.BlockSpec((nb, 1, LANES), st3),
                   pl.BlockSpec((nb, CONV_W - 1, 2 * QK_A), st3)),
        scratch_shapes=[pltpu.VMEM((nb, c + 8, 2 * QK_A), F32),
                        pltpu.VMEM((nb, NH_A, DQK_A, DV_A), F32),
                        pltpu.VMEM((nb, NH_A, DQK_A), F32),
                        pltpu.VMEM((nb, 1, LANES), F32)],
        compiler_params=_params(("parallel", "arbitrary")),
        name="mlstm",
    )(*([P] * (3 * nb) + [G] * nb), bif, conv_w, conv_b.reshape(1, -1), g_head.reshape(1, -1),
      conv_buf, C0, n0, m0p)
    h, C1, n1, m1, cv = outs
    return h.reshape(N * L, D_A), C1, n1, m1[:, 0, :NH_A], cv


STICK_UNDERFLOW = -110.0


def _later_matrix(tk):
    jr = lax.broadcasted_iota(jnp.int32, (tk, tk), 0)
    sc = lax.broadcasted_iota(jnp.int32, (tk, tk), 1)
    return (jr > sc).astype(BF16)


def _sb_blocks(q_bfs, k_blks, v_blks, Rs, later, mask):
    tq = q_bfs[0].shape[0]
    zs, l1mbs, parts = [], [], []
    for q_bf, k_blk in zip(q_bfs, k_blks):
        z = lax.dot_general(q_bf, k_blk.astype(BF16), (((1,), (1,)), ((), ())),
                            preferred_element_type=F32) * (DH_B ** -0.5)
        l1mb = -(jnp.maximum(z, 0.0) + jnp.log(1.0 + jnp.exp(-jnp.abs(z))))
        if mask is not None:
            l1mb = jnp.where(mask, l1mb, 0.0)
        hi = l1mb.astype(BF16)
        parts += [hi, (l1mb - hi.astype(F32)).astype(BF16)]
        zs.append(z)
        l1mbs.append(l1mb)
    suffix = jnp.dot(jnp.concatenate(parts, axis=0), later, preferred_element_type=F32)
    pvs, new_Rs = [], []
    for h, (z, l1mb) in enumerate(zip(zs, l1mbs)):
        rest = suffix[2 * h * tq:(2 * h + 1) * tq] + suffix[(2 * h + 1) * tq:(2 * h + 2) * tq] + Rs[h]
        a = jnp.exp(z + l1mb + rest)
        if mask is not None:
            a = jnp.where(mask, a, 0.0)
        pvs.append(jnp.dot(a.astype(BF16), v_blks[h].astype(BF16), preferred_element_type=F32))
        new_Rs.append(Rs[h] + jnp.sum(l1mb, axis=1, keepdims=True))
    return pvs, new_Rs


SB_HEADS = 4


def _sb_kernel(q_ref, k_ref, v_ref, o_ref, *, tq):
    qi = pl.program_id(2)
    rr = lax.broadcasted_iota(jnp.int32, (tq, tq), 0)
    cc = lax.broadcasted_iota(jnp.int32, (tq, tq), 1)
    later = _later_matrix(tq)
    cols = [slice(h * DH_B, (h + 1) * DH_B) for h in range(SB_HEADS)]
    q_bf = [q_ref[:, c].astype(BF16) for c in cols]
    def blocks(st, Rs, mask):
        return _sb_blocks(q_bf, [k_ref[pl.ds(st, tq), c] for c in cols], [v_ref[pl.ds(st, tq), c] for c in cols],
                          Rs, later, mask)

    accs, Rs = blocks(pl.multiple_of(qi * tq, tq), [jnp.zeros((tq, 1), F32)] * SB_HEADS, cc < rr)

    def rmax(Rs):
        return jnp.max(jnp.concatenate(Rs, axis=1))

    def cond(c):
        return jnp.logical_and(c[0] < qi, c[3] > STICK_UNDERFLOW)

    def body(carry):
        t, accs, Rs, _ = carry
        pvs, new_Rs = blocks(pl.multiple_of((qi - 1 - t) * tq, tq), list(Rs), None)
        new_accs = [acc + pv for acc, pv in zip(accs, pvs)]
        return t + 1, tuple(new_accs), tuple(new_Rs), rmax(new_Rs)

    _, accs, _, _ = lax.while_loop(cond, body, (jnp.int32(0), tuple(accs), tuple(Rs), rmax(Rs)))
    for h, c in enumerate(cols):
        o_ref[:, c] = accs[h].astype(o_ref.dtype)


QB_BLK, KB_BLK, VB_BLK = 3 * D_B // DH_B, 4 * D_B // DH_B, 5 * D_B // DH_B


def sb_attention(P, row0, N, L):
    tq = min(L, 256)
    nq = L // tq
    W = SB_HEADS * DH_B
    qb, kb, vb = QB_BLK // SB_HEADS, KB_BLK // SB_HEADS, VB_BLK // SB_HEADS
    return pl.pallas_call(
        functools.partial(_sb_kernel, tq=tq),
        out_shape=jax.ShapeDtypeStruct((N * L, D_B), BF16),
        grid=(N, NH_B // SB_HEADS, nq),
        in_specs=[pl.BlockSpec((tq, W), lambda n, h, i: (row0 // tq + n * nq + i, qb + h)),
                  pl.BlockSpec((L, W), lambda n, h, i: (row0 // L + n, kb + h), pipeline_mode=pl.Buffered(1)),
                  pl.BlockSpec((L, W), lambda n, h, i: (row0 // L + n, vb + h), pipeline_mode=pl.Buffered(1))],
        out_specs=pl.BlockSpec((tq, W), lambda n, h, i: (n * nq + i, h)),
        compiler_params=_params(("parallel", "parallel", "arbitrary")),
        name="sb_attention",
    )(P, P, P)


def _sb_past_kernel(q_ref, k_ref, v_ref, pk_hbm, pv_hbm, o_ref, kbuf, vbuf, ksem, vsem, *, L, tk, n_blk):
    n = pl.program_id(0)

    def copies(blk):
        st = blk * tk
        return ([pltpu.make_async_copy(pk_hbm.at[0, n, pl.ds(st, tk), h, :], kbuf.at[h], ksem)
                 for h in range(NH_B)]
                + [pltpu.make_async_copy(pv_hbm.at[0, n, pl.ds(st, tk), h, :], vbuf.at[h], vsem)
                   for h in range(NH_B)])

    rr = lax.broadcasted_iota(jnp.int32, (L, L), 0)
    cc = lax.broadcasted_iota(jnp.int32, (L, L), 1)
    later_new = _later_matrix(L)
    later = _later_matrix(tk)
    sls = [slice(h * DH_B, (h + 1) * DH_B) for h in range(NH_B)]
    q_bf = [q_ref[:, sl].astype(BF16) for sl in sls]
    accs, Rs = _sb_blocks(q_bf, [k_ref[:, sl] for sl in sls], [v_ref[:, sl] for sl in sls],
                          [jnp.zeros((L, 1), F32)] * NH_B, later_new, cc < rr)

    def rmax(Rs):
        return jnp.max(jnp.concatenate(Rs, axis=1))

    def cond(c):
        return jnp.logical_and(c[0] < n_blk, c[3] > STICK_UNDERFLOW)

    def body(c):
        t, accs, Rs, _ = c
        cps = copies(n_blk - 1 - t)
        for cp in cps:
            cp.start()
        for cp in cps:
            cp.wait()
        pvs, new_Rs = _sb_blocks(q_bf, [kbuf[h] for h in range(NH_B)], [vbuf[h] for h in range(NH_B)],
                                 list(Rs), later, None)
        new_accs = [acc + pv for acc, pv in zip(accs, pvs)]
        return t + 1, tuple(new_accs), tuple(new_Rs), rmax(new_Rs)

    _, accs, _, _ = lax.while_loop(cond, body, (jnp.int32(0), tuple(accs), tuple(Rs), rmax(Rs)))
    for p, sl in enumerate(sls):
        o_ref[:, sl] = accs[p].astype(o_ref.dtype)


def sb_attention_past(P, row0, N, L, past_k, past_v):
    Lp = past_k.shape[2]
    tk = min(Lp, 256)
    blk = lambda c: pl.BlockSpec((L, D_B), lambda n: (row0 // L + n, c))
    return pl.pallas_call(
        functools.partial(_sb_past_kernel, L=L, tk=tk, n_blk=Lp // tk),
        out_shape=jax.ShapeDtypeStruct((N * L, D_B), BF16),
        grid=(N,),
        in_specs=[blk(3), blk(4), blk(5),
                  pl.BlockSpec(memory_space=pl.ANY), pl.BlockSpec(memory_space=pl.ANY)],
        out_specs=pl.BlockSpec((L, D_B), lambda n: (n, 0)),
        scratch_shapes=[pltpu.VMEM((NH_B, tk, DH_B), F32), pltpu.VMEM((NH_B, tk, DH_B), F32),
                        pltpu.SemaphoreType.DMA(()), pltpu.SemaphoreType.DMA(())],
        compiler_params=_params(("arbitrary",)),
        name="sb_attention_past",
    )(P, P, P, past_k, past_v)


def _kv_layout_kernel(kv_ref, kp_ref, ks_ref, vp_ref, vs_ref, *, n_first):
    tm = kv_ref.shape[0]
    first = pl.program_id(0) < n_first

    def relayout(k_out, v_out):
        for which, out in enumerate((k_out, v_out)):
            for h in range(NH_B):
                out[pl.ds(h, tm, stride=NH_B), :] = kv_ref[:, which * D_B + h * DH_B:which * D_B + (h + 1) * DH_B]

    pl.when(first)(functools.partial(relayout, kp_ref, vp_ref))
    pl.when(jnp.logical_not(first))(functools.partial(relayout, ks_ref, vs_ref))


def kv_outputs(P, Tp, tm):
    T = P.shape[0]
    n_first = Tp // tm
    outs = _two_stream_specs(tm * NH_B, DH_B, n_first)
    shapes = [jax.ShapeDtypeStruct((Tp * NH_B, DH_B), F32), jax.ShapeDtypeStruct(((T - Tp) * NH_B, DH_B), F32)]
    return pl.pallas_call(
        functools.partial(_kv_layout_kernel, n_first=n_first),
        out_shape=tuple(shapes + shapes),
        grid=(T // tm,),
        in_specs=[pl.BlockSpec((tm, 2 * D_B), lambda i: (i, 2))],
        out_specs=tuple(outs + outs),
        compiler_params=_params(("arbitrary",)),
        name="kv_layout",
    )(P)


def _merge_kernel(h_ref, ha_ref, hb_ref, wga_ref, wgb_ref, wpa_ref, wpb_ref, o_ref):
    h = h_ref[...]
    ga = jnp.dot(h, wga_ref[...], preferred_element_type=F32)
    gb = jnp.dot(h, wgb_ref[...], preferred_element_type=F32)
    pa = jnp.dot(ha_ref[...], wpa_ref[...], preferred_element_type=F32)
    pb = jnp.dot(hb_ref[...], wpb_ref[...], preferred_element_type=F32)
    o_ref[...] = (jax.nn.sigmoid(ga) * pa + jax.nn.sigmoid(gb) * pb).astype(o_ref.dtype)


def merge(h, row0, ha, hb, wga, wgb, wpa, wpb):
    Tn = ha.shape[0]
    D = h.shape[1]
    tm = _pick(math.gcd(Tn, row0) if row0 else Tn, (512, 256, 128, 64, 32, 16))
    tn = _pick(D, (1024, 512, 256, 128))
    row = lambda i, j: (i, 0)
    colw = lambda i, j: (0, j)
    return pl.pallas_call(
        _merge_kernel,
        out_shape=jax.ShapeDtypeStruct((Tn, D), BF16),
        grid=(Tn // tm, D // tn),
        in_specs=[pl.BlockSpec((tm, D), lambda i, j: (row0 // tm + i, 0)),
                  pl.BlockSpec((tm, D_A), row), pl.BlockSpec((tm, D_B), row),
                  pl.BlockSpec((D, tn), colw), pl.BlockSpec((D, tn), colw),
                  pl.BlockSpec((D_A, tn), colw), pl.BlockSpec((D_B, tn), colw)],
        out_specs=pl.BlockSpec((tm, tn), lambda i, j: (i, j)),
        compiler_params=_params(("parallel", "arbitrary")),
        name="merge",
    )(h, ha, hb, wga, wgb, wpa, wpb)


def _wout_route_kernel(mgp_ref, mgs_ref, xp_ref, xs_ref, wo_ref, g_ref, wr_ref, br_ref, x1_ref, t_ref, r_ref,
                       cnt_ref, run_s, *, n_first):
    i = pl.program_id(0)

    @pl.when(i == 0)
    def _():
        run_s[...] = jnp.zeros_like(run_s)

    x = jnp.where(i < n_first, xp_ref[...], xs_ref[...])
    mg = jnp.where(i < n_first, mgp_ref[...], mgs_ref[...])
    x1 = x + jnp.dot(mg, wo_ref[...], preferred_element_type=F32)
    x1_ref[...] = x1
    t = _rms(x1, g_ref[...])
    _store_row_major(t_ref, t)
    lg = jnp.dot(t.astype(BF16), wr_ref[...], preferred_element_type=F32) + br_ref[...]
    tm = lg.shape[0]
    lane = lax.broadcasted_iota(jnp.int32, (tm, LANES), 1)
    ninf = -jnp.inf
    glm = jnp.where(lane < N_GROUPS, lg, ninf)
    gmax = jnp.max(glm, axis=1, keepdims=True)
    gsel = jnp.min(jnp.where(glm == gmax, lane, LANES), axis=1, keepdims=True)
    p_sel = 1.0 / jnp.sum(jnp.exp(glm - gmax), axis=1, keepdims=True)
    lo = N_GROUPS + E_PER_GROUP * gsel
    elm = jnp.where((lane >= lo) & (lane < lo + E_PER_GROUP), lg, ninf)
    m1 = jnp.max(elm, axis=1, keepdims=True)
    i1 = jnp.min(jnp.where(elm == m1, lane, LANES), axis=1, keepdims=True)
    elm2 = jnp.where(lane == i1, ninf, elm)
    m2 = jnp.max(elm2, axis=1, keepdims=True)
    i2 = jnp.min(jnp.where(elm2 == m2, lane, LANES), axis=1, keepdims=True)
    e2 = jnp.exp(m2 - m1)
    w1 = p_sel / (1.0 + e2)
    w2 = p_sel * e2 / (1.0 + e2)
    e1 = (lane == i1 - N_GROUPS).astype(F32)
    e2h = (lane == i2 - N_GROUPS).astype(F32)
    both = e1 + e2h
    rr = lax.broadcasted_iota(jnp.int32, (tm, tm), 0)
    cc = lax.broadcasted_iota(jnp.int32, (tm, tm), 1)
    before = jnp.dot((cc < rr).astype(BF16), both.astype(BF16), preferred_element_type=F32) + run_s[...]
    rank1 = jnp.sum(e1 * before, axis=1, keepdims=True)
    rank2 = jnp.sum(e2h * before, axis=1, keepdims=True)
    run_s[...] = run_s[...] + jnp.sum(both, axis=0, keepdims=True)
    cnt_ref[...] = run_s[...]
    r = jnp.where(lane == 0, (i1 - N_GROUPS).astype(F32),
        jnp.where(lane == 1, (i2 - N_GROUPS).astype(F32),
        jnp.where(lane == 2, w1, jnp.where(lane == 3, w2,
        jnp.where(lane == 4, rank1, jnp.where(lane == 5, rank2, 0.0))))))
    r_ref[...] = r


def wout_route(mg_p, mg_s, xp, xs, wo, g_ffn, wr, br, tm):
    (Tp, D), Ts = xp.shape, xs.shape[0]
    T = Tp + Ts
    row = lambda i: (i, 0)
    fix = lambda i: (0, 0)
    return pl.pallas_call(
        functools.partial(_wout_route_kernel, n_first=Tp // tm),
        out_shape=(jax.ShapeDtypeStruct((T, D), F32), jax.ShapeDtypeStruct((T * ROW_PITCH, LANES), F32),
                   jax.ShapeDtypeStruct((T, LANES), F32), jax.ShapeDtypeStruct((1, LANES), F32)),
        grid=(T // tm,),
        in_specs=_two_stream_specs(tm, D, Tp // tm) + _two_stream_specs(tm, D, Tp // tm)
                 + [pl.BlockSpec((D, D), fix), pl.BlockSpec((1, D), fix), pl.BlockSpec((D, LANES), fix),
                    pl.BlockSpec((1, LANES), fix)],
        out_specs=(pl.BlockSpec((tm, D), row), pl.BlockSpec((tm * ROW_PITCH, LANES), row),
                   pl.BlockSpec((tm, LANES), row),
                   pl.BlockSpec((1, LANES), fix)),
        scratch_shapes=[pltpu.VMEM((1, LANES), F32)],
        compiler_params=_params(("arbitrary",)),
        name="wout_route",
    )(mg_p, mg_s, xp, xs, wo, g_ffn.reshape(1, D), wr, br)


IDX_RING = 8


def _expert_kernel(te_ref, nu_ref, src_hbm, dst_hbm, t_hbm, wg_ref, wu_ref, wd_ref, y_hbm,
                   sidx, didx, xb0, xb1, yb0, yb1, wg_s, wu_s, wd_s, gsem, ssem, isem, *, dump0):
    i = pl.program_id(0)
    nu = nu_ref[0]
    TE = EXPERT_TILE
    xb, yb = (xb0, xb1), (yb0, yb1)

    def table_copies(tile, seq):
        base = (seq & (IDX_RING - 1)) * TE
        return (pltpu.make_async_copy(src_hbm.at[pl.ds(tile * TE, TE)], sidx.at[pl.ds(base, TE)], isem.at[0]),
                pltpu.make_async_copy(dst_hbm.at[pl.ds(tile * TE, TE)], didx.at[pl.ds(base, TE)], isem.at[1]))

    def gather(seq, par):
        base = (seq & (IDX_RING - 1)) * TE
        return [pltpu.make_async_copy(t_hbm.at[pl.ds(sidx[base + r], ROW_CHUNKS), :],
                                      xb[par].at[pl.ds(r * ROW_PITCH, ROW_CHUNKS), :],
                                      gsem.at[par]) for r in range(TE)]

    def scatter(seq, par):
        base = (seq & (IDX_RING - 1)) * TE
        return [pltpu.make_async_copy(yb[par].at[pl.ds(r * ROW_PITCH, ROW_PITCH), :],
                                      y_hbm.at[pl.ds(didx[base + r], ROW_PITCH), :],
                                      ssem.at[par]) for r in range(TE)]

    def start(cps):
        rows = len(cps) == TE
        for n, cp in enumerate(cps):
            cp.start(priority=n % 2 if rows else 0)

    def wait(cps):
        for cp in cps:
            cp.wait()

    last = nu - 1

    @pl.when(i == 0)
    def _():
        first = table_copies(0, 0)
        start(first)
        wait(first)
        start(table_copies(jnp.minimum(1, last), 1))
        start(gather(0, 0))
        yb0[...] = jnp.zeros_like(yb0)
        yb1[...] = jnp.zeros_like(yb1)
        for r in range(TE):
            didx[(IDX_RING - 1) * TE + r] = (dump0 + TE + r) * ROW_PITCH
        fill = pltpu.make_async_copy(yb1, y_hbm.at[pl.ds(dump0 * ROW_PITCH, TE * ROW_PITCH), :], ssem.at[0])
        fill.start()
        fill.wait()

    @pl.when((i < nu) & ((i == 0) | (te_ref[i] != te_ref[jnp.maximum(i - 1, 0)])))
    def _():
        wg_s[...] = wg_ref[...].astype(BF16)
        wu_s[...] = wu_ref[...].astype(BF16)
        wd_s[...] = wd_ref[...].astype(BF16)

    def step(par):
        wait(table_copies(jnp.minimum(i + 1, last), i + 1))
        start(table_copies(jnp.minimum(i + 2, last), i + 2))

        @pl.when(i >= 1)
        def _():
            wait(scatter(i - 2, par))

        wait(gather(i, par))
        start(scatter(i - 1, 1 - par))
        start(gather(i + 1, 1 - par))
        x = _load_row_major(xb[par], TE).astype(BF16)
        a = jnp.dot(x, wg_s[...], preferred_element_type=F32)
        u = jnp.dot(x, wu_s[...], preferred_element_type=F32)
        hid = (a * jax.nn.sigmoid(a) * u).astype(BF16)
        y = jnp.dot(hid, wd_s[...], preferred_element_type=F32)
        for j in range(ROW_CHUNKS):
            yb[par][pl.ds(j, TE, stride=ROW_PITCH), :] = y[:, j * LANES:(j + 1) * LANES]

        @pl.when(i == last)
        def _():
            start(scatter(i, par))
            wait(scatter(i - 1, 1 - par))
            wait(scatter(i, par))
            wait(gather(i + 1, 1 - par))
            wait(table_copies(last, i + 2))

    for par in (0, 1):
        pl.when((i < nu) & (i % 2 == par))(functools.partial(step, par))


def experts(t, slot_src, slot_dst, tile_expert, n_used, w_gate, w_up, w_down):
    T = t.shape[0] // ROW_PITCH
    D, F = w_gate.shape[-2:]
    nt = slot_src.shape[0] // EXPERT_TILE
    wmap = lambda i, te, nu: (te[i], 0, 0)
    any_spec = pl.BlockSpec(memory_space=pl.ANY)
    tile_buf = pltpu.VMEM((EXPERT_TILE * ROW_PITCH, LANES), F32)
    grid_spec = pltpu.PrefetchScalarGridSpec(
        num_scalar_prefetch=2,
        grid=(nt,),
        in_specs=[any_spec, any_spec, any_spec,
                  pl.BlockSpec((None, D, F), wmap), pl.BlockSpec((None, D, F), wmap),
                  pl.BlockSpec((None, F, D), wmap)],
        out_specs=any_spec,
        scratch_shapes=[pltpu.SMEM((IDX_RING * EXPERT_TILE,), jnp.int32),
                        pltpu.SMEM((IDX_RING * EXPERT_TILE,), jnp.int32),
                        tile_buf, tile_buf, tile_buf, tile_buf,
                        pltpu.VMEM((D, F), BF16), pltpu.VMEM((D, F), BF16), pltpu.VMEM((F, D), BF16),
                        pltpu.SemaphoreType.DMA((2,)), pltpu.SemaphoreType.DMA((2,)),
                        pltpu.SemaphoreType.DMA((2,))],
    )
    return pl.pallas_call(
        functools.partial(_expert_kernel, dump0=2 * T),
        out_shape=jax.ShapeDtypeStruct(((2 * T + 2 * EXPERT_TILE) * ROW_PITCH, LANES), F32),
        grid_spec=grid_spec,
        compiler_params=_params(("arbitrary",)),
        name="moe_experts",
    )(tile_expert, n_used, slot_src, slot_dst, t, w_gate.reshape(N_EXPERTS, D, F),
      w_up.reshape(N_EXPERTS, D, F), w_down.reshape(N_EXPERTS, F, D))


def _combine_kernel(x1_ref, y0_ref, y1_ref, r_ref, pp_ref, ps_ref, wple_ref, wpg_ref, gple_ref, gfin_ref,
                    yp_ref, ys_ref, *, n_first):
    i = pl.program_id(0)
    p = jnp.where(i < n_first, pp_ref[...], ps_ref[...])
    r = r_ref[...]
    tm = r.shape[0]
    x2 = x1_ref[...] + r[:, 2:3] * _load_row_major(y0_ref, tm) + r[:, 3:4] * _load_row_major(y1_ref, tm)
    hp = _rms(x2, gple_ref[...]).astype(BF16)
    gate = jax.nn.sigmoid(jnp.dot(hp, wpg_ref[...], preferred_element_type=F32))
    emb = jnp.dot(p.astype(BF16), wple_ref[...], preferred_element_type=F32)
    y = _rms(x2 + emb * gate, gfin_ref[...])

    @pl.when(i < n_first)
    def _():
        yp_ref[...] = y

    @pl.when(i >= n_first)
    def _():
        ys_ref[...] = y


def combine(y_flat, x1, route, p_p, p_s, w_ple, w_ple_gate, g_ple, g_final, tm):
    T, D = x1.shape
    row = lambda i: (i, 0)
    row1 = lambda i: (i + T // tm, 0)
    fix = lambda i: (0, 0)
    Tp = p_p.shape[0]
    n_first = Tp // tm
    return pl.pallas_call(
        functools.partial(_combine_kernel, n_first=n_first),
        out_shape=(jax.ShapeDtypeStruct((Tp, D), F32), jax.ShapeDtypeStruct((T - Tp, D), F32)),
        grid=(T // tm,),
        in_specs=[pl.BlockSpec((tm, D), row), pl.BlockSpec((tm * ROW_PITCH, LANES), row),
                  pl.BlockSpec((tm * ROW_PITCH, LANES), row1),
                  pl.BlockSpec((tm, LANES), row)] + _two_stream_specs(tm, p_p.shape[1], n_first)
                 + [pl.BlockSpec(w_ple.shape, fix), pl.BlockSpec((D, D), fix),
                  pl.BlockSpec((1, D), fix), pl.BlockSpec((1, D), fix)],
        out_specs=tuple(_two_stream_specs(tm, D, n_first)),
        compiler_params=_params(("arbitrary",)),
        name="moe_combine_ple",
    )(x1, y_flat, y_flat, route, p_p, p_s, w_ple, w_ple_gate, g_ple.reshape(1, D), g_final.reshape(1, D))


def _routing_tables(route, counts, n_tiles):
    ids = route[:, 0:2].astype(jnp.int32).reshape(-1)
    rank = route[:, 4:6].astype(jnp.int32).reshape(-1)
    counts = counts.astype(jnp.int32)
    tiles = (counts + EXPERT_TILE - 1) // EXPERT_TILE
    tile_end = jnp.cumsum(tiles)
    tile_start = tile_end - tiles
    onehot = ids[:, None] == jnp.arange(N_EXPERTS, dtype=jnp.int32)[None, :]
    pos = jnp.sum(jnp.where(onehot, tile_start[None, :], 0), axis=1) * EXPERT_TILE + rank
    slot_assign = jnp.full((n_tiles * EXPERT_TILE,), -1, jnp.int32).at[pos].set(
        jnp.arange(ids.shape[0], dtype=jnp.int32), unique_indices=True)
    n_assign = ids.shape[0]
    slot = jnp.arange(n_tiles * EXPERT_TILE, dtype=jnp.int32)
    token = jnp.maximum(slot_assign, 0) // 2
    slot_src = token * ROW_PITCH
    dump = n_assign + ((slot // EXPERT_TILE) % 2) * EXPERT_TILE + slot % EXPERT_TILE
    slot_dst = jnp.where(slot_assign >= 0, (slot_assign % 2) * (n_assign // 2) + token, dump) * ROW_PITCH
    n_used = tile_end[-1]
    tidx = jnp.minimum(jnp.arange(n_tiles, dtype=jnp.int32), n_used - 1)
    tile_expert = jnp.sum((tidx[:, None] >= tile_end[None, :]).astype(jnp.int32), axis=1)
    return slot_src, slot_dst, tile_expert.astype(jnp.int32), n_used.reshape(1).astype(jnp.int32)


def kernel(x_prompt, x_sample, cache_k, cache_v, state_conv, state_C, state_n, state_m, p_prompt, p_sample,
           g_mix, w_in, b_if, conv_w, conv_b, g_head_a, w_proj_a, w_proj_b, w_out, g_ffn, w_route_g,
           b_route_g, w_route_e, b_route_e, w_exp_gate, w_exp_up, w_exp_down, g_ple, w_ple, w_ple_gate,
           g_final):
    assert w_in.shape[0] == 1, "single layer"
    B, S, D = x_prompt.shape
    DB, DS, _ = x_sample.shape
    Tp, Ts = B * S, DB * DS
    T = Tp + Ts
    xp, xs = x_prompt.reshape(Tp, D), x_sample.reshape(Ts, D)
    tm = _pick(math.gcd(Tp, Ts), (256, 128, 64, 32, 16, 8))

    wi = w_in[0]
    g0 = 2 * QK_A + D_A
    m0 = g0 + 2 * NH_A
    e0 = m0 + D_A + 3 * D_B
    w_main = jnp.concatenate([wi[:, :g0], wi[:, m0:e0]], axis=1).astype(BF16)
    w_gates = jnp.zeros((D, LANES), F32).at[:, :2 * NH_A].set(wi[:, g0:m0]).astype(BF16)
    wga = wi[:, e0:e0 + D].astype(BF16)
    wgb = wi[:, e0 + D:].astype(BF16)

    h = rms_cast(xp, xs, g_mix[0], tm)
    P = matmul(h, w_main, "proj_main")
    G = matmul(h, w_gates, "proj_gates")

    zeros = functools.partial(jnp.zeros, dtype=F32)
    ha_p, C_p, n_p, m_p, cv_p = mlstm(P, G, 0, B, S, b_if[0], conv_w[0], conv_b[0], g_head_a[0],
                                      zeros((B, CONV_W - 1, 2 * QK_A)), zeros((B, NH_A, DQK_A, DV_A)),
                                      zeros((B, NH_A, DQK_A)), zeros((B, NH_A)))
    ha_s, C_s, n_s, m_s, cv_s = mlstm(P, G, Tp, DB, DS, b_if[0], conv_w[0], conv_b[0], g_head_a[0],
                                      state_conv[0], state_C[0], state_n[0], state_m[0])
    hb_p = sb_attention(P, 0, B, S)
    hb_s = sb_attention_past(P, Tp, DB, DS, cache_k, cache_v)
    wpa, wpb = w_proj_a[0].astype(BF16), w_proj_b[0].astype(BF16)
    mg_p = merge(h, 0, ha_p, hb_p, wga, wgb, wpa, wpb)
    mg_s = merge(h, Tp, ha_s, hb_s, wga, wgb, wpa, wpb)

    wr = jnp.zeros((D, LANES), F32)
    wr = wr.at[:, :N_GROUPS].set(w_route_g[0])
    wr = wr.at[:, N_GROUPS:N_GROUPS + N_EXPERTS].set(
        jnp.transpose(w_route_e[0], (1, 0, 2)).reshape(D, N_EXPERTS))
    br = jnp.zeros((1, LANES), F32)
    br = br.at[0, :N_GROUPS].set(b_route_g[0])
    br = br.at[0, N_GROUPS:N_GROUPS + N_EXPERTS].set(b_route_e[0].reshape(-1))
    x1, t, route, counts = wout_route(mg_p, mg_s, xp, xs, w_out[0].astype(BF16), g_ffn[0], wr.astype(BF16), br, tm)

    n_tiles = (2 * T + N_EXPERTS * (EXPERT_TILE - 1)) // EXPERT_TILE + 1
    slot_src, slot_dst, tile_expert, n_used = _routing_tables(route, counts[0, :N_EXPERTS], n_tiles)
    y_flat = experts(t, slot_src, slot_dst, tile_expert, n_used, w_exp_gate[0], w_exp_up[0], w_exp_down[0])
    y_p, y_s = combine(y_flat, x1, route, p_prompt[0].reshape(Tp, -1), p_sample[0].reshape(Ts, -1),
                       w_ple[0].astype(BF16), w_ple_gate[0].astype(BF16), g_ple[0], g_final, tm)

    k_p, k_s, v_p, v_s = kv_outputs(P, Tp, tm)
    return (y_p.reshape(B, S, D), y_s.reshape(DB, DS, D),
            k_p.reshape(1, B, S, NH_B, DH_B), v_p.reshape(1, B, S, NH_B, DH_B),
            cv_p[None], C_p[None], n_p[None], m_p[None],
            k_s.reshape(1, DB, DS, NH_B, DH_B), v_s.reshape(1, DB, DS, NH_B, DH_B),
            cv_s[None], C_s[None], n_s[None], m_s[None])
```

```python
import functools
import math

import jax
import jax.numpy as jnp
from jax import lax
from jax.experimental import pallas as pl
from jax.experimental.pallas import tpu as pltpu

F32 = jnp.float32
BF16 = jnp.bfloat16
EPS = 1e-6

CHUNK = 128
NH_A, DQK_A, DV_A = 4, 128, 256
QK_A, D_A = NH_A * DQK_A, NH_A * DV_A
CONV_W = 4
NH_B, DH_B = 8, 128
D_B = NH_B * DH_B
N_GROUPS, E_PER_GROUP = 4, 8
N_EXPERTS = N_GROUPS * E_PER_GROUP
LANES = 128
EXPERT_TILE = 256
VMEM_LIMIT = 56 * 1024 * 1024


def _pick(n, cands):
    for c in cands:
        if n % c == 0:
            return c
    raise ValueError(f"no tile for {n} in {cands}")


def _params(sem):
    return pltpu.CompilerParams(dimension_semantics=sem, vmem_limit_bytes=VMEM_LIMIT)


def _rms(x, g):
    return x * lax.rsqrt(jnp.mean(x * x, axis=-1, keepdims=True) + EPS) * g


ROW_CHUNKS = 2048 // LANES
ROW_PITCH = 17


def _store_row_major(ref, x):
    rows = x.shape[0]
    for j in range(ROW_PITCH):
        piece = x[:, j * LANES:(j + 1) * LANES] if j < ROW_CHUNKS else jnp.zeros((rows, LANES), x.dtype)
        ref[pl.ds(j, rows, stride=ROW_PITCH), :] = piece


def _load_row_major(ref, rows):
    return jnp.concatenate([ref[pl.ds(j, rows, stride=ROW_PITCH), :] for j in range(ROW_CHUNKS)], axis=1)


def _log_sigmoid(x):
    return jnp.minimum(x, 0.0) - jnp.log(1.0 + jnp.exp(-jnp.abs(x)))


def _two_stream_specs(tm, D, n_first):
    return [pl.BlockSpec((tm, D), lambda i: (jnp.minimum(i, n_first - 1), 0)),
            pl.BlockSpec((tm, D), lambda i: (jnp.maximum(i - n_first, 0), 0))]


def _rms_kernel(xp_ref, xs_ref, g_ref, o_ref, *, n_first):
    x = jnp.where(pl.program_id(0) < n_first, xp_ref[...], xs_ref[...])
    o_ref[...] = _rms(x, g_ref[...]).astype(o_ref.dtype)


def rms_cast(xp, xs, g, tm):
    (Tp, D), Ts = xp.shape, xs.shape[0]
    T = Tp + Ts
    return pl.pallas_call(
        functools.partial(_rms_kernel, n_first=Tp // tm),
        out_shape=jax.ShapeDtypeStruct((T, D), BF16),
        grid=(T // tm,),
        in_specs=_two_stream_specs(tm, D, Tp // tm) + [pl.BlockSpec((1, D), lambda i: (0, 0))],
        out_specs=pl.BlockSpec((tm, D), lambda i: (i, 0)),
        compiler_params=_params(("arbitrary",)),
        name="rms_cast",
    )(xp, xs, g.reshape(1, D))


def _mm_kernel(a_ref, w_ref, o_ref):
    o_ref[...] = jnp.dot(a_ref[...], w_ref[...], preferred_element_type=F32)


def matmul(a, w, name):
    T, K = a.shape
    N = w.shape[1]
    tm = _pick(T, (1280, 640, 512, 256, 128, 64, 32, 16))
    tn = _pick(N, (2048, 1024, 512, 256, 128))
    return pl.pallas_call(
        _mm_kernel,
        out_shape=jax.ShapeDtypeStruct((T, N), F32),
        grid=(T // tm, N // tn),
        in_specs=[pl.BlockSpec((tm, K), lambda i, j: (i, 0)),
                  pl.BlockSpec((K, tn), lambda i, j: (0, j))],
        out_specs=pl.BlockSpec((tm, tn), lambda i, j: (i, j)),
        compiler_params=_params(("parallel", "arbitrary")),
        name=name,
    )(a, w)


def _mlstm_kernel(*refs, c, nc, nb):
    qk_refs, v_refs, o_refs, g_refs = (refs[k * nb:(k + 1) * nb] for k in range(4))
    (bif_ref, cw_ref, cb_ref, gh_ref, cbuf_ref, c0_ref, n0_ref, m0_ref,
     h_ref, cout_ref, nout_ref, mout_ref, convout_ref, xbuf, c_s, n_s, m_s) = refs[4 * nb:]
    j = pl.program_id(1)

    @pl.when(j == 0)
    def _():
        c_s[...] = c0_ref[...]
        n_s[...] = n0_ref[...]
        m_s[...] = m0_ref[...]
        xbuf[:, 5:8, :] = cbuf_ref[...]

    row = lax.broadcasted_iota(jnp.int32, (c, c), 0)
    col = lax.broadcasted_iota(jnp.int32, (c, c), 1)
    consts = dict(lane=lax.broadcasted_iota(jnp.int32, (c, LANES), 1), tril=(col <= row).astype(F32),
                  eye=(col == row).astype(F32), triu=(row <= col).astype(F32), causal=col <= row)
    last_rows = [
        _mlstm_chunk(qk_refs[b], v_refs[b], o_refs[b], g_refs[b], bif_ref, cw_ref, cb_ref, gh_ref,
                     h_ref.at[b], xbuf.at[b], c_s.at[b], n_s.at[b], m_s.at[b], consts, c)
        for b in range(nb)]

    @pl.when(j == nc - 1)
    def _():
        cout_ref[...] = c_s[...]
        nout_ref[...] = n_s[...]
        mout_ref[...] = m_s[...]
        for b in range(nb):
            convout_ref[b] = last_rows[b]


def _mlstm_chunk(qk_ref, v_ref, o_ref, g_ref, bif_ref, cw_ref, cb_ref, gh_ref, h_ref, xbuf, c_s, n_s, m_s,
                 consts, c):
    lane, tril, eye, triu, causal = (consts[k] for k in ("lane", "tril", "eye", "triu", "causal"))
    xbuf[8:8 + c, :] = qk_ref[...]
    y = cb_ref[...] + cw_ref[0:1, :] * xbuf[5:5 + c, :]
    for t in range(1, CONV_W):
        y = y + cw_ref[t:t + 1, :] * xbuf[5 + t:5 + t + c, :]
    last_rows = xbuf[c + 5:c + 8, :]
    xbuf[5:8, :] = last_rows
    qk = y * jax.nn.sigmoid(y)

    graw = g_ref[...] + bif_ref[...]
    xg = jnp.where(lane < NH_A, graw, jnp.where(lane < 2 * NH_A, _log_sigmoid(graw), 0.0))
    hp = lax.Precision.HIGHEST
    bc = jnp.dot(tril, xg, precision=hp, preferred_element_type=F32)
    dn0 = (((0,), (0,)), ((), ()))
    xt = lax.dot_general(xg, eye, dn0, precision=hp, preferred_element_type=F32)
    bt = lax.dot_general(xg, triu, dn0, precision=hp, preferred_element_type=F32)

    for h in range(NH_A):
        q = qk[:, h * DQK_A:(h + 1) * DQK_A]
        k = qk[:, QK_A + h * DQK_A:QK_A + (h + 1) * DQK_A] * (DQK_A ** -0.5)
        v = v_ref[:, h * DV_A:(h + 1) * DV_A]
        qb, kb, vb = q.astype(BF16), k.astype(BF16), v.astype(BF16)
        i_col = xg[:, h:h + 1]
        b_col = bc[:, NH_A + h:NH_A + h + 1]
        i_row = xt[h:h + 1, :]
        b_row = bt[NH_A + h:NH_A + h + 1, :]
        m_prev = m_s[0:1, h:h + 1]
        C = c_s[h]
        nvec = n_s[h:h + 1, :]

        dmat = jnp.where(causal, b_col - b_row + i_row, -jnp.inf)
        inter = b_col + m_prev
        m_t = jnp.maximum(inter, jnp.max(dmat, axis=1, keepdims=True))
        e = jnp.exp(dmat - m_t)
        s = lax.dot_general(qb, kb, (((1,), (1,)), ((), ())), preferred_element_type=F32) * e
        w_inter = jnp.exp(inter - m_t)
        num = (jnp.dot(s.astype(BF16), vb, preferred_element_type=F32)
               + w_inter * jnp.dot(qb, C.astype(BF16), preferred_element_type=F32))
        den = jnp.sum(s, axis=1, keepdims=True) + w_inter * jnp.sum(q * nvec, axis=1, keepdims=True)
        hh = num / jnp.maximum(jnp.abs(den), jnp.exp(-m_t))
        hh = hh * lax.rsqrt(jnp.mean(hh * hh, axis=1, keepdims=True) + EPS)
        hh = hh * gh_ref[:, h * DV_A:(h + 1) * DV_A]
        hh = jax.nn.sigmoid(o_ref[:, h * DV_A:(h + 1) * DV_A]) * hh
        h_ref[:, h * DV_A:(h + 1) * DV_A] = hh.astype(h_ref.dtype)

        b_end = b_col[c - 1:c, :]
        dec = b_end - b_col + i_col
        m_new = jnp.maximum(b_end + m_prev, jnp.max(dec, axis=0, keepdims=True))
        wk = jnp.exp(dec - m_new)
        w_old = jnp.exp(b_end + m_prev - m_new)
        kw = k * wk
        c_s[h] = w_old * C + lax.dot_general(kw.astype(BF16), vb, dn0, preferred_element_type=F32)
        n_s[h:h + 1, :] = w_old * nvec + jnp.sum(kw, axis=0, keepdims=True)
        m_s[0:1, h:h + 1] = m_new
    return last_rows


MLSTM_SEQS = 2


def mlstm(P, G, row0, N, L, b_if, conv_w, conv_b, g_head, conv_buf, C0, n0, m0):
    c = min(L, CHUNK)
    nc = L // c
    nb = MLSTM_SEQS if N % MLSTM_SEQS == 0 else 1
    base = row0 // c
    m0p = jnp.zeros((N, 1, LANES), F32).at[:, 0, :NH_A].set(m0)
    bif = jnp.zeros((1, LANES), F32).at[0, :2 * NH_A].set(b_if)

    def rows(width, col):
        return [pl.BlockSpec((c, width), lambda n, j, b=b: (base + (n * nb + b) * nc + j, col))
                for b in range(nb)]
    fix = lambda n, j: (0, 0)
    st4 = lambda n, j: (n, 0, 0, 0)
    st3 = lambda n, j: (n, 0, 0)
    outs = pl.pallas_call(
        functools.partial(_mlstm_kernel, c=c, nc=nc, nb=nb),
        out_shape=(jax.ShapeDtypeStruct((N, L, D_A), BF16),
                   jax.ShapeDtypeStruct((N, NH_A, DQK_A, DV_A), F32),
                   jax.ShapeDtypeStruct((N, NH_A, DQK_A), F32),
                   jax.ShapeDtypeStruct((N, 1, LANES), F32),
                   jax.ShapeDtypeStruct((N, CONV_W - 1, 2 * QK_A), F32)),
        grid=(N // nb, nc),
        in_specs=rows(2 * QK_A, 0) + rows(D_A, 1) + rows(D_A, 2) + rows(LANES, 0)
                 + [pl.BlockSpec((1, LANES), fix),
                    pl.BlockSpec((CONV_W, 2 * QK_A), fix),
                    pl.BlockSpec((1, 2 * QK_A), fix),
                    pl.BlockSpec((1, D_A), fix),
                    pl.BlockSpec((nb, CONV_W - 1, 2 * QK_A), st3),
                    pl.BlockSpec((nb, NH_A, DQK_A, DV_A), st4),
                    pl.BlockSpec((nb, NH_A, DQK_A), st3),
                    pl.BlockSpec((nb, 1, LANES), st3)],
        out_specs=(pl.BlockSpec((nb, c, D_A), lambda n, j: (n, j, 0)),
                   pl.BlockSpec((nb, NH_A, DQK_A, DV_A), st4),
                   pl.BlockSpec((nb, NH_A, DQK_A), st3),
                   pl.BlockSpec((nb, 1, LANES), st3),
                   pl.BlockSpec((nb, CONV_W - 1, 2 * QK_A), st3)),
        scratch_shapes=[pltpu.VMEM((nb, c + 8, 2 * QK_A), F32),
                        pltpu.VMEM((nb, NH_A, DQK_A, DV_A), F32),
                        pltpu.VMEM((nb, NH_A, DQK_A), F32),
                        pltpu.VMEM((nb, 1, LANES), F32)],
        compiler_params=_params(("parallel", "arbitrary")),
        name="mlstm",
    )(*([P] * (3 * nb) + [G] * nb), bif, conv_w, conv_b.reshape(1, -1), g_head.reshape(1, -1),
      conv_buf, C0, n0, m0p)
    h, C1, n1, m1, cv = outs
    return h.reshape(N * L, D_A), C1, n1, m1[:, 0, :NH_A], cv


STICK_UNDERFLOW = -110.0


def _later_matrix(tk):
    jr = lax.broadcasted_iota(jnp.int32, (tk, tk), 0)
    sc = lax.broadcasted_iota(jnp.int32, (tk, tk), 1)
    return (jr > sc).astype(BF16)


def _sb_blocks(q_bfs, k_blks, v_blks, Rs, later, mask):
    tq = q_bfs[0].shape[0]
    zs, l1mbs, parts = [], [], []
    for q_bf, k_blk in zip(q_bfs, k_blks):
        z = lax.dot_general(q_bf, k_blk.astype(BF16), (((1,), (1,)), ((), ())),
                            preferred_element_type=F32) * (DH_B ** -0.5)
        l1mb = -(jnp.maximum(z, 0.0) + jnp.log(1.0 + jnp.exp(-jnp.abs(z))))
        if mask is not None:
            l1mb = jnp.where(mask, l1mb, 0.0)
        hi = l1mb.astype(BF16)
        parts += [hi, (l1mb - hi.astype(F32)).astype(BF16)]
        zs.append(z)
        l1mbs.append(l1mb)
    suffix = jnp.dot(jnp.concatenate(parts, axis=0), later, preferred_element_type=F32)
    pvs, new_Rs = [], []
    for h, (z, l1mb) in enumerate(zip(zs, l1mbs)):
        rest = suffix[2 * h * tq:(2 * h + 1) * tq] + suffix[(2 * h + 1) * tq:(2 * h + 2) * tq] + Rs[h]
        a = jnp.exp(z + l1mb + rest)
        if mask is not None:
            a = jnp.where(mask, a, 0.0)
        pvs.append(jnp.dot(a.astype(BF16), v_blks[h].astype(BF16), preferred_element_type=F32))
        new_Rs.append(Rs[h] + jnp.sum(l1mb, axis=1, keepdims=True))
    return pvs, new_Rs


SB_HEADS = 4


def _sb_kernel(q_ref, k_ref, v_ref, o_ref, *, tq):
    qi = pl.program_id(2)
    rr = lax.broadcasted_iota(jnp.int32, (tq, tq), 0)
    cc = lax.broadcasted_iota(jnp.int32, (tq, tq), 1)
    later = _later_matrix(tq)
    cols = [slice(h * DH_B, (h + 1) * DH_B) for h in range(SB_HEADS)]
    q_bf = [q_ref[:, c].astype(BF16) for c in cols]
    def blocks(st, Rs, mask):
        return _sb_blocks(q_bf, [k_ref[pl.ds(st, tq), c] for c in cols], [v_ref[pl.ds(st, tq), c] for c in cols],
                          Rs, later, mask)

    accs, Rs = blocks(pl.multiple_of(qi * tq, tq), [jnp.zeros((tq, 1), F32)] * SB_HEADS, cc < rr)

    def rmax(Rs):
        return jnp.max(jnp.concatenate(Rs, axis=1))

    def cond(c):
        return jnp.logical_and(c[0] < qi, c[3] > STICK_UNDERFLOW)

    def body(carry):
        t, accs, Rs, _ = carry
        pvs, new_Rs = blocks(pl.multiple_of((qi - 1 - t) * tq, tq), list(Rs), None)
        new_accs = [acc + pv for acc, pv in zip(accs, pvs)]
        return t + 1, tuple(new_accs), tuple(new_Rs), rmax(new_Rs)

    _, accs, _, _ = lax.while_loop(cond, body, (jnp.int32(0), tuple(accs), tuple(Rs), rmax(Rs)))
    for h, c in enumerate(cols):
        o_ref[:, c] = accs[h].astype(o_ref.dtype)


QB_BLK, KB_BLK, VB_BLK = 3 * D_B // DH_B, 4 * D_B // DH_B, 5 * D_B // DH_B


def sb_attention(P, row0, N, L):
    tq = min(L, 256)
    nq = L // tq
    W = SB_HEADS * DH_B
    qb, kb, vb = QB_BLK // SB_HEADS, KB_BLK // SB_HEADS, VB_BLK // SB_HEADS
    return pl.pallas_call(
        functools.partial(_sb_kernel, tq=tq),
        out_shape=jax.ShapeDtypeStruct((N * L, D_B), BF16),
        grid=(N, NH_B // SB_HEADS, nq),
        in_specs=[pl.BlockSpec((tq, W), lambda n, h, i: (row0 // tq + n * nq + i, qb + h)),
                  pl.BlockSpec((L, W), lambda n, h, i: (row0 // L + n, kb + h), pipeline_mode=pl.Buffered(1)),
                  pl.BlockSpec((L, W), lambda n, h, i: (row0 // L + n, vb + h), pipeline_mode=pl.Buffered(1))],
        out_specs=pl.BlockSpec((tq, W), lambda n, h, i: (n * nq + i, h)),
        compiler_params=_params(("parallel", "parallel", "arbitrary")),
        name="sb_attention",
    )(P, P, P)


def _sb_past_kernel(q_ref, k_ref, v_ref, pk_hbm, pv_hbm, o_ref, kbuf, vbuf, ksem, vsem, *, L, tk, n_blk):
    n = pl.program_id(0)

    def copies(blk):
        st = blk * tk
        return ([pltpu.make_async_copy(pk_hbm.at[0, n, pl.ds(st, tk), h, :], kbuf.at[h], ksem)
                 for h in range(NH_B)]
                + [pltpu.make_async_copy(pv_hbm.at[0, n, pl.ds(st, tk), h, :], vbuf.at[h], vsem)
                   for h in range(NH_B)])

    rr = lax.broadcasted_iota(jnp.int32, (L, L), 0)
    cc = lax.broadcasted_iota(jnp.int32, (L, L), 1)
    later_new = _later_matrix(L)
    later = _later_matrix(tk)
    sls = [slice(h * DH_B, (h + 1) * DH_B) for h in range(NH_B)]
    q_bf = [q_ref[:, sl].astype(BF16) for sl in sls]
    accs, Rs = _sb_blocks(q_bf, [k_ref[:, sl] for sl in sls], [v_ref[:, sl] for sl in sls],
                          [jnp.zeros((L, 1), F32)] * NH_B, later_new, cc < rr)

    def rmax(Rs):
        return jnp.max(jnp.concatenate(Rs, axis=1))

    def cond(c):
        return jnp.logical_and(c[0] < n_blk, c[3] > STICK_UNDERFLOW)

    def body(c):
        t, accs, Rs, _ = c
        cps = copies(n_blk - 1 - t)
        for cp in cps:
            cp.start()
        for cp in cps:
            cp.wait()
        pvs, new_Rs = _sb_blocks(q_bf, [kbuf[h] for h in range(NH_B)], [vbuf[h] for h in range(NH_B)],
                                 list(Rs), later, None)
        new_accs = [acc + pv for acc, pv in zip(accs, pvs)]
        return t + 1, tuple(new_accs), tuple(new_Rs), rmax(new_Rs)

    _, accs, _, _ = lax.while_loop(cond, body, (jnp.int32(0), tuple(accs), tuple(Rs), rmax(Rs)))
    for p, sl in enumerate(sls):
        o_ref[:, sl] = accs[p].astype(o_ref.dtype)


def sb_attention_past(P, row0, N, L, past_k, past_v):
    Lp = past_k.shape[2]
    tk = min(Lp, 256)
    blk = lambda c: pl.BlockSpec((L, D_B), lambda n: (row0 // L + n, c))
    return pl.pallas_call(
        functools.partial(_sb_past_kernel, L=L, tk=tk, n_blk=Lp // tk),
        out_shape=jax.ShapeDtypeStruct((N * L, D_B), BF16),
        grid=(N,),
        in_specs=[blk(3), blk(4), blk(5),
                  pl.BlockSpec(memory_space=pl.ANY), pl.BlockSpec(memory_space=pl.ANY)],
        out_specs=pl.BlockSpec((L, D_B), lambda n: (n, 0)),
        scratch_shapes=[pltpu.VMEM((NH_B, tk, DH_B), F32), pltpu.VMEM((NH_B, tk, DH_B), F32),
                        pltpu.SemaphoreType.DMA(()), pltpu.SemaphoreType.DMA(())],
        compiler_params=_params(("arbitrary",)),
        name="sb_attention_past",
    )(P, P, P, past_k, past_v)


def _kv_layout_kernel(kv_ref, kp_ref, ks_ref, vp_ref, vs_ref, *, n_first):
    tm = kv_ref.shape[0]
    first = pl.program_id(0) < n_first

    def relayout(k_out, v_out):
        for which, out in enumerate((k_out, v_out)):
            for h in range(NH_B):
                out[pl.ds(h, tm, stride=NH_B), :] = kv_ref[:, which * D_B + h * DH_B:which * D_B + (h + 1) * DH_B]

    pl.when(first)(functools.partial(relayout, kp_ref, vp_ref))
    pl.when(jnp.logical_not(first))(functools.partial(relayout, ks_ref, vs_ref))


def kv_outputs(P, Tp, tm):
    T = P.shape[0]
    n_first = Tp // tm
    outs = _two_stream_specs(tm * NH_B, DH_B, n_first)
    shapes = [jax.ShapeDtypeStruct((Tp * NH_B, DH_B), F32), jax.ShapeDtypeStruct(((T - Tp) * NH_B, DH_B), F32)]
    return pl.pallas_call(
        functools.partial(_kv_layout_kernel, n_first=n_first),
        out_shape=tuple(shapes + shapes),
        grid=(T // tm,),
        in_specs=[pl.BlockSpec((tm, 2 * D_B), lambda i: (i, 2))],
        out_specs=tuple(outs + outs),
        compiler_params=_params(("arbitrary",)),
        name="kv_layout",
    )(P)


def _merge_kernel(h_ref, ha_ref, hb_ref, wga_ref, wgb_ref, wpa_ref, wpb_ref, o_ref):
    h = h_ref[...]
    ga = jnp.dot(h, wga_ref[...], preferred_element_type=F32)
    gb = jnp.dot(h, wgb_ref[...], preferred_element_type=F32)
    pa = jnp.dot(ha_ref[...], wpa_ref[...], preferred_element_type=F32)
    pb = jnp.dot(hb_ref[...], wpb_ref[...], preferred_element_type=F32)
    o_ref[...] = (jax.nn.sigmoid(ga) * pa + jax.nn.sigmoid(gb) * pb).astype(o_ref.dtype)


def merge(h, row0, ha, hb, wga, wgb, wpa, wpb):
    Tn = ha.shape[0]
    D = h.shape[1]
    tm = _pick(math.gcd(Tn, row0) if row0 else Tn, (512, 256, 128, 64, 32, 16))
    tn = _pick(D, (1024, 512, 256, 128))
    row = lambda i, j: (i, 0)
    colw = lambda i, j: (0, j)
    return pl.pallas_call(
        _merge_kernel,
        out_shape=jax.ShapeDtypeStruct((Tn, D), BF16),
        grid=(Tn // tm, D // tn),
        in_specs=[pl.BlockSpec((tm, D), lambda i, j: (row0 // tm + i, 0)),
                  pl.BlockSpec((tm, D_A), row), pl.BlockSpec((tm, D_B), row),
                  pl.BlockSpec((D, tn), colw), pl.BlockSpec((D, tn), colw),
                  pl.BlockSpec((D_A, tn), colw), pl.BlockSpec((D_B, tn), colw)],
        out_specs=pl.BlockSpec((tm, tn), lambda i, j: (i, j)),
        compiler_params=_params(("parallel", "arbitrary")),
        name="merge",
    )(h, ha, hb, wga, wgb, wpa, wpb)


def _wout_route_kernel(mgp_ref, mgs_ref, xp_ref, xs_ref, wo_ref, g_ref, wr_ref, br_ref, x1_ref, t_ref, r_ref,
                       cnt_ref, run_s, *, n_first):
    i = pl.program_id(0)

    @pl.when(i == 0)
    def _():
        run_s[...] = jnp.zeros_like(run_s)

    x = jnp.where(i < n_first, xp_ref[...], xs_ref[...])
    mg = jnp.where(i < n_first, mgp_ref[...], mgs_ref[...])
    x1 = x + jnp.dot(mg, wo_ref[...], preferred_element_type=F32)
    x1_ref[...] = x1
    t = _rms(x1, g_ref[...])
    _store_row_major(t_ref, t)
    lg = jnp.dot(t.astype(BF16), wr_ref[...], preferred_element_type=F32) + br_ref[...]
    tm = lg.shape[0]
    lane = lax.broadcasted_iota(jnp.int32, (tm, LANES), 1)
    ninf = -jnp.inf
    glm = jnp.where(lane < N_GROUPS, lg, ninf)
    gmax = jnp.max(glm, axis=1, keepdims=True)
    gsel = jnp.min(jnp.where(glm == gmax, lane, LANES), axis=1, keepdims=True)
    p_sel = 1.0 / jnp.sum(jnp.exp(glm - gmax), axis=1, keepdims=True)
    lo = N_GROUPS + E_PER_GROUP * gsel
    elm = jnp.where((lane >= lo) & (lane < lo + E_PER_GROUP), lg, ninf)
    m1 = jnp.max(elm, axis=1, keepdims=True)
    i1 = jnp.min(jnp.where(elm == m1, lane, LANES), axis=1, keepdims=True)
    elm2 = jnp.where(lane == i1, ninf, elm)
    m2 = jnp.max(elm2, axis=1, keepdims=True)
    i2 = jnp.min(jnp.where(elm2 == m2, lane, LANES), axis=1, keepdims=True)
    e2 = jnp.exp(m2 - m1)
    w1 = p_sel / (1.0 + e2)
    w2 = p_sel * e2 / (1.0 + e2)
    e1 = (lane == i1 - N_GROUPS).astype(F32)
    e2h = (lane == i2 - N_GROUPS).astype(F32)
    both = e1 + e2h
    rr = lax.broadcasted_iota(jnp.int32, (tm, tm), 0)
    cc = lax.broadcasted_iota(jnp.int32, (tm, tm), 1)
    before = jnp.dot((cc < rr).astype(BF16), both.astype(BF16), preferred_element_type=F32) + run_s[...]
    rank1 = jnp.sum(e1 * before, axis=1, keepdims=True)
    rank2 = jnp.sum(e2h * before, axis=1, keepdims=True)
    run_s[...] = run_s[...] + jnp.sum(both, axis=0, keepdims=True)
    cnt_ref[...] = run_s[...]
    r = jnp.where(lane == 0, (i1 - N_GROUPS).astype(F32),
        jnp.where(lane == 1, (i2 - N_GROUPS).astype(F32),
        jnp.where(lane == 2, w1, jnp.where(lane == 3, w2,
        jnp.where(lane == 4, rank1, jnp.where(lane == 5, rank2, 0.0))))))
    r_ref[...] = r


def wout_route(mg_p, mg_s, xp, xs, wo, g_ffn, wr, br, tm):
    (Tp, D), Ts = xp.shape, xs.shape[0]
    T = Tp + Ts
    row = lambda i: (i, 0)
    fix = lambda i: (0, 0)
    return pl.pallas_call(
        functools.partial(_wout_route_kernel, n_first=Tp // tm),
        out_shape=(jax.ShapeDtypeStruct((T, D), F32), jax.ShapeDtypeStruct((T * ROW_PITCH, LANES), F32),
                   jax.ShapeDtypeStruct((T, LANES), F32), jax.ShapeDtypeStruct((1, LANES), F32)),
        grid=(T // tm,),
        in_specs=_two_stream_specs(tm, D, Tp // tm) + _two_stream_specs(tm, D, Tp // tm)
                 + [pl.BlockSpec((D, D), fix), pl.BlockSpec((1, D), fix), pl.BlockSpec((D, LANES), fix),
                    pl.BlockSpec((1, LANES), fix)],
        out_specs=(pl.BlockSpec((tm, D), row), pl.BlockSpec((tm * ROW_PITCH, LANES), row),
                   pl.BlockSpec((tm, LANES), row),
                   pl.BlockSpec((1, LANES), fix)),
        scratch_shapes=[pltpu.VMEM((1, LANES), F32)],
        compiler_params=_params(("arbitrary",)),
        name="wout_route",
    )(mg_p, mg_s, xp, xs, wo, g_ffn.reshape(1, D), wr, br)


IDX_RING = 8


def _expert_kernel(te_ref, nu_ref, src_hbm, dst_hbm, t_hbm, wg_ref, wu_ref, wd_ref, y_hbm,
                   sidx, didx, xb0, xb1, yb0, yb1, wg_s, wu_s, wd_s, gsem, ssem, isem, *, dump0):
    i = pl.program_id(0)
    nu = nu_ref[0]
    TE = EXPERT_TILE
    xb, yb = (xb0, xb1), (yb0, yb1)

    def table_copies(tile, seq):
        base = (seq & (IDX_RING - 1)) * TE
        return (pltpu.make_async_copy(src_hbm.at[pl.ds(tile * TE, TE)], sidx.at[pl.ds(base, TE)], isem.at[0]),
                pltpu.make_async_copy(dst_hbm.at[pl.ds(tile * TE, TE)], didx.at[pl.ds(base, TE)], isem.at[1]))

    def gather(seq, par):
        base = (seq & (IDX_RING - 1)) * TE
        return [pltpu.make_async_copy(t_hbm.at[pl.ds(sidx[base + r], ROW_CHUNKS), :],
                                      xb[par].at[pl.ds(r * ROW_PITCH, ROW_CHUNKS), :],
                                      gsem.at[par]) for r in range(TE)]

    def scatter(seq, par):
        base = (seq & (IDX_RING - 1)) * TE
        return [pltpu.make_async_copy(yb[par].at[pl.ds(r * ROW_PITCH, ROW_PITCH), :],
                                      y_hbm.at[pl.ds(didx[base + r], ROW_PITCH), :],
                                      ssem.at[par]) for r in range(TE)]

    def start(cps, queue=0):
        for cp in cps:
            cp.start(priority=queue)

    def wait(cps):
        for cp in cps:
            cp.wait()

    last = nu - 1

    @pl.when(i == 0)
    def _():
        first = table_copies(0, 0)
        start(first)
        wait(first)
        start(table_copies(jnp.minimum(1, last), 1))
        start(gather(0, 0))
        yb0[...] = jnp.zeros_like(yb0)
        yb1[...] = jnp.zeros_like(yb1)
        for r in range(TE):
            didx[(IDX_RING - 1) * TE + r] = (dump0 + TE + r) * ROW_PITCH
        fill = pltpu.make_async_copy(yb1, y_hbm.at[pl.ds(dump0 * ROW_PITCH, TE * ROW_PITCH), :], ssem.at[0])
        fill.start()
        fill.wait()

    @pl.when((i < nu) & ((i == 0) | (te_ref[i] != te_ref[jnp.maximum(i - 1, 0)])))
    def _():
        wg_s[...] = wg_ref[...].astype(BF16)
        wu_s[...] = wu_ref[...].astype(BF16)
        wd_s[...] = wd_ref[...].astype(BF16)

    def step(par):
        wait(table_copies(jnp.minimum(i + 1, last), i + 1))
        start(table_copies(jnp.minimum(i + 2, last), i + 2))

        @pl.when(i >= 1)
        def _():
            wait(scatter(i - 2, par))

        wait(gather(i, par))
        start(gather(i + 1, 1 - par))
        start(scatter(i - 1, 1 - par), queue=1)
        x = _load_row_major(xb[par], TE).astype(BF16)
        a = jnp.dot(x, wg_s[...], preferred_element_type=F32)
        u = jnp.dot(x, wu_s[...], preferred_element_type=F32)
        hid = (a * jax.nn.sigmoid(a) * u).astype(BF16)
        y = jnp.dot(hid, wd_s[...], preferred_element_type=F32)
        for j in range(ROW_CHUNKS):
            yb[par][pl.ds(j, TE, stride=ROW_PITCH), :] = y[:, j * LANES:(j + 1) * LANES]

        @pl.when(i == last)
        def _():
            start(scatter(i, par))
            wait(scatter(i - 1, 1 - par))
            wait(scatter(i, par))
            wait(gather(i + 1, 1 - par))
            wait(table_copies(last, i + 2))

    for par in (0, 1):
        pl.when((i < nu) & (i % 2 == par))(functools.partial(step, par))


def experts(t, slot_src, slot_dst, tile_expert, n_used, w_gate, w_up, w_down):
    T = t.shape[0] // ROW_PITCH
    D, F = w_gate.shape[-2:]
    nt = slot_src.shape[0] // EXPERT_TILE
    wmap = lambda i, te, nu: (te[i], 0, 0)
    any_spec = pl.BlockSpec(memory_space=pl.ANY)
    tile_buf = pltpu.VMEM((EXPERT_TILE * ROW_PITCH, LANES), F32)
    grid_spec = pltpu.PrefetchScalarGridSpec(
        num_scalar_prefetch=2,
        grid=(nt,),
        in_specs=[any_spec, any_spec, any_spec,
                  pl.BlockSpec((None, D, F), wmap), pl.BlockSpec((None, D, F), wmap),
                  pl.BlockSpec((None, F, D), wmap)],
        out_specs=any_spec,
        scratch_shapes=[pltpu.SMEM((IDX_RING * EXPERT_TILE,), jnp.int32),
                        pltpu.SMEM((IDX_RING * EXPERT_TILE,), jnp.int32),
                        tile_buf, tile_buf, tile_buf, tile_buf,
                        pltpu.VMEM((D, F), BF16), pltpu.VMEM((D, F), BF16), pltpu.VMEM((F, D), BF16),
                        pltpu.SemaphoreType.DMA((2,)), pltpu.SemaphoreType.DMA((2,)),
                        pltpu.SemaphoreType.DMA((2,))],
    )
    return pl.pallas_call(
        functools.partial(_expert_kernel, dump0=2 * T),
        out_shape=jax.ShapeDtypeStruct(((2 * T + 2 * EXPERT_TILE) * ROW_PITCH, LANES), F32),
        grid_spec=grid_spec,
        compiler_params=_params(("arbitrary",)),
        name="moe_experts",
    )(tile_expert, n_used, slot_src, slot_dst, t, w_gate.reshape(N_EXPERTS, D, F),
      w_up.reshape(N_EXPERTS, D, F), w_down.reshape(N_EXPERTS, F, D))


def _combine_kernel(x1_ref, y0_ref, y1_ref, r_ref, pp_ref, ps_ref, wple_ref, wpg_ref, gple_ref, gfin_ref,
                    yp_ref, ys_ref, *, n_first):
    i = pl.program_id(0)
    p = jnp.where(i < n_first, pp_ref[...], ps_ref[...])
    r = r_ref[...]
    tm = r.shape[0]
    x2 = x1_ref[...] + r[:, 2:3] * _load_row_major(y0_ref, tm) + r[:, 3:4] * _load_row_major(y1_ref, tm)
    hp = _rms(x2, gple_ref[...]).astype(BF16)
    gate = jax.nn.sigmoid(jnp.dot(hp, wpg_ref[...], preferred_element_type=F32))
    emb = jnp.dot(p.astype(BF16), wple_ref[...], preferred_element_type=F32)
    y = _rms(x2 + emb * gate, gfin_ref[...])

    @pl.when(i < n_first)
    def _():
        yp_ref[...] = y

    @pl.when(i >= n_first)
    def _():
        ys_ref[...] = y


def combine(y_flat, x1, route, p_p, p_s, w_ple, w_ple_gate, g_ple, g_final, tm):
    T, D = x1.shape
    row = lambda i: (i, 0)
    row1 = lambda i: (i + T // tm, 0)
    fix = lambda i: (0, 0)
    Tp = p_p.shape[0]
    n_first = Tp // tm
    return pl.pallas_call(
        functools.partial(_combine_kernel, n_first=n_first),
        out_shape=(jax.ShapeDtypeStruct((Tp, D), F32), jax.ShapeDtypeStruct((T - Tp, D), F32)),
        grid=(T // tm,),
        in_specs=[pl.BlockSpec((tm, D), row), pl.BlockSpec((tm * ROW_PITCH, LANES), row),
                  pl.BlockSpec((tm * ROW_PITCH, LANES), row1),
                  pl.BlockSpec((tm, LANES), row)] + _two_stream_specs(tm, p_p.shape[1], n_first)
                 + [pl.BlockSpec(w_ple.shape, fix), pl.BlockSpec((D, D), fix),
                  pl.BlockSpec((1, D), fix), pl.BlockSpec((1, D), fix)],
        out_specs=tuple(_two_stream_specs(tm, D, n_first)),
        compiler_params=_params(("arbitrary",)),
        name="moe_combine_ple",
    )(x1, y_flat, y_flat, route, p_p, p_s, w_ple, w_ple_gate, g_ple.reshape(1, D), g_final.reshape(1, D))


def _routing_tables(route, counts, n_tiles):
    ids = route[:, 0:2].astype(jnp.int32).reshape(-1)
    rank = route[:, 4:6].astype(jnp.int32).reshape(-1)
    counts = counts.astype(jnp.int32)
    tiles = (counts + EXPERT_TILE - 1) // EXPERT_TILE
    tile_end = jnp.cumsum(tiles)
    tile_start = tile_end - tiles
    onehot = ids[:, None] == jnp.arange(N_EXPERTS, dtype=jnp.int32)[None, :]
    pos = jnp.sum(jnp.where(onehot, tile_start[None, :], 0), axis=1) * EXPERT_TILE + rank
    slot_assign = jnp.full((n_tiles * EXPERT_TILE,), -1, jnp.int32).at[pos].set(
        jnp.arange(ids.shape[0], dtype=jnp.int32), unique_indices=True)
    n_assign = ids.shape[0]
    slot = jnp.arange(n_tiles * EXPERT_TILE, dtype=jnp.int32)
    token = jnp.maximum(slot_assign, 0) // 2
    slot_src = token * ROW_PITCH
    dump = n_assign + ((slot // EXPERT_TILE) % 2) * EXPERT_TILE + slot % EXPERT_TILE
    slot_dst = jnp.where(slot_assign >= 0, (slot_assign % 2) * (n_assign // 2) + token, dump) * ROW_PITCH
    n_used = tile_end[-1]
    tidx = jnp.minimum(jnp.arange(n_tiles, dtype=jnp.int32), n_used - 1)
    tile_expert = jnp.sum((tidx[:, None] >= tile_end[None, :]).astype(jnp.int32), axis=1)
    return slot_src, slot_dst, tile_expert.astype(jnp.int32), n_used.reshape(1).astype(jnp.int32)


def kernel(x_prompt, x_sample, cache_k, cache_v, state_conv, state_C, state_n, state_m, p_prompt, p_sample,
           g_mix, w_in, b_if, conv_w, conv_b, g_head_a, w_proj_a, w_proj_b, w_out, g_ffn, w_route_g,
           b_route_g, w_route_e, b_route_e, w_exp_gate, w_exp_up, w_exp_down, g_ple, w_ple, w_ple_gate,
           g_final):
    assert w_in.shape[0] == 1, "single layer"
    B, S, D = x_prompt.shape
    DB, DS, _ = x_sample.shape
    Tp, Ts = B * S, DB * DS
    T = Tp + Ts
    xp, xs = x_prompt.reshape(Tp, D), x_sample.reshape(Ts, D)
    tm = _pick(math.gcd(Tp, Ts), (256, 128, 64, 32, 16, 8))

    wi = w_in[0]
    g0 = 2 * QK_A + D_A
    m0 = g0 + 2 * NH_A
    e0 = m0 + D_A + 3 * D_B
    w_main = jnp.concatenate([wi[:, :g0], wi[:, m0:e0]], axis=1).astype(BF16)
    w_gates = jnp.zeros((D, LANES), F32).at[:, :2 * NH_A].set(wi[:, g0:m0]).astype(BF16)
    wga = wi[:, e0:e0 + D].astype(BF16)
    wgb = wi[:, e0 + D:].astype(BF16)

    h = rms_cast(xp, xs, g_mix[0], tm)
    P = matmul(h, w_main, "proj_main")
    G = matmul(h, w_gates, "proj_gates")

    zeros = functools.partial(jnp.zeros, dtype=F32)
    ha_p, C_p, n_p, m_p, cv_p = mlstm(P, G, 0, B, S, b_if[0], conv_w[0], conv_b[0], g_head_a[0],
                                      zeros((B, CONV_W - 1, 2 * QK_A)), zeros((B, NH_A, DQK_A, DV_A)),
                                      zeros((B, NH_A, DQK_A)), zeros((B, NH_A)))
    ha_s, C_s, n_s, m_s, cv_s = mlstm(P, G, Tp, DB, DS, b_if[0], conv_w[0], conv_b[0], g_head_a[0],
                                      state_conv[0], state_C[0], state_n[0], state_m[0])
    hb_p = sb_attention(P, 0, B, S)
    hb_s = sb_attention_past(P, Tp, DB, DS, cache_k, cache_v)
    wpa, wpb = w_proj_a[0].astype(BF16), w_proj_b[0].astype(BF16)
    mg_p = merge(h, 0, ha_p, hb_p, wga, wgb, wpa, wpb)
    mg_s = merge(h, Tp, ha_s, hb_s, wga, wgb, wpa, wpb)

    wr = jnp.zeros((D, LANES), F32)
    wr = wr.at[:, :N_GROUPS].set(w_route_g[0])
    wr = wr.at[:, N_GROUPS:N_GROUPS + N_EXPERTS].set(
        jnp.transpose(w_route_e[0], (1, 0, 2)).reshape(D, N_EXPERTS))
    br = jnp.zeros((1, LANES), F32)
    br = br.at[0, :N_GROUPS].set(b_route_g[0])
    br = br.at[0, N_GROUPS:N_GROUPS + N_EXPERTS].set(b_route_e[0].reshape(-1))
    x1, t, route, counts = wout_route(mg_p, mg_s, xp, xs, w_out[0].astype(BF16), g_ffn[0], wr.astype(BF16), br, tm)

    n_tiles = (2 * T + N_EXPERTS * (EXPERT_TILE - 1)) // EXPERT_TILE + 1
    slot_src, slot_dst, tile_expert, n_used = _routing_tables(route, counts[0, :N_EXPERTS], n_tiles)
    y_flat = experts(t, slot_src, slot_dst, tile_expert, n_used, w_exp_gate[0], w_exp_up[0], w_exp_down[0])
    y_p, y_s = combine(y_flat, x1, route, p_prompt[0].reshape(Tp, -1), p_sample[0].reshape(Ts, -1),
                       w_ple[0].astype(BF16), w_ple_gate[0].astype(BF16), g_ple[0], g_final, tm)

    k_p, k_s, v_p, v_s = kv_outputs(P, Tp, tm)
    return (y_p.reshape(B, S, D), y_s.reshape(DB, DS, D),
            k_p.reshape(1, B, S, NH_B, DH_B), v_p.reshape(1, B, S, NH_B, DH_B),
            cv_p[None], C_p[None], n_p[None], m_p[None],
            k_s.reshape(1, DB, DS, NH_B, DH_B), v_s.reshape(1, DB, DS, NH_B, DH_B),
            cv_s[None], C_s[None], n_s[None], m_s[None])
```
